```python
import jax
import jax.numpy as jnp
from jax import lax
import numpy as np

D_MODEL = 1024
BATCH = 2
SEQ = 16384
DEPTH = 4
DEC_BATCH = 16
DEC_SEQ = 2048
PAST_LEN = 128

PLE_DIM = 256
HEAD_DIM = 64
RMS_EPS = 1e-6
RWKV_HEADS = 8
RWKV_W = RWKV_HEADS * HEAD_DIM
DECAY_LORA = 64
ICLR_LORA = 64
GATE_LORA = 128
RWKV_IN = 3 * RWKV_W + 2 * DECAY_LORA + 2 * ICLR_LORA + GATE_LORA
GN_EPS = 64e-5
ATTN_PATTERNS = ((128, 1), (512, 4), (2048, 16))
ATTN_HEADS_PER_GROUP = 4
ATTN_HEADS = ATTN_HEADS_PER_GROUP * len(ATTN_PATTERNS)
ATTN_IN = 3 * ATTN_HEADS * HEAD_DIM
ATTN_OUT = ATTN_HEADS_PER_GROUP * HEAD_DIM
ROPE_THETA = 500000.0
ROT_DIM = HEAD_DIM // 4
NEG_INF = -1e30
AB_IN = RWKV_IN + ATTN_IN
AB_OUT = RWKV_W + ATTN_OUT
POOL_WINDOWS = (2, 4, 8, 16)
POOL_GROUP = 256
POOL_W = len(POOL_WINDOWS) * POOL_GROUP
D_FF = 2816
CONV_WIDTH = 3
N_EVEN = (DEPTH + 1) // 2
N_ODD = DEPTH // 2

kernel_name = 'hybrid_rwkv7_dilattn_pool_encoder'


def _rmsnorm(x, g):
    xf = x.astype(jnp.float32)
    y = xf * lax.rsqrt(jnp.mean(xf * xf, axis=-1, keepdims=True) + RMS_EPS)
    return (y * g.astype(jnp.float32)).astype(x.dtype)


def _centred_shift(z):
    zp = jnp.pad(z, ((0, 0), (1, 1), (0, 0)))
    return 0.5 * (zp[:, :-2] + zp[:, 2:])


def _partial_rotary(t, pos):
    half = ROT_DIM // 2
    inv = jnp.float32(ROPE_THETA) ** (-jnp.arange(half, dtype=jnp.float32) * 2.0 / ROT_DIM)
    ang = pos[:, None] * inv[None, :]
    cos = jnp.cos(ang)[None, :, None, :]
    sin = jnp.sin(ang)[None, :, None, :]
    tf = t.astype(jnp.float32)
    x1 = tf[..., :half]
    x2 = tf[..., half:ROT_DIM]
    return jnp.concatenate([x1 * cos - x2 * sin, x2 * cos + x1 * sin, tf[..., ROT_DIM:]], axis=-1)


def _band_attention(q, k, v, half):
    n, L, hd = q.shape
    qb = half
    nb = -(-L // qb)
    lp = nb * qb
    qf = jnp.pad(q.astype(jnp.float32), ((0, 0), (0, lp - L), (0, 0))).reshape(n, nb, qb, hd)

    def key_windows(t):
        tp = jnp.pad(t.astype(jnp.float32), ((0, 0), (qb, lp - L + qb), (0, 0))).reshape(n, nb + 2, qb, hd)
        return jnp.concatenate([tp[:, :-2], tp[:, 1:-1], tp[:, 2:]], axis=2)

    kw = key_windows(k)
    vw = key_windows(v)
    s = jnp.einsum('nbqd,nbkd->nbqk', qf, kw) * (hd ** -0.5)
    qi = jnp.arange(qb)[:, None]
    kc = jnp.arange(3 * qb)[None, :]
    rel = kc - qb - qi
    kpos = jnp.arange(nb)[:, None, None] * qb + kc[None] - qb
    valid = (jnp.abs(rel) <= half)[None] & (kpos >= 0) & (kpos < L)
    s = jnp.where(valid[None], s, NEG_INF)
    m = jnp.max(s, axis=-1, keepdims=True)
    pexp = jnp.exp(s - m)
    den = jnp.sum(pexp, axis=-1, keepdims=True)
    o = jnp.einsum('nbqk,nbkd->nbqd', pexp, vw) / den
    lse = (m + jnp.log(den))[..., 0]
    return o.reshape(n, lp, hd)[:, :L], lse.reshape(n, lp)[:, :L]


def _dilated_attention(q, k, v, dil, half):
    b, s, h, hd = q.shape
    L = s // dil

    def to_sub(t):
        return t.reshape(b, L, dil, h, hd).transpose(0, 2, 3, 1, 4).reshape(b * dil * h, L, hd)

    o, lse = _band_attention(to_sub(q), to_sub(k), to_sub(v), half)
    o = o.reshape(b, dil, h, L, hd).transpose(0, 3, 1, 2, 4).reshape(b, s, h, hd)
    lse = lse.reshape(b, dil, h, L).transpose(0, 3, 1, 2).reshape(b, s, h)
    return o, lse


def _dilated_mixture_attention(z):
    b, s, _ = z.shape
    q, k, v = jnp.split(z, 3, axis=-1)
    shp = (b, s, ATTN_HEADS, HEAD_DIM)
    pos = jnp.arange(s, dtype=jnp.float32)
    q = _partial_rotary(q.reshape(shp), pos)
    k = _partial_rotary(k.reshape(shp), pos)
    v = v.reshape(shp)
    outs, lses = [], []
    for g, (win, dil) in enumerate(ATTN_PATTERNS):
        hs = slice(g * ATTN_HEADS_PER_GROUP, (g + 1) * ATTN_HEADS_PER_GROUP)
        o, lse = _dilated_attention(q[:, :, hs], k[:, :, hs], v[:, :, hs], dil, win // (2 * dil))
        outs.append(o)
        lses.append(lse)
    alpha = jax.nn.softmax(jnp.stack(lses), axis=0)[..., None]
    y = jnp.sum(alpha * jnp.stack(outs), axis=0)
    return y.reshape(b, s, ATTN_OUT)


def _wkv_scan(decay, kk, kka, k, v, r, reverse):
    b, s, h, n = r.shape
    xs = tuple(jnp.moveaxis(t, 1, 0) for t in (decay, kk, kka, k, v, r))

    def step(state, inp):
        w_t, kk_t, kka_t, k_t, v_t, r_t = inp
        sa = jnp.einsum('bhvk,bhk->bhv', state, kk_t)
        state = (state * w_t[:, :, None, :] - sa[..., None] * kka_t[:, :, None, :]
                 + v_t[..., None] * k_t[:, :, None, :])
        return state, jnp.einsum('bhvk,bhk->bhv', state, r_t)

    s0 = jnp.zeros((b, h, n, n), jnp.float32)
    _, y = lax.scan(step, s0, xs, reverse=reverse)
    return jnp.moveaxis(y, 0, 1)


def _rwkv7_bidir(z, mu, w0, w_up, a0, a_up, g_up, k_k, k_a, r_k, ln_g, ln_b):
    b, s, _ = z.shape
    z = (z + mu * (_centred_shift(z) - z)).astype(jnp.float32)
    r = z[..., :RWKV_W]
    k = z[..., RWKV_W:2 * RWKV_W]
    v = z[..., 2 * RWKV_W:3 * RWKV_W]
    o = 3 * RWKV_W
    wd = z[..., o:o + 2 * DECAY_LORA].reshape(b, s, 2, DECAY_LORA)
    o += 2 * DECAY_LORA
    ad = z[..., o:o + 2 * ICLR_LORA].reshape(b, s, 2, ICLR_LORA)
    o += 2 * ICLR_LORA
    gd = z[..., o:]
    w_log = w0 + jnp.einsum('bsdl,dlc->bsdc', jnp.tanh(wd), w_up)
    decay = jnp.exp(-jnp.exp(-jax.nn.softplus(-w_log) - 0.5))
    a = jax.nn.sigmoid(a0 + jnp.einsum('bsdl,dlc->bsdc', ad, a_up))
    g = jax.nn.sigmoid(gd) @ g_up

    def heads(t):
        return t.reshape(t.shape[:-1] + (RWKV_HEADS, HEAD_DIM))

    kk = heads(k * k_k)
    kk = kk / jnp.sqrt(jnp.sum(kk * kk, axis=-1, keepdims=True) + 1e-12)
    kdir = heads(k[:, :, None, :] * (1.0 + (a - 1.0) * k_a))
    adir = heads(a)
    dec = heads(decay)
    rh = heads(r)
    vh = heads(v)
    y = (_wkv_scan(dec[:, :, 0], kk, kk * adir[:, :, 0], kdir[:, :, 0], vh, rh, False)
         + _wkv_scan(dec[:, :, 1], kk, kk * adir[:, :, 1], kdir[:, :, 1], vh, rh, True))
    mean = jnp.mean(y, axis=-1, keepdims=True)
    var = jnp.mean((y - mean) ** 2, axis=-1, keepdims=True)
    yn = ((y - mean) * lax.rsqrt(var + GN_EPS)).reshape(b, s, RWKV_W) * ln_g + ln_b
    bonus = jnp.sum(rh * (kdir[:, :, 0] + kdir[:, :, 1]) * r_k, axis=-1, keepdims=True) * vh
    return (yn + bonus.reshape(b, s, RWKV_W)) * g


def _mixer_ab(h, prm, j):
    z = h @ prm['ab_w_in'][j]
    ya = _rwkv7_bidir(z[..., :RWKV_IN], prm['rwkv_mu'][j], prm['rwkv_w0'][j], prm['rwkv_w_up'][j],
                      prm['rwkv_a0'][j], prm['rwkv_a_up'][j], prm['rwkv_g_up'][j], prm['rwkv_k_k'][j],
                      prm['rwkv_k_a'][j], prm['rwkv_r_k'][j], prm['rwkv_ln_g'][j], prm['rwkv_ln_b'][j])
    yb = _dilated_mixture_attention(z[..., RWKV_IN:])
    y = jnp.concatenate([ya.astype(h.dtype), yb.astype(h.dtype)], axis=-1)
    return y @ prm['ab_w_out'][j]


def _pool_mixer(h, w_in, w_group, scale, w_out):
    u = (h @ w_in).astype(jnp.float32)
    b, s, _ = u.shape
    cs = jnp.concatenate([jnp.zeros((b, 1, POOL_W), jnp.float32), jnp.cumsum(u, axis=1)], axis=1)
    t = jnp.arange(s)
    ys = []
    for g, win in enumerate(POOL_WINDOWS):
        rad = win // 2
        lo = jnp.clip(t - rad, 0, s)
        hi = jnp.clip(t + rad + 1, 0, s)
        cols = slice(g * POOL_GROUP, (g + 1) * POOL_GROUP)
        csg = cs[..., cols]
        mean = (jnp.take(csg, hi, axis=1) - jnp.take(csg, lo, axis=1)) / (hi - lo).astype(jnp.float32)[None, :, None]
        ys.append((mean - u[..., cols]) @ w_group[g])
    y = jnp.concatenate(ys, axis=-1) * scale
    return y.astype(h.dtype) @ w_out


def _conv_ffn(h, w_up, conv_w, conv_b, w_down):
    u = h @ w_up
    up = jnp.pad(u, ((0, 0), (1, 1), (0, 0)))
    c = up[:, :-2] * conv_w[0] + up[:, 1:-1] * conv_w[1] + up[:, 2:] * conv_w[2] + conv_b
    gate, val = jnp.split(c, 2, axis=-1)
    return (jax.nn.gelu(gate) * val) @ w_down


def _ple(h, p_i, w_proj, w_gate, b_gate):
    return jax.nn.sigmoid(h @ w_gate + b_gate) * (p_i @ w_proj)


def _trunk(x, p, prm):
    for i in range(DEPTH):
        j = i // 2
        h = _rmsnorm(x, prm['norm_mix_g'][i])
        if i % 2 == 0:
            y = _mixer_ab(h, prm, j)
        else:
            y = _pool_mixer(h, prm['c_w_in'][j], prm['c_w_group'][j], prm['c_scale'][j], prm['c_w_out'][j])
        x = x + y.astype(x.dtype)
        x = x + _conv_ffn(_rmsnorm(x, prm['norm_ffn_g'][i]), prm['ffn_w_up'][i], prm['ffn_conv_w'][i],
                          prm['ffn_conv_b'][i], prm['ffn_w_down'][i]).astype(x.dtype)
        x = x + _ple(_rmsnorm(x, prm['norm_ple_g'][i]), p[i], prm['ple_w_proj'][i],
                     prm['ple_w_gate'][i], prm['ple_b_gate'][i]).astype(x.dtype)
    return _rmsnorm(x, prm['norm_final_g'])


def setup_inputs(seed: int = 0) -> dict:
    key = jax.random.key(seed)
    ks = iter(jax.random.split(key, 40))
    f32 = jnp.float32

    def nrm(shape, scale):
        return scale * jax.random.normal(next(ks), shape, f32)

    def uni(shape, lo, hi):
        return jax.random.uniform(next(ks), shape, f32, lo, hi)

    return {
        'x_prompt': nrm((BATCH, SEQ, D_MODEL), 1.0),
        'x_sample': nrm((DEC_BATCH, DEC_SEQ, D_MODEL), 1.0),
        'p_prompt': nrm((DEPTH, BATCH, SEQ, PLE_DIM), 1.0),
        'p_sample': nrm((DEPTH, DEC_BATCH, DEC_SEQ, PLE_DIM), 1.0),
        'ab_w_in': nrm((N_EVEN, D_MODEL, AB_IN), D_MODEL ** -0.5),
        'ab_w_out': nrm((N_EVEN, AB_OUT, D_MODEL), AB_OUT ** -0.5),
        'rwkv_mu': uni((N_EVEN, RWKV_IN), 0.0, 1.0),
        'rwkv_w0': uni((N_EVEN, 2, RWKV_W), -6.0, -1.0),
        'rwkv_w_up': nrm((N_EVEN, 2, DECAY_LORA, RWKV_W), 0.1 * DECAY_LORA ** -0.5),
        'rwkv_a0': nrm((N_EVEN, 2, RWKV_W), 0.1),
        'rwkv_a_up': nrm((N_EVEN, 2, ICLR_LORA, RWKV_W), 0.5 * ICLR_LORA ** -0.5),
        'rwkv_g_up': nrm((N_EVEN, GATE_LORA, RWKV_W), GATE_LORA ** -0.5),
        'rwkv_k_k': 0.85 + nrm((N_EVEN, RWKV_W), 0.05),
        'rwkv_k_a': 1.0 + nrm((N_EVEN, RWKV_W), 0.05),
        'rwkv_r_k': nrm((N_EVEN, RWKV_HEADS, HEAD_DIM), 0.1),
        'rwkv_ln_g': 1.0 + nrm((N_EVEN, RWKV_W), 0.02),
        'rwkv_ln_b': nrm((N_EVEN, RWKV_W), 0.02),
        'c_w_in': nrm((N_ODD, D_MODEL, POOL_W), D_MODEL ** -0.5),
        'c_w_group': nrm((N_ODD, len(POOL_WINDOWS), POOL_GROUP, POOL_GROUP), POOL_GROUP ** -0.5),
        'c_scale': 1.0 + nrm((N_ODD, POOL_W), 0.02),
        'c_w_out': nrm((N_ODD, POOL_W, D_MODEL), POOL_W ** -0.5),
        'norm_mix_g': 1.0 + nrm((DEPTH, D_MODEL), 0.02),
        'norm_ffn_g': 1.0 + nrm((DEPTH, D_MODEL), 0.02),
        'norm_ple_g': 1.0 + nrm((DEPTH, D_MODEL), 0.02),
        'norm_final_g': 1.0 + nrm((D_MODEL,), 0.02),
        'ffn_w_up': nrm((DEPTH, D_MODEL, 2 * D_FF), D_MODEL ** -0.5),
        'ffn_conv_w': nrm((DEPTH, CONV_WIDTH, 2 * D_FF), CONV_WIDTH ** -0.5),
        'ffn_conv_b': nrm((DEPTH, 2 * D_FF), 0.02),
        'ffn_w_down': nrm((DEPTH, D_FF, D_MODEL), D_FF ** -0.5),
        'ple_w_proj': nrm((DEPTH, PLE_DIM, D_MODEL), PLE_DIM ** -0.5),
        'ple_w_gate': nrm((DEPTH, D_MODEL, D_MODEL), D_MODEL ** -0.5),
        'ple_b_gate': nrm((DEPTH, D_MODEL), 0.02),
    }


def reference(x_prompt, x_sample, p_prompt, p_sample, ab_w_in, ab_w_out, rwkv_mu, rwkv_w0, rwkv_w_up,
              rwkv_a0, rwkv_a_up, rwkv_g_up, rwkv_k_k, rwkv_k_a, rwkv_r_k, rwkv_ln_g, rwkv_ln_b,
              c_w_in, c_w_group, c_scale, c_w_out, norm_mix_g, norm_ffn_g, norm_ple_g, norm_final_g,
              ffn_w_up, ffn_conv_w, ffn_conv_b, ffn_w_down, ple_w_proj, ple_w_gate, ple_b_gate):
    prm = {
        'ab_w_in': ab_w_in, 'ab_w_out': ab_w_out, 'rwkv_mu': rwkv_mu, 'rwkv_w0': rwkv_w0,
        'rwkv_w_up': rwkv_w_up, 'rwkv_a0': rwkv_a0, 'rwkv_a_up': rwkv_a_up, 'rwkv_g_up': rwkv_g_up,
        'rwkv_k_k': rwkv_k_k, 'rwkv_k_a': rwkv_k_a, 'rwkv_r_k': rwkv_r_k, 'rwkv_ln_g': rwkv_ln_g,
        'rwkv_ln_b': rwkv_ln_b, 'c_w_in': c_w_in, 'c_w_group': c_w_group, 'c_scale': c_scale,
        'c_w_out': c_w_out, 'norm_mix_g': norm_mix_g, 'norm_ffn_g': norm_ffn_g, 'norm_ple_g': norm_ple_g,
        'norm_final_g': norm_final_g, 'ffn_w_up': ffn_w_up, 'ffn_conv_w': ffn_conv_w,
        'ffn_conv_b': ffn_conv_b, 'ffn_w_down': ffn_w_down, 'ple_w_proj': ple_w_proj,
        'ple_w_gate': ple_w_gate, 'ple_b_gate': ple_b_gate,
    }
    y_prompt = _trunk(x_prompt, p_prompt, prm)
    y_sample = _trunk(x_sample, p_sample, prm)
    return (y_prompt, y_sample)
```

```python
import functools
import math

import jax
import jax.numpy as jnp
from jax import lax
from jax.experimental import pallas as pl
from jax.experimental.pallas import tpu as pltpu

D_MODEL = 1024
DEPTH = 4
PLE_DIM = 256
HEAD_DIM = 64
RMS_EPS = 1e-6
RWKV_HEADS = 8
RWKV_W = RWKV_HEADS * HEAD_DIM
DECAY_LORA = 64
ICLR_LORA = 64
GATE_LORA = 128
RWKV_IN = 3 * RWKV_W + 2 * DECAY_LORA + 2 * ICLR_LORA + GATE_LORA
GN_EPS = 64e-5
ATTN_PATTERNS = ((128, 1), (512, 4), (2048, 16))
ATTN_HEADS_PER_GROUP = 4
ATTN_HEADS = ATTN_HEADS_PER_GROUP * len(ATTN_PATTERNS)
ATTN_OUT = ATTN_HEADS_PER_GROUP * HEAD_DIM
ATTN_GROUP_IN = 3 * ATTN_OUT
ATTN_HALF = 64
ROPE_THETA = 500000.0
ROT_DIM = HEAD_DIM // 4
NEG_INF = -1e30
AB_OUT = RWKV_W + ATTN_OUT
POOL_WINDOWS = (2, 4, 8, 16)
POOL_GROUP = 256
POOL_W = len(POOL_WINDOWS) * POOL_GROUP
D_FF = 2816

LANES = 128
SUBLANES = 8
HALO = SUBLANES
CHUNK = 64
VMEM_LIMIT = 56 * 1024 * 1024

F32 = jnp.float32
BF16 = jnp.bfloat16


def _dot(a, b):
    return jnp.dot(a, b, preferred_element_type=F32)


def _dot_nt(a, b):
    return lax.dot_general(a, b, (((1,), (1,)), ((), ())), preferred_element_type=F32)


def _split(x):
    hi = x.astype(BF16)
    lo = (x - hi.astype(F32)).astype(BF16)
    return hi, lo


def _dot3(ap, bp):
    return _dot(ap[0], bp[0]) + (_dot(ap[0], bp[1]) + _dot(ap[1], bp[0]))


def _dot3_nt(ap, bp):
    return _dot_nt(ap[0], bp[0]) + (_dot_nt(ap[0], bp[1]) + _dot_nt(ap[1], bp[0]))


def _rms(x, g):
    return x * lax.rsqrt(jnp.mean(x * x, axis=-1, keepdims=True) + RMS_EPS) * g


def _sigmoid(x):
    return 1.0 / (1.0 + jnp.exp(-x))


def _head_ones():
    r = lax.broadcasted_iota(jnp.int32, (LANES, LANES), 0) // HEAD_DIM
    c = lax.broadcasted_iota(jnp.int32, (LANES, LANES), 1) // HEAD_DIM
    return jnp.where(r == c, 1.0, 0.0).astype(BF16)


def _head_sum(x, ones):
    outs = []
    for j in range(x.shape[1] // LANES):
        hi, lo = _split(x[:, j * LANES:(j + 1) * LANES])
        outs.append(_dot(hi, ones) + _dot(lo, ones))
    return jnp.concatenate(outs, axis=1)


def _shift_rows(z, prev_row, next_row):
    n = z.shape[0]
    row = lax.broadcasted_iota(jnp.int32, z.shape, 0)
    zp = jnp.where(row == 0, prev_row, pltpu.roll(z, 1, 0))
    zn = jnp.where(row == n - 1, next_row, pltpu.roll(z, n - 1, 0))
    return zp, zn


def _cparams(sem):
    return pltpu.CompilerParams(dimension_semantics=sem, vmem_limit_bytes=VMEM_LIMIT)


def _halo_specs(tm, width, n_rows):
    nb = n_rows // HALO
    step = tm // HALO
    prev = pl.BlockSpec((HALO, width), lambda i: (jnp.maximum(i * step - 1, 0), 0))
    nxt = pl.BlockSpec((HALO, width), lambda i: (jnp.minimum((i + 1) * step, nb - 1), 0))
    return prev, nxt


def _full(shape):
    nd = len(shape)
    return pl.BlockSpec(shape, lambda *_: (0,) * nd)


def _inproj_kernel(x_ref, g_ref, w_ref, cos_ref, sin_ref, zr_ref, q0_ref, q1_ref, q2_ref, scr_ref, *, tm):
    h = _rms(x_ref[...], g_ref[...]).astype(BF16)
    zr_ref[...] = _dot(h, w_ref[:, :RWKV_IN])
    cs = cos_ref[...]
    sn = sin_ref[...]
    lane = lax.broadcasted_iota(jnp.int32, (tm, LANES), 1)
    first = (lane & (HEAD_DIM - 1)) < (ROT_DIM // 2)
    outs = (q0_ref, q1_ref, q2_ref)
    for g, (_, dil) in enumerate(ATTN_PATTERNS):
        base = RWKV_IN + g * ATTN_GROUP_IN
        t = _dot(h, w_ref[:, base:base + ATTN_GROUP_IN])
        for c in range(ATTN_GROUP_IN // LANES):
            tc = t[:, c * LANES:(c + 1) * LANES]
            if c < 2 * ATTN_OUT // LANES:
                rot = jnp.where(first, pltpu.roll(tc, LANES - ROT_DIM // 2, 1), pltpu.roll(tc, ROT_DIM // 2, 1))
                tc = tc * cs + rot * sn
            if dil == 1:
                outs[g][0, :, c * LANES:(c + 1) * LANES] = tc
            else:
                scr_ref[c] = tc
        if dil > 1:
            for c in range(ATTN_GROUP_IN // LANES):
                for r in range(dil):
                    outs[g][r, :, c * LANES:(c + 1) * LANES] = scr_ref[c, pl.ds(r, tm // dil, stride=dil), :]


def _inproj(x2, g, w, cos_t, sin_t, B, S, tm):
    T = B * S
    nts = S // tm
    out_shape = [jax.ShapeDtypeStruct((T, RWKV_IN), F32)]
    out_specs = [pl.BlockSpec((tm, RWKV_IN), lambda i: (i, 0))]
    for _, dil in ATTN_PATTERNS:
        out_shape.append(jax.ShapeDtypeStruct((B, dil, S // dil, ATTN_GROUP_IN), F32))
        out_specs.append(pl.BlockSpec((None, dil, tm // dil, ATTN_GROUP_IN),
                                      lambda i: (i // nts, 0, i % nts, 0)))
    return pl.pallas_call(
        functools.partial(_inproj_kernel, tm=tm),
        grid=(T // tm,),
        in_specs=[pl.BlockSpec((tm, D_MODEL), lambda i: (i, 0)),
                  _full((1, D_MODEL)),
                  _full(w.shape),
                  pl.BlockSpec((tm, LANES), lambda i: (i % nts, 0)),
                  pl.BlockSpec((tm, LANES), lambda i: (i % nts, 0))],
        out_specs=out_specs,
        out_shape=out_shape,
        scratch_shapes=[pltpu.VMEM((ATTN_GROUP_IN // LANES, tm, LANES), F32)],
        compiler_params=_cparams(("parallel",)),
        name="inproj",
    )(x2, g, w, cos_t, sin_t)


def _prep_kernel(z_ref, zp_ref, zn_ref, mu_ref, w0_ref, wuh_ref, wul_ref, a0_ref, auh_ref, aul_ref,
                 guh_ref, gul_ref, kk_ref, ka_ref, rk_ref,
                 r_o, kk_o, v_o, g_o, bonus_o, lw0_o, b0_o, kd0_o, lw1_o, b1_o, kd1_o, *, nts):
    it = pl.program_id(0) % nts
    z = z_ref[...]
    prev_row = jnp.where(it == 0, 0.0, zp_ref[HALO - 1:HALO, :])
    next_row = jnp.where(it == nts - 1, 0.0, zn_ref[0:1, :])
    zp, zn = _shift_rows(z, prev_row, next_row)
    zz = z + mu_ref[...] * (0.5 * (zp + zn) - z)
    W = RWKV_W
    r = zz[:, 0:W]
    k = zz[:, W:2 * W]
    v = zz[:, 2 * W:3 * W]
    o = 3 * W
    wd = jnp.tanh(zz[:, o:o + 2 * DECAY_LORA])
    o += 2 * DECAY_LORA
    ad = zz[:, o:o + 2 * ICLR_LORA]
    o += 2 * ICLR_LORA
    gd = _sigmoid(zz[:, o:])
    wlog = w0_ref[...] + _dot3(_split(wd), (wuh_ref[...], wul_ref[...]))
    aa = _sigmoid(a0_ref[...] + _dot3(_split(ad), (auh_ref[...], aul_ref[...])))
    g_o[...] = _dot3(_split(gd), (guh_ref[...], gul_ref[...]))
    ones = _head_ones()
    kkv = k * kk_ref[...]
    kkn = kkv / jnp.sqrt(_head_sum(kkv * kkv, ones) + 1e-12)
    r_o[...] = r
    kk_o[...] = kkn
    v_o[...] = v
    ka = ka_ref[...]
    kd_sum = None
    for d, (lw_o, b_o, kd_o) in enumerate(((lw0_o, b0_o, kd0_o), (lw1_o, b1_o, kd1_o))):
        a_d = aa[:, d * W:(d + 1) * W]
        lw_o[...] = -math.exp(-0.5) * _sigmoid(wlog[:, d * W:(d + 1) * W])
        kd = k * (1.0 + (a_d - 1.0) * ka)
        kd_o[...] = kd
        b_o[...] = kkn * a_d
        kd_sum = kd if kd_sum is None else kd_sum + kd
    bonus_o[...] = _head_sum(r * kd_sum * rk_ref[...], ones) * v


def _prep(zr, prm, j, B, S, tm):
    T = B * S
    nts = S // tm
    prev, nxt = _halo_specs(tm, RWKV_IN, T)
    W = RWKV_W
    vec = lambda a: a.reshape(1, -1)

    def bdiag(w):
        L = w.shape[1]
        m = jnp.zeros((2 * L, 2 * W), F32).at[:L, :W].set(w[0]).at[L:, W:].set(w[1])
        return _split(m)

    wuh, wul = bdiag(prm['rwkv_w_up'][j])
    auh, aul = bdiag(prm['rwkv_a_up'][j])
    guh, gul = _split(prm['rwkv_g_up'][j])
    params = [vec(prm['rwkv_mu'][j]), vec(prm['rwkv_w0'][j]), wuh, wul, vec(prm['rwkv_a0'][j]), auh, aul,
              guh, gul, vec(prm['rwkv_k_k'][j]), vec(prm['rwkv_k_a'][j]), vec(prm['rwkv_r_k'][j])]
    row = pl.BlockSpec((tm, W), lambda i: (i, 0))
    return pl.pallas_call(
        functools.partial(_prep_kernel, nts=nts),
        grid=(T // tm,),
        in_specs=[pl.BlockSpec((tm, RWKV_IN), lambda i: (i, 0)), prev, nxt] + [_full(a.shape) for a in params],
        out_specs=[row] * 11,
        out_shape=[jax.ShapeDtypeStruct((T, W), F32)] * 11,
        compiler_params=_cparams(("parallel",)),
        name="rwkv_prep",
    )(zr, zr, zr, *params)


def _stack2(x):
    lane = lax.broadcasted_iota(jnp.int32, x.shape, 1)
    m0 = lane < HEAD_DIM
    return jnp.concatenate([jnp.where(m0, x, 0.0), jnp.where(m0, 0.0, x)], axis=0)


def _wkv_chunk(r, kk, v, lw, b, kd, S, fwd):
    C = CHUNK
    ti = lax.broadcasted_iota(jnp.int32, (C, C), 0)
    tj = lax.broadcasted_iota(jnp.int32, (C, C), 1)
    cum = jnp.where((tj <= ti) if fwd else (tj >= ti), 1.0, 0.0).astype(BF16)
    lh, ll = _split(lw)
    G = _dot(cum, lh) + _dot(cum, ll)
    g_tot = G[C - 1:C, :] if fwd else G[0:1, :]
    e_neg = jnp.exp(-G)
    rt = r * jnp.exp(G)
    at = -kk * jnp.exp(G - lw)
    kt = kd * e_neg
    bt = b * e_neg
    e_tot = jnp.exp(g_tot)
    x_p = _split(jnp.concatenate([at, rt], axis=0))
    sc_k = _dot3_nt(x_p, _split(_stack2(kt)))
    sc_b = _dot3_nt(x_p, _split(_stack2(bt)))
    trow = lax.broadcasted_iota(jnp.int32, (C, 2 * C), 0)
    tcol = lax.broadcasted_iota(jnp.int32, (C, 2 * C), 1) & (C - 1)
    strict = (tcol < trow) if fwd else (tcol > trow)
    incl = (tcol <= trow) if fwd else (tcol >= trow)
    a_ak = jnp.where(strict, sc_k[:C], 0.0)
    a_rk = jnp.where(incl, sc_k[C:], 0.0)
    n_p = jnp.where(strict, sc_b[:C], 0.0)
    a_rb = jnp.where(incl, sc_b[C:], 0.0)
    s_p = _split(S)
    v2_p = _split(_stack2(v))
    w = _dot3_nt((x_p[0][:C], x_p[1][:C]), s_p) + _dot3(_split(a_ak), v2_p)
    for j in range(6):
        n_s = _split(n_p)
        w = w + _dot3(n_s, _split(_stack2(w)))
        if j < 5:
            n_p = _dot3(n_s, _split(_stack2(n_p)))
    u = w
    y = (_dot3_nt((x_p[0][C:], x_p[1][C:]), s_p) + _dot3(_split(a_rk), v2_p)
         + _dot3(_split(a_rb), _split(_stack2(u))))
    vu_t = jnp.concatenate([v, u], axis=0).T
    kb = jnp.concatenate([kt * e_tot, bt * e_tot], axis=0)
    upd = _dot3(_split(vu_t), _split(kb))
    hr = lax.broadcasted_iota(jnp.int32, (LANES, LANES), 0) // HEAD_DIM
    hc = lax.broadcasted_iota(jnp.int32, (LANES, LANES), 1) // HEAD_DIM
    s_new = S * e_tot + jnp.where(hr == hc, upd, 0.0)
    return y, s_new


def _scan_kernel(rf, kkf, vf, lwf, bf, kdf, rb, kkb, vb, lwb, bb, kdb, yf_o, yb_o, sf_ref, sb_ref, *, nc):
    @pl.when(pl.program_id(2) == 0)
    def _():
        sf_ref[...] = jnp.zeros_like(sf_ref)
        sb_ref[...] = jnp.zeros_like(sb_ref)

    def body(c, carry):
        of = pl.multiple_of(c * CHUNK, CHUNK)
        ob = pl.multiple_of((nc - 1 - c) * CHUNK, CHUNK)
        sl = pl.ds(of, CHUNK)
        y, s = _wkv_chunk(rf[sl, :], kkf[sl, :], vf[sl, :], lwf[sl, :], bf[sl, :], kdf[sl, :], sf_ref[...], True)
        yf_o[sl, :] = y
        sf_ref[...] = s
        sl = pl.ds(ob, CHUNK)
        y, s = _wkv_chunk(rb[sl, :], kkb[sl, :], vb[sl, :], lwb[sl, :], bb[sl, :], kdb[sl, :], sb_ref[...], False)
        yb_o[sl, :] = y
        sb_ref[...] = s
        return carry

    lax.fori_loop(0, nc, body, 0)


def _scan(r, kk, v, lw0, b0, kd0, lw1, b1, kd1, B, S, tb):
    nblk = S // tb
    shp = (B, S, RWKV_W)
    args = [a.reshape(shp) for a in (r, kk, v, lw0, b0, kd0, r, kk, v, lw1, b1, kd1)]
    fwd = pl.BlockSpec((None, tb, LANES), lambda bi, p, j: (bi, j, p))
    bwd = pl.BlockSpec((None, tb, LANES), lambda bi, p, j: (bi, nblk - 1 - j, p))
    yf, yb = pl.pallas_call(
        functools.partial(_scan_kernel, nc=tb // CHUNK),
        grid=(B, RWKV_W // LANES, nblk),
        in_specs=[fwd] * 6 + [bwd] * 6,
        out_specs=[fwd, bwd],
        out_shape=[jax.ShapeDtypeStruct(shp, F32)] * 2,
        scratch_shapes=[pltpu.VMEM((LANES, LANES), F32)] * 2,
        compiler_params=_cparams(("parallel", "parallel", "arbitrary")),
        name="wkv_scan",
    )(*args)
    return yf.reshape(B * S, RWKV_W), yb.reshape(B * S, RWKV_W)


def _attn_kernel(q_ref, km_ref, vm_ref, kp_ref, vp_ref, kn_ref, vn_ref, o_ref, lse_ref, *, Q, L):
    j = pl.program_id(2)
    q = q_ref[...] * (HEAD_DIM ** -0.5)
    kw = jnp.concatenate([kp_ref[...], km_ref[...], kn_ref[...]], axis=0).astype(BF16)
    vw = jnp.concatenate([vp_ref[...], vm_ref[...], vn_ref[...]], axis=0).astype(BF16)
    Wn = Q + 2 * ATTN_HALF
    qi = lax.broadcasted_iota(jnp.int32, (Q, Wn), 0)
    wc = lax.broadcasted_iota(jnp.int32, (Q, Wn), 1)
    rel = wc - ATTN_HALF - qi
    kpos = j * Q - ATTN_HALF + wc
    valid = jnp.minimum(jnp.minimum(ATTN_HALF - jnp.abs(rel), kpos), L - 1 - kpos) >= 0
    lane = lax.broadcasted_iota(jnp.int32, (Q, ATTN_OUT), 1) // HEAD_DIM
    o_acc = jnp.zeros((Q, ATTN_OUT), F32)
    l_acc = jnp.zeros((Q, ATTN_OUT), F32)
    for hh in range(ATTN_HEADS_PER_GROUP):
        mh = lane == hh
        qh = jnp.where(mh, q, 0.0).astype(BF16)
        s = jnp.where(valid, _dot_nt(qh, kw), NEG_INF)
        m = jnp.max(s, axis=-1, keepdims=True)
        p = jnp.exp(s - m)
        den = jnp.sum(p, axis=-1, keepdims=True)
        oh = _dot(p.astype(BF16), vw) / den
        o_acc = jnp.where(mh, oh, o_acc)
        l_acc = jnp.where(mh, m + jnp.log(den), l_acc)
    o_ref[...] = o_acc
    lse_ref[...] = l_acc


def _attn(qkv, B, S, dil):
    L = S // dil
    Q = min(256, L)
    nq = L // Q
    qb = Q // ATTN_HALF
    nhb = L // ATTN_HALF

    def mid(col):
        return pl.BlockSpec((None, None, Q, ATTN_OUT), lambda bi, r, j: (bi, r, j, col))

    def prev(col):
        return pl.BlockSpec((None, None, ATTN_HALF, ATTN_OUT),
                            lambda bi, r, j: (bi, r, jnp.maximum(j * qb - 1, 0), col))

    def nxt(col):
        return pl.BlockSpec((None, None, ATTN_HALF, ATTN_OUT),
                            lambda bi, r, j: (bi, r, jnp.minimum((j + 1) * qb, nhb - 1), col))

    out = pl.BlockSpec((None, None, Q, ATTN_OUT), lambda bi, r, j: (bi, r, j, 0))
    return pl.pallas_call(
        functools.partial(_attn_kernel, Q=Q, L=L),
        grid=(B, dil, nq),
        in_specs=[mid(0), mid(1), mid(2), prev(1), prev(2), nxt(1), nxt(2)],
        out_specs=[out, out],
        out_shape=[jax.ShapeDtypeStruct((B, dil, L, ATTN_OUT), F32)] * 2,
        compiler_params=_cparams(("parallel", "parallel", "parallel")),
        name="band_attn_d%d" % dil,
    )(qkv, qkv, qkv, qkv, qkv, qkv, qkv)


def _post_kernel(x_ref, yf_ref, yb_ref, bonus_ref, g_ref, o0, l0, o1, l1, o2, l2, lng_ref, lnb_ref, wo_ref,
                 out_ref, oscr, lscr, *, tm):
    ones = _head_ones()
    y = yf_ref[...] + yb_ref[...]
    yc = y - _head_sum(y, ones) * (1.0 / HEAD_DIM)
    var = _head_sum(yc * yc, ones) * (1.0 / HEAD_DIM)
    yn = yc * lax.rsqrt(var + GN_EPS) * lng_ref[...] + lnb_ref[...]
    ya = (yn + bonus_ref[...]) * g_ref[...]
    nlb = ATTN_OUT // LANES
    os_, ls = [], []
    for gi, (o_r, l_r, (_, dil)) in enumerate(zip((o0, o1, o2), (l0, l1, l2), ATTN_PATTERNS)):
        if dil == 1:
            os_.append(o_r[0])
            ls.append(l_r[0])
            continue
        for c in range(nlb):
            for r in range(dil):
                oscr[gi * nlb + c, pl.ds(r, tm // dil, stride=dil), :] = o_r[r, :, c * LANES:(c + 1) * LANES]
                lscr[gi * nlb + c, pl.ds(r, tm // dil, stride=dil), :] = l_r[r, :, c * LANES:(c + 1) * LANES]
        os_.append(jnp.concatenate([oscr[gi * nlb + c] for c in range(nlb)], axis=1))
        ls.append(jnp.concatenate([lscr[gi * nlb + c] for c in range(nlb)], axis=1))
    m = jnp.maximum(jnp.maximum(ls[0], ls[1]), ls[2])
    es = [jnp.exp(l - m) for l in ls]
    num = es[0] * os_[0] + es[1] * os_[1] + es[2] * os_[2]
    yb = num / (es[0] + es[1] + es[2])
    out_ref[...] = (x_ref[...] + _dot(ya.astype(BF16), wo_ref[:RWKV_W, :])
                    + _dot(yb.astype(BF16), wo_ref[RWKV_W:, :]))


def _post(x2, yf, yb, bonus, g, attn, lng, lnb, wo, B, S, tm):
    T = B * S
    nts = S // tm
    row = lambda w: pl.BlockSpec((tm, w), lambda i: (i, 0))
    in_specs = [row(D_MODEL)] + [row(RWKV_W)] * 4
    args = [x2, yf, yb, bonus, g]
    for (o, l), (_, dil) in zip(attn, ATTN_PATTERNS):
        spec = pl.BlockSpec((None, dil, tm // dil, ATTN_OUT), lambda i: (i // nts, 0, i % nts, 0))
        in_specs += [spec, spec]
        args += [o, l]
    in_specs += [_full(lng.shape), _full(lnb.shape), _full(wo.shape)]
    args += [lng, lnb, wo]
    ng = len(ATTN_PATTERNS)
    return pl.pallas_call(
        functools.partial(_post_kernel, tm=tm),
        grid=(T // tm,),
        in_specs=in_specs,
        out_specs=row(D_MODEL),
        out_shape=jax.ShapeDtypeStruct((T, D_MODEL), F32),
        scratch_shapes=[pltpu.VMEM((ng * ATTN_OUT // LANES, tm, LANES), F32)] * 2,
        compiler_params=_cparams(("parallel",)),
        name="mixer_ab_out",
    )(*args)


def _pool_kernel(x_ref, xp_ref, xn_ref, g_ref, win_ref, wg_ref, sc_ref, wo_ref, out_ref, u_ref, *, tm, nts, S):
    it = pl.program_id(0) % nts
    x = x_ref[...]
    g = g_ref[...]
    hp = jnp.where(it == 0, 0.0, _rms(xp_ref[...], g))
    hn = jnp.where(it == nts - 1, 0.0, _rms(xn_ref[...], g))
    h = jnp.concatenate([hp, _rms(x, g), hn], axis=0).astype(BF16)
    u_ref[...] = _dot(h, win_ref[...])
    pos = it * tm + lax.broadcasted_iota(jnp.int32, (tm, 1), 0)
    acc = x
    for gi, win in enumerate(POOL_WINDOWS):
        rad = win // 2
        cols = slice(gi * POOL_GROUP, (gi + 1) * POOL_GROUP)
        ws = u_ref[pl.ds(HALO, tm), cols]
        u_c = ws
        for d in range(1, rad + 1):
            ws = ws + (u_ref[pl.ds(HALO - d, tm), cols] + u_ref[pl.ds(HALO + d, tm), cols])
        cnt = (jnp.minimum(pos + rad + 1, S) - jnp.maximum(pos - rad, 0)).astype(F32)
        dlt = (ws / cnt - u_c).astype(BF16)
        yg = _dot(dlt, wg_ref[gi]) * sc_ref[:, cols]
        acc = acc + _dot(yg.astype(BF16), wo_ref[cols, :])
    out_ref[...] = acc


def _pool(x2, g, w_in, w_group, scale, w_out, B, S, tm):
    T = B * S
    nts = S // tm
    prev, nxt = _halo_specs(tm, D_MODEL, T)
    row = pl.BlockSpec((tm, D_MODEL), lambda i: (i, 0))
    return pl.pallas_call(
        functools.partial(_pool_kernel, tm=tm, nts=nts, S=S),
        grid=(T // tm,),
        in_specs=[row, prev, nxt, _full(g.shape), _full(w_in.shape), _full(w_group.shape), _full(scale.shape),
                  _full(w_out.shape)],
        out_specs=row,
        out_shape=jax.ShapeDtypeStruct((T, D_MODEL), F32),
        scratch_shapes=[pltpu.VMEM((tm + 2 * HALO, POOL_W), F32)],
        compiler_params=_cparams(("parallel",)),
        name="pool_mixer",
    )(x2, x2, x2, g, w_in, w_group, scale, w_out)


def _gelu(x):
    return x * (0.5 * (1.0 + jnp.tanh(math.sqrt(2.0 / math.pi) * (x + 0.044715 * (x * x * x)))))


def _ffn_kernel(x_ref, xp_ref, xn_ref, p_ref, gf_ref, wg_ref, wv_ref, cwg_ref, cwv_ref, cbg_ref, cbv_ref, wd_ref,
                gp_ref, wpg_ref, bpg_ref, wpp_ref, gfin_ref, out_ref, h_ref, acc_ref, ug_ref, uv_ref,
                *, tm, nts, nf, final):
    it = pl.program_id(0) % nts
    f = pl.program_id(1)

    @pl.when(f == 0)
    def _():
        g = gf_ref[...]
        h_ref[pl.ds(0, HALO), :] = jnp.where(it == 0, 0.0, _rms(xp_ref[...], g)).astype(BF16)
        h_ref[pl.ds(HALO, tm), :] = _rms(x_ref[...], g).astype(BF16)
        h_ref[pl.ds(HALO + tm, HALO), :] = jnp.where(it == nts - 1, 0.0, _rms(xn_ref[...], g)).astype(BF16)
        acc_ref[...] = jnp.zeros_like(acc_ref)

    h = h_ref[...]
    ug_ref[...] = _dot(h, wg_ref[...])
    uv_ref[...] = _dot(h, wv_ref[...])

    def conv(u_ref, cw_ref, cb_ref):
        return (u_ref[pl.ds(HALO - 1, tm), :] * cw_ref[0:1, :] + u_ref[pl.ds(HALO, tm), :] * cw_ref[1:2, :]
                + u_ref[pl.ds(HALO + 1, tm), :] * cw_ref[2:3, :] + cb_ref[...])

    act = _gelu(conv(ug_ref, cwg_ref, cbg_ref)) * conv(uv_ref, cwv_ref, cbv_ref)
    acc_ref[...] += _dot(act.astype(BF16), wd_ref[...])

    @pl.when(f == nf - 1)
    def _():
        x2 = x_ref[...] + acc_ref[...]
        h3 = _rms(x2, gp_ref[...]).astype(BF16)
        gate = _sigmoid(_dot(h3, wpg_ref[...]) + bpg_ref[...])
        x3 = x2 + gate * _dot(p_ref[...].astype(BF16), wpp_ref[...])
        out_ref[...] = _rms(x3, gfin_ref[...]) if final else x3


def _ffn(x2, p_all, layer, w, B, S, tm, tf, final):
    T = B * S
    nt = T // tm
    nts = S // tm
    nf = D_FF // tf
    nb = T // HALO
    step = tm // HALO
    row = pl.BlockSpec((tm, D_MODEL), lambda i, f: (i, 0))
    prev = pl.BlockSpec((HALO, D_MODEL), lambda i, f: (jnp.maximum(i * step - 1, 0), 0))
    nxt = pl.BlockSpec((HALO, D_MODEL), lambda i, f: (jnp.minimum((i + 1) * step, nb - 1), 0))
    full = lambda a: pl.BlockSpec(a.shape, lambda i, f: (0,) * a.ndim)
    colg = lambda rows: pl.BlockSpec((rows, tf), lambda i, f: (0, f))
    colv = lambda rows: pl.BlockSpec((rows, tf), lambda i, f: (0, nf + f))
    in_specs = [row, prev, nxt,
                pl.BlockSpec((tm, PLE_DIM), lambda i, f: (layer * nt + i, 0)),
                full(w['gf']), colg(D_MODEL), colv(D_MODEL), colg(3), colv(3), colg(1), colv(1),
                pl.BlockSpec((tf, D_MODEL), lambda i, f: (f, 0)),
                full(w['gp']), full(w['wpg']), full(w['bpg']), full(w['wpp']), full(w['gfin'])]
    return pl.pallas_call(
        functools.partial(_ffn_kernel, tm=tm, nts=nts, nf=nf, final=final),
        grid=(nt, nf),
        in_specs=in_specs,
        out_specs=row,
        out_shape=jax.ShapeDtypeStruct((T, D_MODEL), F32),
        scratch_shapes=[pltpu.VMEM((tm + 2 * HALO, D_MODEL), BF16), pltpu.VMEM((tm, D_MODEL), F32),
                        pltpu.VMEM((tm + 2 * HALO, tf), F32), pltpu.VMEM((tm + 2 * HALO, tf), F32)],
        compiler_params=_cparams(("parallel", "arbitrary")),
        name="convffn_ple",
    )(x2, x2, x2, p_all, w['gf'], w['wup'], w['wup'], w['cw'], w['cw'], w['cb'], w['cb'], w['wd'],
      w['gp'], w['wpg'], w['bpg'], w['wpp'], w['gfin'])


def _rope_tables(S):
    half = ROT_DIM // 2
    inv = jnp.float32(ROPE_THETA) ** (-jnp.arange(half, dtype=F32) * 2.0 / ROT_DIM)
    ang = jnp.arange(S, dtype=F32)[:, None] * inv[None, :]
    cos = jnp.cos(ang)
    sin = jnp.sin(ang)
    pad1 = jnp.ones((S, HEAD_DIM - ROT_DIM), F32)
    pad0 = jnp.zeros((S, HEAD_DIM - ROT_DIM), F32)
    cos_h = jnp.concatenate([cos, cos, pad1], axis=1)
    sin_h = jnp.concatenate([-sin, sin, pad0], axis=1)
    reps = LANES // HEAD_DIM
    return jnp.tile(cos_h, (1, reps)), jnp.tile(sin_h, (1, reps))


def _pack_ab_w_in(w):
    parts = [w[:, :RWKV_IN]]
    qkv = [w[:, RWKV_IN + s * ATTN_HEADS * HEAD_DIM: RWKV_IN + (s + 1) * ATTN_HEADS * HEAD_DIM] for s in range(3)]
    for g in range(len(ATTN_PATTERNS)):
        parts += [t[:, g * ATTN_OUT:(g + 1) * ATTN_OUT] for t in qkv]
    return jnp.concatenate(parts, axis=1).astype(BF16)


def _prepare(prm):
    vec = lambda a: a.reshape(1, -1).astype(F32)
    pk = {'ab_w_in': [_pack_ab_w_in(prm['ab_w_in'][j]) for j in range(prm['ab_w_in'].shape[0])],
          'ab_w_out': prm['ab_w_out'].astype(BF16),
          'c_w_in': prm['c_w_in'].astype(BF16), 'c_w_group': prm['c_w_group'].astype(BF16),
          'c_w_out': prm['c_w_out'].astype(BF16), 'ffn': []}
    for i in range(DEPTH):
        pk['ffn'].append({
            'gf': vec(prm['norm_ffn_g'][i]), 'wup': prm['ffn_w_up'][i].astype(BF16),
            'cw': prm['ffn_conv_w'][i], 'cb': vec(prm['ffn_conv_b'][i]),
            'wd': prm['ffn_w_down'][i].astype(BF16), 'gp': vec(prm['norm_ple_g'][i]),
            'wpg': prm['ple_w_gate'][i].astype(BF16), 'bpg': vec(prm['ple_b_gate'][i]),
            'wpp': prm['ple_w_proj'][i].astype(BF16), 'gfin': vec(prm['norm_final_g'])})
    return pk


def _trunk(x, p, prm, pk, tiles):
    B, S, _ = x.shape
    T = B * S
    vec = lambda a: a.reshape(1, -1)
    x2 = x.reshape(T, D_MODEL)
    p_all = p.reshape(p.shape[0] * T, PLE_DIM)
    cos_t, sin_t = _rope_tables(S)
    for i in range(DEPTH):
        j = i // 2
        gmix = vec(prm['norm_mix_g'][i])
        if i % 2 == 0:
            zr, q0, q1, q2 = _inproj(x2, gmix, pk['ab_w_in'][j], cos_t, sin_t, B, S, tiles['inproj'])
            r, kk, v, g, bonus, lw0, b0, kd0, lw1, b1, kd1 = _prep(zr, prm, j, B, S, tiles['prep'])
            yf, yb = _scan(r, kk, v, lw0, b0, kd0, lw1, b1, kd1, B, S, tiles['scan'])
            attn = [_attn(q, B, S, dil) for q, (_, dil) in zip((q0, q1, q2), ATTN_PATTERNS)]
            x2 = _post(x2, yf, yb, bonus, g, attn, vec(prm['rwkv_ln_g'][j]), vec(prm['rwkv_ln_b'][j]),
                       pk['ab_w_out'][j], B, S, tiles['post'])
        else:
            x2 = _pool(x2, gmix, pk['c_w_in'][j], pk['c_w_group'][j], vec(prm['c_scale'][j]), pk['c_w_out'][j],
                       B, S, tiles['pool'])
        x2 = _ffn(x2, p_all, i, pk['ffn'][i], B, S, tiles['ffn_m'], tiles['ffn_f'], final=(i == DEPTH - 1))
    return x2.reshape(B, S, D_MODEL)


_TILES = {'inproj': 512, 'prep': 256, 'scan': 256, 'post': 512, 'pool': 512, 'ffn_m': 1024, 'ffn_f': 256}


def kernel(x_prompt, x_sample, p_prompt, p_sample, ab_w_in, ab_w_out, rwkv_mu, rwkv_w0, rwkv_w_up, rwkv_a0, rwkv_a_up, rwkv_g_up, rwkv_k_k, rwkv_k_a, rwkv_r_k, rwkv_ln_g, rwkv_ln_b, c_w_in, c_w_group, c_scale, c_w_out, norm_mix_g, norm_ffn_g, norm_ple_g, norm_final_g, ffn_w_up, ffn_conv_w, ffn_conv_b, ffn_w_down, ple_w_proj, ple_w_gate, ple_b_gate):
    prm = {
        'ab_w_in': ab_w_in, 'ab_w_out': ab_w_out, 'rwkv_mu': rwkv_mu, 'rwkv_w0': rwkv_w0,
        'rwkv_w_up': rwkv_w_up, 'rwkv_a0': rwkv_a0, 'rwkv_a_up': rwkv_a_up, 'rwkv_g_up': rwkv_g_up,
        'rwkv_k_k': rwkv_k_k, 'rwkv_k_a': rwkv_k_a, 'rwkv_r_k': rwkv_r_k, 'rwkv_ln_g': rwkv_ln_g,
        'rwkv_ln_b': rwkv_ln_b, 'c_w_in': c_w_in, 'c_w_group': c_w_group, 'c_scale': c_scale,
        'c_w_out': c_w_out, 'norm_mix_g': norm_mix_g, 'norm_ffn_g': norm_ffn_g, 'norm_ple_g': norm_ple_g,
        'norm_final_g': norm_final_g, 'ffn_w_up': ffn_w_up, 'ffn_conv_w': ffn_conv_w,
        'ffn_conv_b': ffn_conv_b, 'ffn_w_down': ffn_w_down, 'ple_w_proj': ple_w_proj,
        'ple_w_gate': ple_w_gate, 'ple_b_gate': ple_b_gate,
    }
    pk = _prepare(prm)
    y_prompt = _trunk(x_prompt, p_prompt, prm, pk, _TILES)
    y_sample = _trunk(x_sample, p_sample, prm, pk, _TILES)
    return (y_prompt, y_sample)
```

```python
import functools
import math

import jax
import jax.numpy as jnp
from jax import lax
from jax.experimental import pallas as pl
from jax.experimental.pallas import tpu as pltpu

D_MODEL = 1024
DEPTH = 4
PLE_DIM = 256
HEAD_DIM = 64
RMS_EPS = 1e-6
RWKV_HEADS = 8
RWKV_W = RWKV_HEADS * HEAD_DIM
DECAY_LORA = 64
ICLR_LORA = 64
GATE_LORA = 128
RWKV_IN = 3 * RWKV_W + 2 * DECAY_LORA + 2 * ICLR_LORA + GATE_LORA
GN_EPS = 64e-5
ATTN_PATTERNS = ((128, 1), (512, 4), (2048, 16))
ATTN_HEADS_PER_GROUP = 4
ATTN_HEADS = ATTN_HEADS_PER_GROUP * len(ATTN_PATTERNS)
ATTN_OUT = ATTN_HEADS_PER_GROUP * HEAD_DIM
ATTN_GROUP_IN = 3 * ATTN_OUT
ATTN_HALF = 64
ROPE_THETA = 500000.0
ROT_DIM = HEAD_DIM // 4
NEG_INF = -1e30
AB_OUT = RWKV_W + ATTN_OUT
POOL_WINDOWS = (2, 4, 8, 16)
POOL_GROUP = 256
POOL_W = len(POOL_WINDOWS) * POOL_GROUP
D_FF = 2816

LANES = 128
SUBLANES = 8
HALO = SUBLANES
CHUNK = 64
VMEM_LIMIT = 56 * 1024 * 1024

F32 = jnp.float32
BF16 = jnp.bfloat16


def _dot(a, b):
    return jnp.dot(a, b, preferred_element_type=F32)


def _dot_nt(a, b):
    return lax.dot_general(a, b, (((1,), (1,)), ((), ())), preferred_element_type=F32)


def _split(x):
    hi = x.astype(BF16)
    lo = (x - hi.astype(F32)).astype(BF16)
    return hi, lo


def _dot3(ap, bp):
    return _dot(ap[0], bp[0]) + (_dot(ap[0], bp[1]) + _dot(ap[1], bp[0]))


def _dot3_nt(ap, bp):
    return _dot_nt(ap[0], bp[0]) + (_dot_nt(ap[0], bp[1]) + _dot_nt(ap[1], bp[0]))


def _rms(x, g):
    return x * lax.rsqrt(jnp.mean(x * x, axis=-1, keepdims=True) + RMS_EPS) * g


def _sigmoid(x):
    return 1.0 / (1.0 + jnp.exp(-x))


def _head_ones():
    r = lax.broadcasted_iota(jnp.int32, (LANES, LANES), 0) // HEAD_DIM
    c = lax.broadcasted_iota(jnp.int32, (LANES, LANES), 1) // HEAD_DIM
    return jnp.where(r == c, 1.0, 0.0).astype(BF16)


def _head_sum(x, ones):
    outs = []
    for j in range(x.shape[1] // LANES):
        hi, lo = _split(x[:, j * LANES:(j + 1) * LANES])
        outs.append(_dot(hi, ones) + _dot(lo, ones))
    return jnp.concatenate(outs, axis=1)


def _shift_rows(z, prev_row, next_row):
    n = z.shape[0]
    row = lax.broadcasted_iota(jnp.int32, z.shape, 0)
    zp = jnp.where(row == 0, prev_row, pltpu.roll(z, 1, 0))
    zn = jnp.where(row == n - 1, next_row, pltpu.roll(z, n - 1, 0))
    return zp, zn


def _cparams(sem):
    return pltpu.CompilerParams(dimension_semantics=sem, vmem_limit_bytes=VMEM_LIMIT)


def _halo_specs(tm, width, n_rows):
    nb = n_rows // HALO
    step = tm // HALO
    prev = pl.BlockSpec((HALO, width), lambda i: (jnp.maximum(i * step - 1, 0), 0))
    nxt = pl.BlockSpec((HALO, width), lambda i: (jnp.minimum((i + 1) * step, nb - 1), 0))
    return prev, nxt


def _full(shape):
    nd = len(shape)
    return pl.BlockSpec(shape, lambda *_: (0,) * nd)


def _inproj_kernel(x_ref, g_ref, w_ref, cos_ref, sin_ref, zr_ref, q0_ref, q1_ref, q2_ref, scr_ref, *, tm):
    h = _rms(x_ref[...], g_ref[...]).astype(BF16)
    zr_ref[...] = _dot(h, w_ref[:, :RWKV_IN])
    cs = cos_ref[...]
    sn = sin_ref[...]
    lane = lax.broadcasted_iota(jnp.int32, (tm, LANES), 1)
    first = (lane & (HEAD_DIM - 1)) < (ROT_DIM // 2)
    outs = (q0_ref, q1_ref, q2_ref)
    for g, (_, dil) in enumerate(ATTN_PATTERNS):
        base = RWKV_IN + g * ATTN_GROUP_IN
        t = _dot(h, w_ref[:, base:base + ATTN_GROUP_IN])
        for c in range(ATTN_GROUP_IN // LANES):
            tc = t[:, c * LANES:(c + 1) * LANES]
            if c < 2 * ATTN_OUT // LANES:
                rot = jnp.where(first, pltpu.roll(tc, LANES - ROT_DIM // 2, 1), pltpu.roll(tc, ROT_DIM // 2, 1))
                tc = tc * cs + rot * sn
            if dil == 1:
                outs[g][0, :, c * LANES:(c + 1) * LANES] = tc
            else:
                scr_ref[c] = tc
        if dil > 1:
            for c in range(ATTN_GROUP_IN // LANES):
                for r in range(dil):
                    outs[g][r, :, c * LANES:(c + 1) * LANES] = scr_ref[c, pl.ds(r, tm // dil, stride=dil), :]


def _inproj(x2, g, w, cos_t, sin_t, B, S, tm):
    T = B * S
    nts = S // tm
    out_shape = [jax.ShapeDtypeStruct((T, RWKV_IN), F32)]
    out_specs = [pl.BlockSpec((tm, RWKV_IN), lambda i: (i, 0))]
    for _, dil in ATTN_PATTERNS:
        out_shape.append(jax.ShapeDtypeStruct((B, dil, S // dil, ATTN_GROUP_IN), F32))
        out_specs.append(pl.BlockSpec((None, dil, tm // dil, ATTN_GROUP_IN),
                                      lambda i: (i // nts, 0, i % nts, 0)))
    return pl.pallas_call(
        functools.partial(_inproj_kernel, tm=tm),
        grid=(T // tm,),
        in_specs=[pl.BlockSpec((tm, D_MODEL), lambda i: (i, 0)),
                  _full((1, D_MODEL)),
                  _full(w.shape),
                  pl.BlockSpec((tm, LANES), lambda i: (i % nts, 0)),
                  pl.BlockSpec((tm, LANES), lambda i: (i % nts, 0))],
        out_specs=out_specs,
        out_shape=out_shape,
        scratch_shapes=[pltpu.VMEM((ATTN_GROUP_IN // LANES, tm, LANES), F32)],
        compiler_params=_cparams(("parallel",)),
        name="inproj",
    )(x2, g, w, cos_t, sin_t)


def _prep_kernel(z_ref, zp_ref, zn_ref, mu_ref, w0_ref, wuh_ref, wul_ref, a0_ref, auh_ref, aul_ref,
                 guh_ref, gul_ref, kk_ref, ka_ref, rk_ref,
                 r_o, kk_o, v_o, g_o, bonus_o, lw0_o, b0_o, kd0_o, lw1_o, b1_o, kd1_o, *, nts):
    it = pl.program_id(0) % nts
    z = z_ref[...]
    prev_row = jnp.where(it == 0, 0.0, zp_ref[HALO - 1:HALO, :])
    next_row = jnp.where(it == nts - 1, 0.0, zn_ref[0:1, :])
    zp, zn = _shift_rows(z, prev_row, next_row)
    zz = z + mu_ref[...] * (0.5 * (zp + zn) - z)
    W = RWKV_W
    r = zz[:, 0:W]
    k = zz[:, W:2 * W]
    v = zz[:, 2 * W:3 * W]
    o = 3 * W
    wd = jnp.tanh(zz[:, o:o + 2 * DECAY_LORA])
    o += 2 * DECAY_LORA
    ad = zz[:, o:o + 2 * ICLR_LORA]
    o += 2 * ICLR_LORA
    gd = _sigmoid(zz[:, o:])
    wlog = w0_ref[...] + _dot3(_split(wd), (wuh_ref[...], wul_ref[...]))
    aa = _sigmoid(a0_ref[...] + _dot3(_split(ad), (auh_ref[...], aul_ref[...])))
    g_o[...] = _dot3(_split(gd), (guh_ref[...], gul_ref[...]))
    ones = _head_ones()
    kkv = k * kk_ref[...]
    kkn = kkv / jnp.sqrt(_head_sum(kkv * kkv, ones) + 1e-12)
    r_o[...] = r
    kk_o[...] = kkn
    v_o[...] = v
    ka = ka_ref[...]
    kd_sum = None
    for d, (lw_o, b_o, kd_o) in enumerate(((lw0_o, b0_o, kd0_o), (lw1_o, b1_o, kd1_o))):
        a_d = aa[:, d * W:(d + 1) * W]
        lw_o[...] = -math.exp(-0.5) * _sigmoid(wlog[:, d * W:(d + 1) * W])
        kd = k * (1.0 + (a_d - 1.0) * ka)
        kd_o[...] = kd
        b_o[...] = kkn * a_d
        kd_sum = kd if kd_sum is None else kd_sum + kd
    bonus_o[...] = _head_sum(r * kd_sum * rk_ref[...], ones) * v


def _prep(zr, prm, j, B, S, tm):
    T = B * S
    nts = S // tm
    prev, nxt = _halo_specs(tm, RWKV_IN, T)
    W = RWKV_W
    vec = lambda a: a.reshape(1, -1)

    def bdiag(w):
        L = w.shape[1]
        m = jnp.zeros((2 * L, 2 * W), F32).at[:L, :W].set(w[0]).at[L:, W:].set(w[1])
        return _split(m)

    wuh, wul = bdiag(prm['rwkv_w_up'][j])
    auh, aul = bdiag(prm['rwkv_a_up'][j])
    guh, gul = _split(prm['rwkv_g_up'][j])
    params = [vec(prm['rwkv_mu'][j]), vec(prm['rwkv_w0'][j]), wuh, wul, vec(prm['rwkv_a0'][j]), auh, aul,
              guh, gul, vec(prm['rwkv_k_k'][j]), vec(prm['rwkv_k_a'][j]), vec(prm['rwkv_r_k'][j])]
    row = pl.BlockSpec((tm, W), lambda i: (i, 0))
    return pl.pallas_call(
        functools.partial(_prep_kernel, nts=nts),
        grid=(T // tm,),
        in_specs=[pl.BlockSpec((tm, RWKV_IN), lambda i: (i, 0)), prev, nxt] + [_full(a.shape) for a in params],
        out_specs=[row] * 11,
        out_shape=[jax.ShapeDtypeStruct((T, W), F32)] * 11,
        compiler_params=_cparams(("parallel",)),
        name="rwkv_prep",
    )(zr, zr, zr, *params)


def _stack2(x):
    lane = lax.broadcasted_iota(jnp.int32, x.shape, 1)
    m0 = lane < HEAD_DIM
    zero = jnp.zeros_like(x)
    return jnp.concatenate([jnp.where(m0, x, zero), jnp.where(m0, zero, x)], axis=0)


def _pm(m, x):
    return _dot(m.astype(BF16), _stack2(x.astype(BF16)))


def _wkv_scaled(r, kk, v, lw, b, kd, fwd):
    C = CHUNK
    ti = lax.broadcasted_iota(jnp.int32, (C, C), 0)
    tj = lax.broadcasted_iota(jnp.int32, (C, C), 1)
    cum = jnp.where((tj <= ti) if fwd else (tj >= ti), 1.0, 0.0).astype(BF16)
    lh, ll = _split(lw)
    G = _dot(cum, lh) + _dot(cum, ll)
    g_tot = G[C - 1:C, :] if fwd else G[0:1, :]
    e_neg = jnp.exp(-G)
    rt = r * jnp.exp(G)
    at = -kk * jnp.exp(G - lw)
    kt = kd * e_neg
    bt = b * e_neg
    e_tot = jnp.exp(g_tot)
    kb = jnp.concatenate([kt * e_tot, bt * e_tot], axis=0)
    return dict(at=at, rt=rt, kt=kt, bt=bt, kb=kb, v=v, e_tot=e_tot, fwd=fwd)


def _wkv_pairs(chains):
    C = CHUNK
    trow = lax.broadcasted_iota(jnp.int32, (C, 2 * C), 0)
    tcol = lax.broadcasted_iota(jnp.int32, (C, 2 * C), 1) & (C - 1)
    eye = jnp.where(tcol == trow, 1.0, 0.0)
    hr = lax.broadcasted_iota(jnp.int32, (LANES, LANES), 0) // HEAD_DIM
    hc = lax.broadcasted_iota(jnp.int32, (LANES, LANES), 1) // HEAD_DIM
    for ch in chains:
        ch['x'] = jnp.concatenate([ch['at'], ch['rt']], axis=0).astype(BF16)
    for ch in chains:
        ch['sc_k'] = _dot_nt(ch['x'], _stack2(ch['kt'].astype(BF16)))
    for ch in chains:
        ch['sc_b'] = _dot_nt(ch['x'], _stack2(ch['bt'].astype(BF16)))
    for ch in chains:
        strict = (tcol < trow) if ch['fwd'] else (tcol > trow)
        incl = (tcol <= trow) if ch['fwd'] else (tcol >= trow)
        ch['a_kk'] = jnp.concatenate([jnp.where(strict, ch['sc_k'][:C], 0.0),
                                      jnp.where(incl, ch['sc_k'][C:], 0.0)], axis=0)
        ch['n'] = jnp.where(strict, ch['sc_b'][:C], 0.0)
        ch['a_rb'] = jnp.where(incl, ch['sc_b'][C:], 0.0)
        ch['t'] = eye + ch['n']
    for ch in chains:
        ch['p2'] = _pm(ch['a_kk'], ch['v'])
    for _ in range(5):
        for ch in chains:
            ch['n'] = _pm(ch['n'], ch['n'])
        for ch in chains:
            ch['t'] = ch['t'] + _pm(ch['t'], ch['n'])
    for ch in chains:
        ch['ta'] = jnp.concatenate([ch['t'], _pm(ch['a_rb'], ch['t'])], axis=0)
    for ch in chains:
        ch['p1'] = _dot_nt(ch['x'], ch['S'].astype(BF16))
    for ch in chains:
        ch['uy'] = _pm(ch['ta'], ch['p1'][:C] + ch['p2'][:C])
    for ch in chains:
        ch['y'] = ch['p1'][C:] + ch['p2'][C:] + ch['uy'][C:]
        vu_t = jnp.concatenate([ch['v'], ch['uy'][:C]], axis=0).T
        upd = _dot(vu_t.astype(BF16), ch['kb'].astype(BF16))
        ch['s_new'] = ch['S'] * ch['e_tot'] + jnp.where(hr == hc, upd, 0.0)


def _scan_kernel(rf, kkf, vf, lwf, bf, kdf, rb, kkb, vb, lwb, bb, kdb, yf_o, yb_o, sf_ref, sb_ref, *, nc):
    @pl.when(pl.program_id(1) == 0)
    def _():
        sf_ref[...] = jnp.zeros_like(sf_ref)
        sb_ref[...] = jnp.zeros_like(sb_ref)

    npairs = RWKV_W // LANES

    def body(c, carry):
        slf = pl.ds(pl.multiple_of(c * CHUNK, CHUNK), CHUNK)
        slb = pl.ds(pl.multiple_of((nc - 1 - c) * CHUNK, CHUNK), CHUNK)
        dirs = (_wkv_scaled(rf[slf, :], kkf[slf, :], vf[slf, :], lwf[slf, :], bf[slf, :], kdf[slf, :], True),
                _wkv_scaled(rb[slb, :], kkb[slb, :], vb[slb, :], lwb[slb, :], bb[slb, :], kdb[slb, :], False))
        chains = []
        for d, s_ref in zip(dirs, (sf_ref, sb_ref)):
            for p in range(npairs):
                sl = slice(p * LANES, (p + 1) * LANES)
                ch = {k: (val if k == 'fwd' else val[:, sl]) for k, val in d.items()}
                ch['S'] = s_ref[p]
                chains.append(ch)
        _wkv_pairs(chains)
        for di, (s_ref, y_o, sl) in enumerate(((sf_ref, yf_o, slf), (sb_ref, yb_o, slb))):
            for p in range(npairs):
                s_ref[p] = chains[di * npairs + p]['s_new']
            y_o[sl, :] = jnp.concatenate([chains[di * npairs + p]['y'] for p in range(npairs)], axis=1)
        return carry

    lax.fori_loop(0, nc, body, 0)


def _scan(r, kk, v, lw0, b0, kd0, lw1, b1, kd1, B, S, tb):
    nblk = S // tb
    shp = (B, S, RWKV_W)
    args = [a.reshape(shp) for a in (r, kk, v, lw0, b0, kd0, r, kk, v, lw1, b1, kd1)]
    fwd = pl.BlockSpec((None, tb, RWKV_W), lambda bi, j: (bi, j, 0))
    bwd = pl.BlockSpec((None, tb, RWKV_W), lambda bi, j: (bi, nblk - 1 - j, 0))
    yf, yb = pl.pallas_call(
        functools.partial(_scan_kernel, nc=tb // CHUNK),
        grid=(B, nblk),
        in_specs=[fwd] * 6 + [bwd] * 6,
        out_specs=[fwd, bwd],
        out_shape=[jax.ShapeDtypeStruct(shp, F32)] * 2,
        scratch_shapes=[pltpu.VMEM((RWKV_W // LANES, LANES, LANES), F32)] * 2,
        compiler_params=_cparams(("parallel", "arbitrary")),
        name="wkv_scan",
    )(*args)
    return yf.reshape(B * S, RWKV_W), yb.reshape(B * S, RWKV_W)


def _attn_kernel(q_ref, km_ref, vm_ref, kp_ref, vp_ref, kn_ref, vn_ref, o_ref, lse_ref, *, Q, L):
    j = pl.program_id(2)
    q = q_ref[...] * (HEAD_DIM ** -0.5)
    kw = jnp.concatenate([kp_ref[...], km_ref[...], kn_ref[...]], axis=0).astype(BF16)
    vw = jnp.concatenate([vp_ref[...], vm_ref[...], vn_ref[...]], axis=0).astype(BF16)
    Wn = Q + 2 * ATTN_HALF
    qi = lax.broadcasted_iota(jnp.int32, (Q, Wn), 0)
    wc = lax.broadcasted_iota(jnp.int32, (Q, Wn), 1)
    rel = wc - ATTN_HALF - qi
    kpos = j * Q - ATTN_HALF + wc
    valid = jnp.minimum(jnp.minimum(ATTN_HALF - jnp.abs(rel), kpos), L - 1 - kpos) >= 0
    lane = lax.broadcasted_iota(jnp.int32, (Q, ATTN_OUT), 1) // HEAD_DIM
    o_acc = jnp.zeros((Q, ATTN_OUT), F32)
    l_acc = jnp.zeros((Q, ATTN_OUT), F32)
    for hh in range(ATTN_HEADS_PER_GROUP):
        mh = lane == hh
        qh = jnp.where(mh, q, 0.0).astype(BF16)
        s = jnp.where(valid, _dot_nt(qh, kw), NEG_INF)
        m = jnp.max(s, axis=-1, keepdims=True)
        p = jnp.exp(s - m)
        den = jnp.sum(p, axis=-1, keepdims=True)
        oh = _dot(p.astype(BF16), vw) / den
        o_acc = jnp.where(mh, oh, o_acc)
        l_acc = jnp.where(mh, m + jnp.log(den), l_acc)
    o_ref[...] = o_acc
    lse_ref[...] = l_acc


def _attn(qkv, B, S, dil):
    L = S // dil
    Q = min(256, L)
    nq = L // Q
    qb = Q // ATTN_HALF
    nhb = L // ATTN_HALF

    def mid(col):
        return pl.BlockSpec((None, None, Q, ATTN_OUT), lambda bi, r, j: (bi, r, j, col))

    def prev(col):
        return pl.BlockSpec((None, None, ATTN_HALF, ATTN_OUT),
                            lambda bi, r, j: (bi, r, jnp.maximum(j * qb - 1, 0), col))

    def nxt(col):
        return pl.BlockSpec((None, None, ATTN_HALF, ATTN_OUT),
                            lambda bi, r, j: (bi, r, jnp.minimum((j + 1) * qb, nhb - 1), col))

    out = pl.BlockSpec((None, None, Q, ATTN_OUT), lambda bi, r, j: (bi, r, j, 0))
    return pl.pallas_call(
        functools.partial(_attn_kernel, Q=Q, L=L),
        grid=(B, dil, nq),
        in_specs=[mid(0), mid(1), mid(2), prev(1), prev(2), nxt(1), nxt(2)],
        out_specs=[out, out],
        out_shape=[jax.ShapeDtypeStruct((B, dil, L, ATTN_OUT), F32)] * 2,
        compiler_params=_cparams(("parallel", "parallel", "parallel")),
        name="band_attn_d%d" % dil,
    )(qkv, qkv, qkv, qkv, qkv, qkv, qkv)


def _post_kernel(x_ref, yf_ref, yb_ref, bonus_ref, g_ref, o0, l0, o1, l1, o2, l2, lng_ref, lnb_ref, wo_ref,
                 out_ref, oscr, lscr, *, tm):
    ones = _head_ones()
    y = yf_ref[...] + yb_ref[...]
    yc = y - _head_sum(y, ones) * (1.0 / HEAD_DIM)
    var = _head_sum(yc * yc, ones) * (1.0 / HEAD_DIM)
    yn = yc * lax.rsqrt(var + GN_EPS) * lng_ref[...] + lnb_ref[...]
    ya = (yn + bonus_ref[...]) * g_ref[...]
    nlb = ATTN_OUT // LANES
    os_, ls = [], []
    for gi, (o_r, l_r, (_, dil)) in enumerate(zip((o0, o1, o2), (l0, l1, l2), ATTN_PATTERNS)):
        if dil == 1:
            os_.append(o_r[0])
            ls.append(l_r[0])
            continue
        for c in range(nlb):
            for r in range(dil):
                oscr[gi * nlb + c, pl.ds(r, tm // dil, stride=dil), :] = o_r[r, :, c * LANES:(c + 1) * LANES]
                lscr[gi * nlb + c, pl.ds(r, tm // dil, stride=dil), :] = l_r[r, :, c * LANES:(c + 1) * LANES]
        os_.append(jnp.concatenate([oscr[gi * nlb + c] for c in range(nlb)], axis=1))
        ls.append(jnp.concatenate([lscr[gi * nlb + c] for c in range(nlb)], axis=1))
    m = jnp.maximum(jnp.maximum(ls[0], ls[1]), ls[2])
    es = [jnp.exp(l - m) for l in ls]
    num = es[0] * os_[0] + es[1] * os_[1] + es[2] * os_[2]
    yb = num / (es[0] + es[1] + es[2])
    out_ref[...] = (x_ref[...] + _dot(ya.astype(BF16), wo_ref[:RWKV_W, :])
                    + _dot(yb.astype(BF16), wo_ref[RWKV_W:, :]))


def _post(x2, yf, yb, bonus, g, attn, lng, lnb, wo, B, S, tm):
    T = B * S
    nts = S // tm
    row = lambda w: pl.BlockSpec((tm, w), lambda i: (i, 0))
    in_specs = [row(D_MODEL)] + [row(RWKV_W)] * 4
    args = [x2, yf, yb, bonus, g]
    for (o, l), (_, dil) in zip(attn, ATTN_PATTERNS):
        spec = pl.BlockSpec((None, dil, tm // dil, ATTN_OUT), lambda i: (i // nts, 0, i % nts, 0))
        in_specs += [spec, spec]
        args += [o, l]
    in_specs += [_full(lng.shape), _full(lnb.shape), _full(wo.shape)]
    args += [lng, lnb, wo]
    ng = len(ATTN_PATTERNS)
    return pl.pallas_call(
        functools.partial(_post_kernel, tm=tm),
        grid=(T // tm,),
        in_specs=in_specs,
        out_specs=row(D_MODEL),
        out_shape=jax.ShapeDtypeStruct((T, D_MODEL), F32),
        scratch_shapes=[pltpu.VMEM((ng * ATTN_OUT // LANES, tm, LANES), F32)] * 2,
        compiler_params=_cparams(("parallel",)),
        name="mixer_ab_out",
    )(*args)


def _pool_kernel(x_ref, xp_ref, xn_ref, g_ref, win_ref, wg_ref, sc_ref, wo_ref, out_ref, u_ref, *, tm, nts, S):
    it = pl.program_id(0) % nts
    x = x_ref[...]
    g = g_ref[...]
    hp = jnp.where(it == 0, 0.0, _rms(xp_ref[...], g))
    hn = jnp.where(it == nts - 1, 0.0, _rms(xn_ref[...], g))
    h = jnp.concatenate([hp, _rms(x, g), hn], axis=0).astype(BF16)
    u_ref[...] = _dot(h, win_ref[...])
    pos = it * tm + lax.broadcasted_iota(jnp.int32, (tm, 1), 0)
    acc = x
    for gi, win in enumerate(POOL_WINDOWS):
        rad = win // 2
        cols = slice(gi * POOL_GROUP, (gi + 1) * POOL_GROUP)
        ws = u_ref[pl.ds(HALO, tm), cols]
        u_c = ws
        for d in range(1, rad + 1):
            ws = ws + (u_ref[pl.ds(HALO - d, tm), cols] + u_ref[pl.ds(HALO + d, tm), cols])
        cnt = (jnp.minimum(pos + rad + 1, S) - jnp.maximum(pos - rad, 0)).astype(F32)
        dlt = (ws / cnt - u_c).astype(BF16)
        yg = _dot(dlt, wg_ref[gi]) * sc_ref[:, cols]
        acc = acc + _dot(yg.astype(BF16), wo_ref[cols, :])
    out_ref[...] = acc


def _pool(x2, g, w_in, w_group, scale, w_out, B, S, tm):
    T = B * S
    nts = S // tm
    prev, nxt = _halo_specs(tm, D_MODEL, T)
    row = pl.BlockSpec((tm, D_MODEL), lambda i: (i, 0))
    return pl.pallas_call(
        functools.partial(_pool_kernel, tm=tm, nts=nts, S=S),
        grid=(T // tm,),
        in_specs=[row, prev, nxt, _full(g.shape), _full(w_in.shape), _full(w_group.shape), _full(scale.shape),
                  _full(w_out.shape)],
        out_specs=row,
        out_shape=jax.ShapeDtypeStruct((T, D_MODEL), F32),
        scratch_shapes=[pltpu.VMEM((tm + 2 * HALO, POOL_W), F32)],
        compiler_params=_cparams(("parallel",)),
        name="pool_mixer",
    )(x2, x2, x2, g, w_in, w_group, scale, w_out)


def _gelu(x):
    return x * (0.5 * (1.0 + jnp.tanh(math.sqrt(2.0 / math.pi) * (x + 0.044715 * (x * x * x)))))


def _ffn_kernel(x_ref, xp_ref, xn_ref, p_ref, gf_ref, wg_ref, wv_ref, cwg_ref, cwv_ref, cbg_ref, cbv_ref, wd_ref,
                gp_ref, wpg_ref, bpg_ref, wpp_ref, gfin_ref, out_ref, h_ref, acc_ref, ug_ref, uv_ref,
                *, tm, nts, nf, final):
    it = pl.program_id(0) % nts
    f = pl.program_id(1)

    @pl.when(f == 0)
    def _():
        g = gf_ref[...]
        h_ref[pl.ds(0, HALO), :] = jnp.where(it == 0, 0.0, _rms(xp_ref[...], g)).astype(BF16)
        h_ref[pl.ds(HALO, tm), :] = _rms(x_ref[...], g).astype(BF16)
        h_ref[pl.ds(HALO + tm, HALO), :] = jnp.where(it == nts - 1, 0.0, _rms(xn_ref[...], g)).astype(BF16)
        acc_ref[...] = jnp.zeros_like(acc_ref)

    h = h_ref[...]
    ug_ref[...] = _dot(h, wg_ref[...])
    uv_ref[...] = _dot(h, wv_ref[...])

    def conv(u_ref, cw_ref, cb_ref):
        return (u_ref[pl.ds(HALO - 1, tm), :] * cw_ref[0:1, :] + u_ref[pl.ds(HALO, tm), :] * cw_ref[1:2, :]
                + u_ref[pl.ds(HALO + 1, tm), :] * cw_ref[2:3, :] + cb_ref[...])

    act = _gelu(conv(ug_ref, cwg_ref, cbg_ref)) * conv(uv_ref, cwv_ref, cbv_ref)
    acc_ref[...] += _dot(act.astype(BF16), wd_ref[...])

    @pl.when(f == nf - 1)
    def _():
        x2 = x_ref[...] + acc_ref[...]
        h3 = _rms(x2, gp_ref[...]).astype(BF16)
        gate = _sigmoid(_dot(h3, wpg_ref[...]) + bpg_ref[...])
        x3 = x2 + gate * _dot(p_ref[...].astype(BF16), wpp_ref[...])
        out_ref[...] = _rms(x3, gfin_ref[...]) if final else x3


def _ffn(x2, p_all, layer, w, B, S, tm, tf, final):
    T = B * S
    nt = T // tm
    nts = S // tm
    nf = D_FF // tf
    nb = T // HALO
    step = tm // HALO
    row = pl.BlockSpec((tm, D_MODEL), lambda i, f: (i, 0))
    prev = pl.BlockSpec((HALO, D_MODEL), lambda i, f: (jnp.maximum(i * step - 1, 0), 0))
    nxt = pl.BlockSpec((HALO, D_MODEL), lambda i, f: (jnp.minimum((i + 1) * step, nb - 1), 0))
    full = lambda a: pl.BlockSpec(a.shape, lambda i, f: (0,) * a.ndim)
    colg = lambda rows: pl.BlockSpec((rows, tf), lambda i, f: (0, f))
    colv = lambda rows: pl.BlockSpec((rows, tf), lambda i, f: (0, nf + f))
    in_specs = [row, prev, nxt,
                pl.BlockSpec((tm, PLE_DIM), lambda i, f: (layer * nt + i, 0)),
                full(w['gf']), colg(D_MODEL), colv(D_MODEL), colg(3), colv(3), colg(1), colv(1),
                pl.BlockSpec((tf, D_MODEL), lambda i, f: (f, 0)),
                full(w['gp']), full(w['wpg']), full(w['bpg']), full(w['wpp']), full(w['gfin'])]
    return pl.pallas_call(
        functools.partial(_ffn_kernel, tm=tm, nts=nts, nf=nf, final=final),
        grid=(nt, nf),
        in_specs=in_specs,
        out_specs=row,
        out_shape=jax.ShapeDtypeStruct((T, D_MODEL), F32),
        scratch_shapes=[pltpu.VMEM((tm + 2 * HALO, D_MODEL), BF16), pltpu.VMEM((tm, D_MODEL), F32),
                        pltpu.VMEM((tm + 2 * HALO, tf), F32), pltpu.VMEM((tm + 2 * HALO, tf), F32)],
        compiler_params=_cparams(("parallel", "arbitrary")),
        name="convffn_ple",
    )(x2, x2, x2, p_all, w['gf'], w['wup'], w['wup'], w['cw'], w['cw'], w['cb'], w['cb'], w['wd'],
      w['gp'], w['wpg'], w['bpg'], w['wpp'], w['gfin'])


def _rope_tables(S):
    half = ROT_DIM // 2
    inv = jnp.float32(ROPE_THETA) ** (-jnp.arange(half, dtype=F32) * 2.0 / ROT_DIM)
    ang = jnp.arange(S, dtype=F32)[:, None] * inv[None, :]
    cos = jnp.cos(ang)
    sin = jnp.sin(ang)
    pad1 = jnp.ones((S, HEAD_DIM - ROT_DIM), F32)
    pad0 = jnp.zeros((S, HEAD_DIM - ROT_DIM), F32)
    cos_h = jnp.concatenate([cos, cos, pad1], axis=1)
    sin_h = jnp.concatenate([-sin, sin, pad0], axis=1)
    reps = LANES // HEAD_DIM
    return jnp.tile(cos_h, (1, reps)), jnp.tile(sin_h, (1, reps))


def _pack_ab_w_in(w):
    parts = [w[:, :RWKV_IN]]
    qkv = [w[:, RWKV_IN + s * ATTN_HEADS * HEAD_DIM: RWKV_IN + (s + 1) * ATTN_HEADS * HEAD_DIM] for s in range(3)]
    for g in range(len(ATTN_PATTERNS)):
        parts += [t[:, g * ATTN_OUT:(g + 1) * ATTN_OUT] for t in qkv]
    return jnp.concatenate(parts, axis=1).astype(BF16)


def _prepare(prm):
    vec = lambda a: a.reshape(1, -1).astype(F32)
    pk = {'ab_w_in': [_pack_ab_w_in(prm['ab_w_in'][j]) for j in range(prm['ab_w_in'].shape[0])],
          'ab_w_out': prm['ab_w_out'].astype(BF16),
          'c_w_in': prm['c_w_in'].astype(BF16), 'c_w_group': prm['c_w_group'].astype(BF16),
          'c_w_out': prm['c_w_out'].astype(BF16), 'ffn': []}
    for i in range(DEPTH):
        pk['ffn'].append({
            'gf': vec(prm['norm_ffn_g'][i]), 'wup': prm['ffn_w_up'][i].astype(BF16),
            'cw': prm['ffn_conv_w'][i], 'cb': vec(prm['ffn_conv_b'][i]),
            'wd': prm['ffn_w_down'][i].astype(BF16), 'gp': vec(prm['norm_ple_g'][i]),
            'wpg': prm['ple_w_gate'][i].astype(BF16), 'bpg': vec(prm['ple_b_gate'][i]),
            'wpp': prm['ple_w_proj'][i].astype(BF16), 'gfin': vec(prm['norm_final_g'])})
    return pk


def _trunk(x, p, prm, pk, tiles):
    B, S, _ = x.shape
    T = B * S
    vec = lambda a: a.reshape(1, -1)
    x2 = x.reshape(T, D_MODEL)
    p_all = p.reshape(p.shape[0] * T, PLE_DIM)
    cos_t, sin_t = _rope_tables(S)
    for i in range(DEPTH):
        j = i // 2
        gmix = vec(prm['norm_mix_g'][i])
        if i % 2 == 0:
            zr, q0, q1, q2 = _inproj(x2, gmix, pk['ab_w_in'][j], cos_t, sin_t, B, S, tiles['inproj'])
            r, kk, v, g, bonus, lw0, b0, kd0, lw1, b1, kd1 = _prep(zr, prm, j, B, S, tiles['prep'])
            yf, yb = _scan(r, kk, v, lw0, b0, kd0, lw1, b1, kd1, B, S, tiles['scan'])
            attn = [_attn(q, B, S, dil) for q, (_, dil) in zip((q0, q1, q2), ATTN_PATTERNS)]
            x2 = _post(x2, yf, yb, bonus, g, attn, vec(prm['rwkv_ln_g'][j]), vec(prm['rwkv_ln_b'][j]),
                       pk['ab_w_out'][j], B, S, tiles['post'])
        else:
            x2 = _pool(x2, gmix, pk['c_w_in'][j], pk['c_w_group'][j], vec(prm['c_scale'][j]), pk['c_w_out'][j],
                       B, S, tiles['pool'])
        x2 = _ffn(x2, p_all, i, pk['ffn'][i], B, S, tiles['ffn_m'], tiles['ffn_f'], final=(i == DEPTH - 1))
    return x2.reshape(B, S, D_MODEL)


_TILES = {'inproj': 512, 'prep': 256, 'scan': 256, 'post': 512, 'pool': 512, 'ffn_m': 1024, 'ffn_f': 256}


def kernel(x_prompt, x_sample, p_prompt, p_sample, ab_w_in, ab_w_out, rwkv_mu, rwkv_w0, rwkv_w_up, rwkv_a0, rwkv_a_up, rwkv_g_up, rwkv_k_k, rwkv_k_a, rwkv_r_k, rwkv_ln_g, rwkv_ln_b, c_w_in, c_w_group, c_scale, c_w_out, norm_mix_g, norm_ffn_g, norm_ple_g, norm_final_g, ffn_w_up, ffn_conv_w, ffn_conv_b, ffn_w_down, ple_w_proj, ple_w_gate, ple_b_gate):
    prm = {
        'ab_w_in': ab_w_in, 'ab_w_out': ab_w_out, 'rwkv_mu': rwkv_mu, 'rwkv_w0': rwkv_w0,
        'rwkv_w_up': rwkv_w_up, 'rwkv_a0': rwkv_a0, 'rwkv_a_up': rwkv_a_up, 'rwkv_g_up': rwkv_g_up,
        'rwkv_k_k': rwkv_k_k, 'rwkv_k_a': rwkv_k_a, 'rwkv_r_k': rwkv_r_k, 'rwkv_ln_g': rwkv_ln_g,
        'rwkv_ln_b': rwkv_ln_b, 'c_w_in': c_w_in, 'c_w_group': c_w_group, 'c_scale': c_scale,
        'c_w_out': c_w_out, 'norm_mix_g': norm_mix_g, 'norm_ffn_g': norm_ffn_g, 'norm_ple_g': norm_ple_g,
        'norm_final_g': norm_final_g, 'ffn_w_up': ffn_w_up, 'ffn_conv_w': ffn_conv_w,
        'ffn_conv_b': ffn_conv_b, 'ffn_w_down': ffn_w_down, 'ple_w_proj': ple_w_proj,
        'ple_w_gate': ple_w_gate, 'ple_b_gate': ple_b_gate,
    }
    pk = _prepare(prm)
    y_prompt = _trunk(x_prompt, p_prompt, prm, pk, _TILES)
    y_sample = _trunk(x_sample, p_sample, prm, pk, _TILES)
    return (y_prompt, y_sample)
```

```python
import functools
import math

import jax
import jax.numpy as jnp
from jax import lax
from jax.experimental import pallas as pl
from jax.experimental.pallas import tpu as pltpu

D_MODEL = 1024
DEPTH = 4
PLE_DIM = 256
HEAD_DIM = 64
RMS_EPS = 1e-6
RWKV_HEADS = 8
RWKV_W = RWKV_HEADS * HEAD_DIM
DECAY_LORA = 64
ICLR_LORA = 64
GATE_LORA = 128
RWKV_IN = 3 * RWKV_W + 2 * DECAY_LORA + 2 * ICLR_LORA + GATE_LORA
GN_EPS = 64e-5
ATTN_PATTERNS = ((128, 1), (512, 4), (2048, 16))
ATTN_HEADS_PER_GROUP = 4
ATTN_HEADS = ATTN_HEADS_PER_GROUP * len(ATTN_PATTERNS)
ATTN_OUT = ATTN_HEADS_PER_GROUP * HEAD_DIM
ATTN_GROUP_IN = 3 * ATTN_OUT
ATTN_HALF = 64
ROPE_THETA = 500000.0
ROT_DIM = HEAD_DIM // 4
NEG_INF = -1e30
AB_OUT = RWKV_W + ATTN_OUT
POOL_WINDOWS = (2, 4, 8, 16)
POOL_GROUP = 256
POOL_W = len(POOL_WINDOWS) * POOL_GROUP
D_FF = 2816

LANES = 128
SUBLANES = 8
HALO = SUBLANES
CHUNK = 64
VMEM_LIMIT = 56 * 1024 * 1024

F32 = jnp.float32
BF16 = jnp.bfloat16


def _dot(a, b):
    return jnp.dot(a, b, preferred_element_type=F32)


def _dot_nt(a, b):
    return lax.dot_general(a, b, (((1,), (1,)), ((), ())), preferred_element_type=F32)


def _split(x):
    hi = x.astype(BF16)
    lo = (x - hi.astype(F32)).astype(BF16)
    return hi, lo


def _dot3(ap, bp):
    return _dot(ap[0], bp[0]) + (_dot(ap[0], bp[1]) + _dot(ap[1], bp[0]))


def _dot3_nt(ap, bp):
    return _dot_nt(ap[0], bp[0]) + (_dot_nt(ap[0], bp[1]) + _dot_nt(ap[1], bp[0]))


def _rms(x, g):
    return x * lax.rsqrt(jnp.mean(x * x, axis=-1, keepdims=True) + RMS_EPS) * g


def _sigmoid(x):
    return 1.0 / (1.0 + jnp.exp(-x))


def _head_ones():
    r = lax.broadcasted_iota(jnp.int32, (LANES, LANES), 0) // HEAD_DIM
    c = lax.broadcasted_iota(jnp.int32, (LANES, LANES), 1) // HEAD_DIM
    return jnp.where(r == c, 1.0, 0.0).astype(BF16)


def _head_sum(x, ones):
    outs = []
    for j in range(x.shape[1] // LANES):
        hi, lo = _split(x[:, j * LANES:(j + 1) * LANES])
        outs.append(_dot(hi, ones) + _dot(lo, ones))
    return jnp.concatenate(outs, axis=1)


def _shift_rows(z, prev_row, next_row):
    n = z.shape[0]
    row = lax.broadcasted_iota(jnp.int32, z.shape, 0)
    zp = jnp.where(row == 0, prev_row, pltpu.roll(z, 1, 0))
    zn = jnp.where(row == n - 1, next_row, pltpu.roll(z, n - 1, 0))
    return zp, zn


def _cparams(sem):
    return pltpu.CompilerParams(dimension_semantics=sem, vmem_limit_bytes=VMEM_LIMIT)


def _halo_specs(tm, width, n_rows):
    nb = n_rows // HALO
    step = tm // HALO
    prev = pl.BlockSpec((HALO, width), lambda i: (jnp.maximum(i * step - 1, 0), 0))
    nxt = pl.BlockSpec((HALO, width), lambda i: (jnp.minimum((i + 1) * step, nb - 1), 0))
    return prev, nxt


def _full(shape):
    nd = len(shape)
    return pl.BlockSpec(shape, lambda *_: (0,) * nd)


def _resident(shape):
    nd = len(shape)
    return pl.BlockSpec(shape, lambda *_: (0,) * nd, pipeline_mode=pl.Buffered(1))


def _inproj_kernel(x_ref, g_ref, w_ref, cos_ref, sin_ref, zr_ref, q0_ref, q1_ref, q2_ref, scr_ref, *, tm):
    h = _rms(x_ref[...], g_ref[...]).astype(BF16)
    zr_ref[...] = _dot(h, w_ref[:, :RWKV_IN])
    cs = cos_ref[...]
    sn = sin_ref[...]
    lane = lax.broadcasted_iota(jnp.int32, (tm, LANES), 1)
    first = (lane & (HEAD_DIM - 1)) < (ROT_DIM // 2)
    outs = (q0_ref, q1_ref, q2_ref)
    for g, (_, dil) in enumerate(ATTN_PATTERNS):
        base = RWKV_IN + g * ATTN_GROUP_IN
        t = _dot(h, w_ref[:, base:base + ATTN_GROUP_IN])
        for c in range(ATTN_GROUP_IN // LANES):
            tc = t[:, c * LANES:(c + 1) * LANES]
            if c < 2 * ATTN_OUT // LANES:
                rot = jnp.where(first, pltpu.roll(tc, LANES - ROT_DIM // 2, 1), pltpu.roll(tc, ROT_DIM // 2, 1))
                tc = tc * cs + rot * sn
            if dil == 1:
                outs[g][0, :, c * LANES:(c + 1) * LANES] = tc
            else:
                scr_ref[c] = tc
        if dil > 1:
            for c in range(ATTN_GROUP_IN // LANES):
                for r in range(dil):
                    outs[g][r, :, c * LANES:(c + 1) * LANES] = scr_ref[c, pl.ds(r, tm // dil, stride=dil), :]


def _inproj(x2, g, w, cos_t, sin_t, B, S, tm):
    T = B * S
    nts = S // tm
    out_shape = [jax.ShapeDtypeStruct((T, RWKV_IN), F32)]
    out_specs = [pl.BlockSpec((tm, RWKV_IN), lambda i: (i, 0))]
    for _, dil in ATTN_PATTERNS:
        out_shape.append(jax.ShapeDtypeStruct((B, dil, S // dil, ATTN_GROUP_IN), F32))
        out_specs.append(pl.BlockSpec((None, dil, tm // dil, ATTN_GROUP_IN),
                                      lambda i: (i // nts, 0, i % nts, 0)))
    return pl.pallas_call(
        functools.partial(_inproj_kernel, tm=tm),
        grid=(T // tm,),
        in_specs=[pl.BlockSpec((tm, D_MODEL), lambda i: (i, 0)),
                  _full((1, D_MODEL)),
                  _full(w.shape),
                  pl.BlockSpec((tm, LANES), lambda i: (i % nts, 0)),
                  pl.BlockSpec((tm, LANES), lambda i: (i % nts, 0))],
        out_specs=out_specs,
        out_shape=out_shape,
        scratch_shapes=[pltpu.VMEM((ATTN_GROUP_IN // LANES, tm, LANES), F32)],
        compiler_params=_cparams(("parallel",)),
        name="inproj",
    )(x2, g, w, cos_t, sin_t)


def _prep_kernel(z_ref, zp_ref, zn_ref, mu_ref, w0_ref, wuh_ref, wul_ref, a0_ref, auh_ref, aul_ref,
                 guh_ref, gul_ref, kk_ref, ka_ref, rk_ref,
                 r_o, kk_o, v_o, g_o, bonus_o, lw0_o, b0_o, kd0_o, lw1_o, b1_o, kd1_o, *, nts):
    it = pl.program_id(0) % nts
    z = z_ref[...]
    prev_row = jnp.where(it == 0, 0.0, zp_ref[HALO - 1:HALO, :])
    next_row = jnp.where(it == nts - 1, 0.0, zn_ref[0:1, :])
    zp, zn = _shift_rows(z, prev_row, next_row)
    zz = z + mu_ref[...] * (0.5 * (zp + zn) - z)
    W = RWKV_W
    r = zz[:, 0:W]
    k = zz[:, W:2 * W]
    v = zz[:, 2 * W:3 * W]
    o = 3 * W
    wd = jnp.tanh(zz[:, o:o + 2 * DECAY_LORA])
    o += 2 * DECAY_LORA
    ad = zz[:, o:o + 2 * ICLR_LORA]
    o += 2 * ICLR_LORA
    gd = _sigmoid(zz[:, o:])
    wlog = w0_ref[...] + _dot3(_split(wd), (wuh_ref[...], wul_ref[...]))
    aa = _sigmoid(a0_ref[...] + _dot3(_split(ad), (auh_ref[...], aul_ref[...])))
    g_o[...] = _dot3(_split(gd), (guh_ref[...], gul_ref[...]))
    ones = _head_ones()
    kkv = k * kk_ref[...]
    kkn = kkv / jnp.sqrt(_head_sum(kkv * kkv, ones) + 1e-12)
    r_o[...] = r
    kk_o[...] = kkn
    v_o[...] = v
    ka = ka_ref[...]
    kd_sum = None
    for d, (lw_o, b_o, kd_o) in enumerate(((lw0_o, b0_o, kd0_o), (lw1_o, b1_o, kd1_o))):
        a_d = aa[:, d * W:(d + 1) * W]
        lw_o[...] = -math.exp(-0.5) * _sigmoid(wlog[:, d * W:(d + 1) * W])
        kd = k * (1.0 + (a_d - 1.0) * ka)
        kd_o[...] = kd
        b_o[...] = kkn * a_d
        kd_sum = kd if kd_sum is None else kd_sum + kd
    bonus_o[...] = _head_sum(r * kd_sum * rk_ref[...], ones) * v


def _prep(zr, prm, j, B, S, tm):
    T = B * S
    nts = S // tm
    prev, nxt = _halo_specs(tm, RWKV_IN, T)
    W = RWKV_W
    vec = lambda a: a.reshape(1, -1)

    def bdiag(w):
        L = w.shape[1]
        m = jnp.zeros((2 * L, 2 * W), F32).at[:L, :W].set(w[0]).at[L:, W:].set(w[1])
        return _split(m)

    wuh, wul = bdiag(prm['rwkv_w_up'][j])
    auh, aul = bdiag(prm['rwkv_a_up'][j])
    guh, gul = _split(prm['rwkv_g_up'][j])
    params = [vec(prm['rwkv_mu'][j]), vec(prm['rwkv_w0'][j]), wuh, wul, vec(prm['rwkv_a0'][j]), auh, aul,
              guh, gul, vec(prm['rwkv_k_k'][j]), vec(prm['rwkv_k_a'][j]), vec(prm['rwkv_r_k'][j])]
    row = pl.BlockSpec((tm, W), lambda i: (i, 0))
    return pl.pallas_call(
        functools.partial(_prep_kernel, nts=nts),
        grid=(T // tm,),
        in_specs=[pl.BlockSpec((tm, RWKV_IN), lambda i: (i, 0)), prev, nxt] + [_full(a.shape) for a in params],
        out_specs=[row] * 11,
        out_shape=[jax.ShapeDtypeStruct((T, W), F32)] * 11,
        compiler_params=_cparams(("parallel",)),
        name="rwkv_prep",
    )(zr, zr, zr, *params)


def _stack2(x):
    lane = lax.broadcasted_iota(jnp.int32, x.shape, 1)
    m0 = lane < HEAD_DIM
    zero = jnp.zeros_like(x)
    return jnp.concatenate([jnp.where(m0, x, zero), jnp.where(m0, zero, x)], axis=0)


def _pm(m, x):
    return _dot(m.astype(BF16), _stack2(x.astype(BF16)))


def _wkv_scaled(r, kk, v, lw, b, kd, fwd):
    C = CHUNK
    ti = lax.broadcasted_iota(jnp.int32, (C, C), 0)
    tj = lax.broadcasted_iota(jnp.int32, (C, C), 1)
    cum = jnp.where((tj <= ti) if fwd else (tj >= ti), 1.0, 0.0).astype(BF16)
    lh, ll = _split(lw)
    G = _dot(cum, lh) + _dot(cum, ll)
    g_tot = G[C - 1:C, :] if fwd else G[0:1, :]
    e_neg = jnp.exp(-G)
    rt = r * jnp.exp(G)
    at = -kk * jnp.exp(G - lw)
    kt = kd * e_neg
    bt = b * e_neg
    e_tot = jnp.exp(g_tot)
    kb = jnp.concatenate([kt * e_tot, bt * e_tot], axis=0)
    return dict(at=at, rt=rt, kt=kt, bt=bt, kb=kb, v=v, e_tot=e_tot, fwd=fwd)


def _wkv_pairs(chains):
    C = CHUNK
    trow = lax.broadcasted_iota(jnp.int32, (C, 2 * C), 0)
    tcol = lax.broadcasted_iota(jnp.int32, (C, 2 * C), 1) & (C - 1)
    eye = jnp.where(tcol == trow, 1.0, 0.0)
    hr = lax.broadcasted_iota(jnp.int32, (LANES, LANES), 0) // HEAD_DIM
    hc = lax.broadcasted_iota(jnp.int32, (LANES, LANES), 1) // HEAD_DIM
    for ch in chains:
        ch['x'] = jnp.concatenate([ch['at'], ch['rt']], axis=0).astype(BF16)
    for ch in chains:
        ch['sc_k'] = _dot_nt(ch['x'], _stack2(ch['kt'].astype(BF16)))
    for ch in chains:
        ch['sc_b'] = _dot_nt(ch['x'], _stack2(ch['bt'].astype(BF16)))
    for ch in chains:
        strict = (tcol < trow) if ch['fwd'] else (tcol > trow)
        incl = (tcol <= trow) if ch['fwd'] else (tcol >= trow)
        ch['a_kk'] = jnp.concatenate([jnp.where(strict, ch['sc_k'][:C], 0.0),
                                      jnp.where(incl, ch['sc_k'][C:], 0.0)], axis=0)
        ch['n'] = jnp.where(strict, ch['sc_b'][:C], 0.0)
        ch['a_rb'] = jnp.where(incl, ch['sc_b'][C:], 0.0)
        ch['t'] = eye + ch['n']
    for ch in chains:
        ch['p2'] = _pm(ch['a_kk'], ch['v'])
    for _ in range(5):
        for ch in chains:
            ch['n'] = _pm(ch['n'], ch['n'])
        for ch in chains:
            ch['t'] = ch['t'] + _pm(ch['t'], ch['n'])
    for ch in chains:
        ch['ta'] = jnp.concatenate([ch['t'], _pm(ch['a_rb'], ch['t'])], axis=0)
    for ch in chains:
        ch['p1'] = _dot_nt(ch['x'], ch['S'].astype(BF16))
    for ch in chains:
        ch['uy'] = _pm(ch['ta'], ch['p1'][:C] + ch['p2'][:C])
    for ch in chains:
        ch['y'] = ch['p1'][C:] + ch['p2'][C:] + ch['uy'][C:]
        vu_t = jnp.concatenate([ch['v'], ch['uy'][:C]], axis=0).T
        upd = _dot(vu_t.astype(BF16), ch['kb'].astype(BF16))
        ch['s_new'] = ch['S'] * ch['e_tot'] + jnp.where(hr == hc, upd, 0.0)


def _scan_kernel(rf, kkf, vf, lwf, bf, kdf, rb, kkb, vb, lwb, bb, kdb, yf_o, yb_o, sf_ref, sb_ref, *, nc):
    @pl.when(pl.program_id(1) == 0)
    def _():
        sf_ref[...] = jnp.zeros_like(sf_ref)
        sb_ref[...] = jnp.zeros_like(sb_ref)

    npairs = RWKV_W // LANES

    def body(c, carry):
        slf = pl.ds(pl.multiple_of(c * CHUNK, CHUNK), CHUNK)
        slb = pl.ds(pl.multiple_of((nc - 1 - c) * CHUNK, CHUNK), CHUNK)
        dirs = (_wkv_scaled(rf[slf, :], kkf[slf, :], vf[slf, :], lwf[slf, :], bf[slf, :], kdf[slf, :], True),
                _wkv_scaled(rb[slb, :], kkb[slb, :], vb[slb, :], lwb[slb, :], bb[slb, :], kdb[slb, :], False))
        chains = []
        for d, s_ref in zip(dirs, (sf_ref, sb_ref)):
            for p in range(npairs):
                sl = slice(p * LANES, (p + 1) * LANES)
                ch = {k: (val if k == 'fwd' else val[:, sl]) for k, val in d.items()}
                ch['S'] = s_ref[p]
                chains.append(ch)
        _wkv_pairs(chains)
        for di, (s_ref, y_o, sl) in enumerate(((sf_ref, yf_o, slf), (sb_ref, yb_o, slb))):
            for p in range(npairs):
                s_ref[p] = chains[di * npairs + p]['s_new']
            y_o[sl, :] = jnp.concatenate([chains[di * npairs + p]['y'] for p in range(npairs)], axis=1)
        return carry

    lax.fori_loop(0, nc, body, 0)


def _scan(r, kk, v, lw0, b0, kd0, lw1, b1, kd1, B, S, tb):
    nblk = S // tb
    shp = (B, S, RWKV_W)
    args = [a.reshape(shp) for a in (r, kk, v, lw0, b0, kd0, r, kk, v, lw1, b1, kd1)]
    fwd = pl.BlockSpec((None, tb, RWKV_W), lambda bi, j: (bi, j, 0))
    bwd = pl.BlockSpec((None, tb, RWKV_W), lambda bi, j: (bi, nblk - 1 - j, 0))
    yf, yb = pl.pallas_call(
        functools.partial(_scan_kernel, nc=tb // CHUNK),
        grid=(B, nblk),
        in_specs=[fwd] * 6 + [bwd] * 6,
        out_specs=[fwd, bwd],
        out_shape=[jax.ShapeDtypeStruct(shp, F32)] * 2,
        scratch_shapes=[pltpu.VMEM((RWKV_W // LANES, LANES, LANES), F32)] * 2,
        compiler_params=_cparams(("parallel", "arbitrary")),
        name="wkv_scan",
    )(*args)
    return yf.reshape(B * S, RWKV_W), yb.reshape(B * S, RWKV_W)


def _attn_kernel(q_ref, km_ref, vm_ref, kp_ref, vp_ref, kn_ref, vn_ref, o_ref, lse_ref, *, Q, L):
    j = pl.program_id(2)
    q = q_ref[...] * (HEAD_DIM ** -0.5)
    kw = jnp.concatenate([kp_ref[...], km_ref[...], kn_ref[...]], axis=0).astype(BF16)
    vw = jnp.concatenate([vp_ref[...], vm_ref[...], vn_ref[...]], axis=0).astype(BF16)
    Wn = Q + 2 * ATTN_HALF
    qi = lax.broadcasted_iota(jnp.int32, (Q, Wn), 0)
    wc = lax.broadcasted_iota(jnp.int32, (Q, Wn), 1)
    rel = wc - ATTN_HALF - qi
    kpos = j * Q - ATTN_HALF + wc
    valid = jnp.minimum(jnp.minimum(ATTN_HALF - jnp.abs(rel), kpos), L - 1 - kpos) >= 0
    lane = lax.broadcasted_iota(jnp.int32, (Q, ATTN_OUT), 1) // HEAD_DIM
    o_acc = jnp.zeros((Q, ATTN_OUT), F32)
    l_acc = jnp.zeros((Q, ATTN_OUT), F32)
    for hh in range(ATTN_HEADS_PER_GROUP):
        mh = lane == hh
        qh = jnp.where(mh, q, 0.0).astype(BF16)
        s = jnp.where(valid, _dot_nt(qh, kw), NEG_INF)
        m = jnp.max(s, axis=-1, keepdims=True)
        p = jnp.exp(s - m)
        den = jnp.sum(p, axis=-1, keepdims=True)
        oh = _dot(p.astype(BF16), vw) / den
        o_acc = jnp.where(mh, oh, o_acc)
        l_acc = jnp.where(mh, m + jnp.log(den), l_acc)
    o_ref[...] = o_acc
    lse_ref[...] = l_acc


def _attn(qkv, B, S, dil):
    L = S // dil
    Q = min(256, L)
    nq = L // Q
    qb = Q // ATTN_HALF
    nhb = L // ATTN_HALF

    def mid(col):
        return pl.BlockSpec((None, None, Q, ATTN_OUT), lambda bi, r, j: (bi, r, j, col))

    def prev(col):
        return pl.BlockSpec((None, None, ATTN_HALF, ATTN_OUT),
                            lambda bi, r, j: (bi, r, jnp.maximum(j * qb - 1, 0), col))

    def nxt(col):
        return pl.BlockSpec((None, None, ATTN_HALF, ATTN_OUT),
                            lambda bi, r, j: (bi, r, jnp.minimum((j + 1) * qb, nhb - 1), col))

    out = pl.BlockSpec((None, None, Q, ATTN_OUT), lambda bi, r, j: (bi, r, j, 0))
    return pl.pallas_call(
        functools.partial(_attn_kernel, Q=Q, L=L),
        grid=(B, dil, nq),
        in_specs=[mid(0), mid(1), mid(2), prev(1), prev(2), nxt(1), nxt(2)],
        out_specs=[out, out],
        out_shape=[jax.ShapeDtypeStruct((B, dil, L, ATTN_OUT), F32)] * 2,
        compiler_params=_cparams(("parallel", "parallel", "parallel")),
        name="band_attn_d%d" % dil,
    )(qkv, qkv, qkv, qkv, qkv, qkv, qkv)


def _post_kernel(x_ref, yf_ref, yb_ref, bonus_ref, g_ref, o0, l0, o1, l1, o2, l2, lng_ref, lnb_ref, wo_ref,
                 out_ref, oscr, lscr, *, tm):
    ones = _head_ones()
    y = yf_ref[...] + yb_ref[...]
    yc = y - _head_sum(y, ones) * (1.0 / HEAD_DIM)
    var = _head_sum(yc * yc, ones) * (1.0 / HEAD_DIM)
    yn = yc * lax.rsqrt(var + GN_EPS) * lng_ref[...] + lnb_ref[...]
    ya = (yn + bonus_ref[...]) * g_ref[...]
    nlb = ATTN_OUT // LANES
    os_, ls = [], []
    for gi, (o_r, l_r, (_, dil)) in enumerate(zip((o0, o1, o2), (l0, l1, l2), ATTN_PATTERNS)):
        if dil == 1:
            os_.append(o_r[0])
            ls.append(l_r[0])
            continue
        for c in range(nlb):
            for r in range(dil):
                oscr[gi * nlb + c, pl.ds(r, tm // dil, stride=dil), :] = o_r[r, :, c * LANES:(c + 1) * LANES]
                lscr[gi * nlb + c, pl.ds(r, tm // dil, stride=dil), :] = l_r[r, :, c * LANES:(c + 1) * LANES]
        os_.append(jnp.concatenate([oscr[gi * nlb + c] for c in range(nlb)], axis=1))
        ls.append(jnp.concatenate([lscr[gi * nlb + c] for c in range(nlb)], axis=1))
    m = jnp.maximum(jnp.maximum(ls[0], ls[1]), ls[2])
    es = [jnp.exp(l - m) for l in ls]
    num = es[0] * os_[0] + es[1] * os_[1] + es[2] * os_[2]
    yb = num / (es[0] + es[1] + es[2])
    out_ref[...] = (x_ref[...] + _dot(ya.astype(BF16), wo_ref[:RWKV_W, :])
                    + _dot(yb.astype(BF16), wo_ref[RWKV_W:, :]))


def _post(x2, yf, yb, bonus, g, attn, lng, lnb, wo, B, S, tm):
    T = B * S
    nts = S // tm
    row = lambda w: pl.BlockSpec((tm, w), lambda i: (i, 0))
    in_specs = [row(D_MODEL)] + [row(RWKV_W)] * 4
    args = [x2, yf, yb, bonus, g]
    for (o, l), (_, dil) in zip(attn, ATTN_PATTERNS):
        spec = pl.BlockSpec((None, dil, tm // dil, ATTN_OUT), lambda i: (i // nts, 0, i % nts, 0))
        in_specs += [spec, spec]
        args += [o, l]
    in_specs += [_full(lng.shape), _full(lnb.shape), _full(wo.shape)]
    args += [lng, lnb, wo]
    ng = len(ATTN_PATTERNS)
    return pl.pallas_call(
        functools.partial(_post_kernel, tm=tm),
        grid=(T // tm,),
        in_specs=in_specs,
        out_specs=row(D_MODEL),
        out_shape=jax.ShapeDtypeStruct((T, D_MODEL), F32),
        scratch_shapes=[pltpu.VMEM((ng * ATTN_OUT // LANES, tm, LANES), F32)] * 2,
        compiler_params=_cparams(("parallel",)),
        name="mixer_ab_out",
    )(*args)


def _pool_kernel(x_ref, xp_ref, xn_ref, g_ref, win_ref, wg_ref, sc_ref, wo_ref, out_ref, u_ref, *, tm, nts, S):
    it = pl.program_id(0) % nts
    x = x_ref[...]
    g = g_ref[...]
    hp = jnp.where(it == 0, 0.0, _rms(xp_ref[...], g))
    hn = jnp.where(it == nts - 1, 0.0, _rms(xn_ref[...], g))
    h = jnp.concatenate([hp, _rms(x, g), hn], axis=0).astype(BF16)
    u_ref[...] = _dot(h, win_ref[...])
    pos = it * tm + lax.broadcasted_iota(jnp.int32, (tm, 1), 0)
    acc = x
    for gi, win in enumerate(POOL_WINDOWS):
        rad = win // 2
        cols = slice(gi * POOL_GROUP, (gi + 1) * POOL_GROUP)
        ws = u_ref[pl.ds(HALO, tm), cols]
        u_c = ws
        for d in range(1, rad + 1):
            ws = ws + (u_ref[pl.ds(HALO - d, tm), cols] + u_ref[pl.ds(HALO + d, tm), cols])
        cnt = (jnp.minimum(pos + rad + 1, S) - jnp.maximum(pos - rad, 0)).astype(F32)
        dlt = (ws / cnt - u_c).astype(BF16)
        yg = _dot(dlt, wg_ref[gi]) * sc_ref[:, cols]
        acc = acc + _dot(yg.astype(BF16), wo_ref[cols, :])
    out_ref[...] = acc


def _pool(x2, g, w_in, w_group, scale, w_out, B, S, tm):
    T = B * S
    nts = S // tm
    prev, nxt = _halo_specs(tm, D_MODEL, T)
    row = pl.BlockSpec((tm, D_MODEL), lambda i: (i, 0))
    return pl.pallas_call(
        functools.partial(_pool_kernel, tm=tm, nts=nts, S=S),
        grid=(T // tm,),
        in_specs=[row, prev, nxt, _full(g.shape), _full(w_in.shape), _full(w_group.shape), _full(scale.shape),
                  _full(w_out.shape)],
        out_specs=row,
        out_shape=jax.ShapeDtypeStruct((T, D_MODEL), F32),
        scratch_shapes=[pltpu.VMEM((tm + 2 * HALO, POOL_W), F32)],
        compiler_params=_cparams(("parallel",)),
        name="pool_mixer",
    )(x2, x2, x2, g, w_in, w_group, scale, w_out)


def _gelu(x):
    return x * (0.5 * (1.0 + jnp.tanh(math.sqrt(2.0 / math.pi) * (x + 0.044715 * (x * x * x)))))


def _ffn_kernel(x_ref, xp_ref, xn_ref, p_ref, gf_ref, wup_ref, cw_ref, cb_ref, wd_ref,
                gp_ref, wpg_ref, bpg_ref, wpp_ref, gfin_ref, out_ref, h_ref, *, tm, rb, nts, final):
    it = pl.program_id(0) % nts
    g = gf_ref[...]
    h_ref[pl.ds(0, HALO), :] = jnp.where(it == 0, 0.0, _rms(xp_ref[...], g)).astype(BF16)
    h_ref[pl.ds(HALO, tm), :] = _rms(x_ref[...], g).astype(BF16)
    h_ref[pl.ds(HALO + tm, HALO), :] = jnp.where(it == nts - 1, 0.0, _rms(xn_ref[...], g)).astype(BF16)
    n_ext = rb + 2 * HALO
    for s in range(tm // rb):
        r0 = s * rb
        u = _dot(h_ref[pl.ds(r0, n_ext), :], wup_ref[...])
        mid = slice(HALO, HALO + rb)
        c = (pltpu.roll(u, 1, 0)[mid] * cw_ref[0:1, :] + u[mid] * cw_ref[1:2, :]
             + pltpu.roll(u, n_ext - 1, 0)[mid] * cw_ref[2:3, :] + cb_ref[...])
        act = _gelu(c[:, :D_FF]) * c[:, D_FF:]
        rows = pl.ds(r0, rb)
        x2 = x_ref[rows, :] + _dot(act.astype(BF16), wd_ref[...])
        h3 = _rms(x2, gp_ref[...]).astype(BF16)
        gate = _sigmoid(_dot(h3, wpg_ref[...]) + bpg_ref[...])
        x3 = x2 + gate * _dot(p_ref[rows, :].astype(BF16), wpp_ref[...])
        out_ref[rows, :] = _rms(x3, gfin_ref[...]) if final else x3


def _ffn(x2, p_all, layer, w, B, S, tm, rb, final):
    T = B * S
    nt = T // tm
    nts = S // tm
    prev, nxt = _halo_specs(tm, D_MODEL, T)
    row = pl.BlockSpec((tm, D_MODEL), lambda i: (i, 0))
    names = ('gf', 'wup', 'cw', 'cb', 'wd', 'gp', 'wpg', 'bpg', 'wpp', 'gfin')
    in_specs = [row, prev, nxt, pl.BlockSpec((tm, PLE_DIM), lambda i: (layer * nt + i, 0))]
    in_specs += [_resident(w[k].shape) for k in names]
    return pl.pallas_call(
        functools.partial(_ffn_kernel, tm=tm, rb=rb, nts=nts, final=final),
        grid=(nt,),
        in_specs=in_specs,
        out_specs=row,
        out_shape=jax.ShapeDtypeStruct((T, D_MODEL), F32),
        scratch_shapes=[pltpu.VMEM((tm + 2 * HALO, D_MODEL), BF16)],
        compiler_params=_cparams(("parallel",)),
        name="convffn_ple",
    )(x2, x2, x2, p_all, *[w[k] for k in names])


def _rope_tables(S):
    half = ROT_DIM // 2
    inv = jnp.float32(ROPE_THETA) ** (-jnp.arange(half, dtype=F32) * 2.0 / ROT_DIM)
    ang = jnp.arange(S, dtype=F32)[:, None] * inv[None, :]
    cos = jnp.cos(ang)
    sin = jnp.sin(ang)
    pad1 = jnp.ones((S, HEAD_DIM - ROT_DIM), F32)
    pad0 = jnp.zeros((S, HEAD_DIM - ROT_DIM), F32)
    cos_h = jnp.concatenate([cos, cos, pad1], axis=1)
    sin_h = jnp.concatenate([-sin, sin, pad0], axis=1)
    reps = LANES // HEAD_DIM
    return jnp.tile(cos_h, (1, reps)), jnp.tile(sin_h, (1, reps))


def _pack_ab_w_in(w):
    parts = [w[:, :RWKV_IN]]
    qkv = [w[:, RWKV_IN + s * ATTN_HEADS * HEAD_DIM: RWKV_IN + (s + 1) * ATTN_HEADS * HEAD_DIM] for s in range(3)]
    for g in range(len(ATTN_PATTERNS)):
        parts += [t[:, g * ATTN_OUT:(g + 1) * ATTN_OUT] for t in qkv]
    return jnp.concatenate(parts, axis=1).astype(BF16)


def _prepare(prm):
    vec = lambda a: a.reshape(1, -1).astype(F32)
    pk = {'ab_w_in': [_pack_ab_w_in(prm['ab_w_in'][j]) for j in range(prm['ab_w_in'].shape[0])],
          'ab_w_out': prm['ab_w_out'].astype(BF16),
          'c_w_in': prm['c_w_in'].astype(BF16), 'c_w_group': prm['c_w_group'].astype(BF16),
          'c_w_out': prm['c_w_out'].astype(BF16), 'ffn': []}
    for i in range(DEPTH):
        pk['ffn'].append({
            'gf': vec(prm['norm_ffn_g'][i]), 'wup': prm['ffn_w_up'][i].astype(BF16),
            'cw': prm['ffn_conv_w'][i], 'cb': vec(prm['ffn_conv_b'][i]),
            'wd': prm['ffn_w_down'][i].astype(BF16), 'gp': vec(prm['norm_ple_g'][i]),
            'wpg': prm['ple_w_gate'][i].astype(BF16), 'bpg': vec(prm['ple_b_gate'][i]),
            'wpp': prm['ple_w_proj'][i].astype(BF16), 'gfin': vec(prm['norm_final_g'])})
    return pk


def _trunk(x, p, prm, pk, tiles):
    B, S, _ = x.shape
    T = B * S
    vec = lambda a: a.reshape(1, -1)
    x2 = x.reshape(T, D_MODEL)
    p_all = p.reshape(p.shape[0] * T, PLE_DIM)
    cos_t, sin_t = _rope_tables(S)
    for i in range(DEPTH):
        j = i // 2
        gmix = vec(prm['norm_mix_g'][i])
        if i % 2 == 0:
            zr, q0, q1, q2 = _inproj(x2, gmix, pk['ab_w_in'][j], cos_t, sin_t, B, S, tiles['inproj'])
            r, kk, v, g, bonus, lw0, b0, kd0, lw1, b1, kd1 = _prep(zr, prm, j, B, S, tiles['prep'])
            yf, yb = _scan(r, kk, v, lw0, b0, kd0, lw1, b1, kd1, B, S, tiles['scan'])
            attn = [_attn(q, B, S, dil) for q, (_, dil) in zip((q0, q1, q2), ATTN_PATTERNS)]
            x2 = _post(x2, yf, yb, bonus, g, attn, vec(prm['rwkv_ln_g'][j]), vec(prm['rwkv_ln_b'][j]),
                       pk['ab_w_out'][j], B, S, tiles['post'])
        else:
            x2 = _pool(x2, gmix, pk['c_w_in'][j], pk['c_w_group'][j], vec(prm['c_scale'][j]), pk['c_w_out'][j],
                       B, S, tiles['pool'])
        x2 = _ffn(x2, p_all, i, pk['ffn'][i], B, S, tiles['ffn_m'], tiles['ffn_f'], final=(i == DEPTH - 1))
    return x2.reshape(B, S, D_MODEL)


_TILES = {'inproj': 512, 'prep': 256, 'scan': 256, 'post': 512, 'pool': 512, 'ffn_m': 512, 'ffn_f': 256}


def kernel(x_prompt, x_sample, p_prompt, p_sample, ab_w_in, ab_w_out, rwkv_mu, rwkv_w0, rwkv_w_up, rwkv_a0, rwkv_a_up, rwkv_g_up, rwkv_k_k, rwkv_k_a, rwkv_r_k, rwkv_ln_g, rwkv_ln_b, c_w_in, c_w_group, c_scale, c_w_out, norm_mix_g, norm_ffn_g, norm_ple_g, norm_final_g, ffn_w_up, ffn_conv_w, ffn_conv_b, ffn_w_down, ple_w_proj, ple_w_gate, ple_b_gate):
    prm = {
        'ab_w_in': ab_w_in, 'ab_w_out': ab_w_out, 'rwkv_mu': rwkv_mu, 'rwkv_w0': rwkv_w0,
        'rwkv_w_up': rwkv_w_up, 'rwkv_a0': rwkv_a0, 'rwkv_a_up': rwkv_a_up, 'rwkv_g_up': rwkv_g_up,
        'rwkv_k_k': rwkv_k_k, 'rwkv_k_a': rwkv_k_a, 'rwkv_r_k': rwkv_r_k, 'rwkv_ln_g': rwkv_ln_g,
        'rwkv_ln_b': rwkv_ln_b, 'c_w_in': c_w_in, 'c_w_group': c_w_group, 'c_scale': c_scale,
        'c_w_out': c_w_out, 'norm_mix_g': norm_mix_g, 'norm_ffn_g': norm_ffn_g, 'norm_ple_g': norm_ple_g,
        'norm_final_g': norm_final_g, 'ffn_w_up': ffn_w_up, 'ffn_conv_w': ffn_conv_w,
        'ffn_conv_b': ffn_conv_b, 'ffn_w_down': ffn_w_down, 'ple_w_proj': ple_w_proj,
        'ple_w_gate': ple_w_gate, 'ple_b_gate': ple_b_gate,
    }
    pk = _prepare(prm)
    y_prompt = _trunk(x_prompt, p_prompt, prm, pk, _TILES)
    y_sample = _trunk(x_sample, p_sample, prm, pk, _TILES)
    return (y_prompt, y_sample)
```

```python
import functools
import math

import jax
import jax.numpy as jnp
from jax import lax
from jax.experimental import pallas as pl
from jax.experimental.pallas import tpu as pltpu

D_MODEL = 1024
DEPTH = 4
PLE_DIM = 256
HEAD_DIM = 64
RMS_EPS = 1e-6
RWKV_HEADS = 8
RWKV_W = RWKV_HEADS * HEAD_DIM
DECAY_LORA = 64
ICLR_LORA = 64
GATE_LORA = 128
RWKV_IN = 3 * RWKV_W + 2 * DECAY_LORA + 2 * ICLR_LORA + GATE_LORA
GN_EPS = 64e-5
ATTN_PATTERNS = ((128, 1), (512, 4), (2048, 16))
ATTN_HEADS_PER_GROUP = 4
ATTN_HEADS = ATTN_HEADS_PER_GROUP * len(ATTN_PATTERNS)
ATTN_OUT = ATTN_HEADS_PER_GROUP * HEAD_DIM
ATTN_GROUP_IN = 3 * ATTN_OUT
ATTN_HALF = 64
ROPE_THETA = 500000.0
ROT_DIM = HEAD_DIM // 4
NEG_INF = -1e30
AB_OUT = RWKV_W + ATTN_OUT
POOL_WINDOWS = (2, 4, 8, 16)
POOL_GROUP = 256
POOL_W = len(POOL_WINDOWS) * POOL_GROUP
D_FF = 2816

LANES = 128
SUBLANES = 8
HALO = SUBLANES
CHUNK = 64
VMEM_LIMIT = 56 * 1024 * 1024

F32 = jnp.float32
BF16 = jnp.bfloat16


def _dot(a, b):
    return jnp.dot(a, b, preferred_element_type=F32)


def _dot_nt(a, b):
    return lax.dot_general(a, b, (((1,), (1,)), ((), ())), preferred_element_type=F32)


def _split(x):
    hi = x.astype(BF16)
    lo = (x - hi.astype(F32)).astype(BF16)
    return hi, lo


def _dot3(ap, bp):
    return _dot(ap[0], bp[0]) + (_dot(ap[0], bp[1]) + _dot(ap[1], bp[0]))


def _dot3_nt(ap, bp):
    return _dot_nt(ap[0], bp[0]) + (_dot_nt(ap[0], bp[1]) + _dot_nt(ap[1], bp[0]))


def _rms(x, g):
    return x * lax.rsqrt(jnp.mean(x * x, axis=-1, keepdims=True) + RMS_EPS) * g


def _sigmoid(x):
    return 1.0 / (1.0 + jnp.exp(-x))


def _head_ones():
    r = lax.broadcasted_iota(jnp.int32, (LANES, LANES), 0) // HEAD_DIM
    c = lax.broadcasted_iota(jnp.int32, (LANES, LANES), 1) // HEAD_DIM
    return jnp.where(r == c, 1.0, 0.0).astype(BF16)


def _head_sum(x, ones):
    outs = []
    for j in range(x.shape[1] // LANES):
        hi, lo = _split(x[:, j * LANES:(j + 1) * LANES])
        outs.append(_dot(hi, ones) + _dot(lo, ones))
    return jnp.concatenate(outs, axis=1)


def _shift_rows(z, prev_row, next_row):
    n = z.shape[0]
    row = lax.broadcasted_iota(jnp.int32, z.shape, 0)
    zp = jnp.where(row == 0, prev_row, pltpu.roll(z, 1, 0))
    zn = jnp.where(row == n - 1, next_row, pltpu.roll(z, n - 1, 0))
    return zp, zn


def _cparams(sem):
    return pltpu.CompilerParams(dimension_semantics=sem, vmem_limit_bytes=VMEM_LIMIT)


def _halo_specs(tm, width, n_rows):
    nb = n_rows // HALO
    step = tm // HALO
    prev = pl.BlockSpec((HALO, width), lambda i: (jnp.maximum(i * step - 1, 0), 0))
    nxt = pl.BlockSpec((HALO, width), lambda i: (jnp.minimum((i + 1) * step, nb - 1), 0))
    return prev, nxt


def _full(shape):
    nd = len(shape)
    return pl.BlockSpec(shape, lambda *_: (0,) * nd)


def _resident(shape):
    nd = len(shape)
    return pl.BlockSpec(shape, lambda *_: (0,) * nd, pipeline_mode=pl.Buffered(1))


def _inproj_kernel(x_ref, g_ref, w_ref, cos_ref, sin_ref, zr_ref, q0_ref, q1_ref, q2_ref, scr_ref, *, tm):
    h = _rms(x_ref[...], g_ref[...]).astype(BF16)
    zr_ref[...] = _dot(h, w_ref[:, :RWKV_IN])
    cs = cos_ref[...]
    sn = sin_ref[...]
    lane = lax.broadcasted_iota(jnp.int32, (tm, LANES), 1)
    first = (lane & (HEAD_DIM - 1)) < (ROT_DIM // 2)
    outs = (q0_ref, q1_ref, q2_ref)
    for g, (_, dil) in enumerate(ATTN_PATTERNS):
        base = RWKV_IN + g * ATTN_GROUP_IN
        t = _dot(h, w_ref[:, base:base + ATTN_GROUP_IN])
        for c in range(ATTN_GROUP_IN // LANES):
            tc = t[:, c * LANES:(c + 1) * LANES]
            if c < 2 * ATTN_OUT // LANES:
                rot = jnp.where(first, pltpu.roll(tc, LANES - ROT_DIM // 2, 1), pltpu.roll(tc, ROT_DIM // 2, 1))
                tc = tc * cs + rot * sn
            if dil == 1:
                outs[g][0, :, c * LANES:(c + 1) * LANES] = tc
            else:
                scr_ref[c] = tc
        if dil > 1:
            for c in range(ATTN_GROUP_IN // LANES):
                for r in range(dil):
                    outs[g][r, :, c * LANES:(c + 1) * LANES] = scr_ref[c, pl.ds(r, tm // dil, stride=dil), :]


def _inproj(x2, g, w, cos_t, sin_t, B, S, tm):
    T = B * S
    nts = S // tm
    out_shape = [jax.ShapeDtypeStruct((T, RWKV_IN), F32)]
    out_specs = [pl.BlockSpec((tm, RWKV_IN), lambda i: (i, 0))]
    for _, dil in ATTN_PATTERNS:
        out_shape.append(jax.ShapeDtypeStruct((B, dil, S // dil, ATTN_GROUP_IN), F32))
        out_specs.append(pl.BlockSpec((None, dil, tm // dil, ATTN_GROUP_IN),
                                      lambda i: (i // nts, 0, i % nts, 0)))
    return pl.pallas_call(
        functools.partial(_inproj_kernel, tm=tm),
        grid=(T // tm,),
        in_specs=[pl.BlockSpec((tm, D_MODEL), lambda i: (i, 0)),
                  _full((1, D_MODEL)),
                  _full(w.shape),
                  pl.BlockSpec((tm, LANES), lambda i: (i % nts, 0)),
                  pl.BlockSpec((tm, LANES), lambda i: (i % nts, 0))],
        out_specs=out_specs,
        out_shape=out_shape,
        scratch_shapes=[pltpu.VMEM((ATTN_GROUP_IN // LANES, tm, LANES), F32)],
        compiler_params=_cparams(("parallel",)),
        name="inproj",
    )(x2, g, w, cos_t, sin_t)


def _prep_kernel(z_ref, zp_ref, zn_ref, mu_ref, w0_ref, wuh_ref, wul_ref, a0_ref, auh_ref, aul_ref,
                 guh_ref, gul_ref, kk_ref, ka_ref, rk_ref,
                 r_o, kk_o, v_o, g_o, bonus_o, lw0_o, b0_o, kd0_o, lw1_o, b1_o, kd1_o, *, nts):
    it = pl.program_id(0) % nts
    z = z_ref[...]
    prev_row = jnp.where(it == 0, 0.0, zp_ref[HALO - 1:HALO, :])
    next_row = jnp.where(it == nts - 1, 0.0, zn_ref[0:1, :])
    zp, zn = _shift_rows(z, prev_row, next_row)
    zz = z + mu_ref[...] * (0.5 * (zp + zn) - z)
    W = RWKV_W
    r = zz[:, 0:W]
    k = zz[:, W:2 * W]
    v = zz[:, 2 * W:3 * W]
    o = 3 * W
    wd = jnp.tanh(zz[:, o:o + 2 * DECAY_LORA])
    o += 2 * DECAY_LORA
    ad = zz[:, o:o + 2 * ICLR_LORA]
    o += 2 * ICLR_LORA
    gd = _sigmoid(zz[:, o:])
    wlog = w0_ref[...] + _dot3(_split(wd), (wuh_ref[...], wul_ref[...]))
    aa = _sigmoid(a0_ref[...] + _dot3(_split(ad), (auh_ref[...], aul_ref[...])))
    g_o[...] = _dot3(_split(gd), (guh_ref[...], gul_ref[...]))
    ones = _head_ones()
    kkv = k * kk_ref[...]
    kkn = kkv / jnp.sqrt(_head_sum(kkv * kkv, ones) + 1e-12)
    r_o[...] = r
    kk_o[...] = kkn
    v_o[...] = v
    ka = ka_ref[...]
    kd_sum = None
    for d, (lw_o, b_o, kd_o) in enumerate(((lw0_o, b0_o, kd0_o), (lw1_o, b1_o, kd1_o))):
        a_d = aa[:, d * W:(d + 1) * W]
        lw_o[...] = -math.exp(-0.5) * _sigmoid(wlog[:, d * W:(d + 1) * W])
        kd = k * (1.0 + (a_d - 1.0) * ka)
        kd_o[...] = kd
        b_o[...] = kkn * a_d
        kd_sum = kd if kd_sum is None else kd_sum + kd
    bonus_o[...] = _head_sum(r * kd_sum * rk_ref[...], ones) * v


def _prep(zr, prm, j, B, S, tm):
    T = B * S
    nts = S // tm
    prev, nxt = _halo_specs(tm, RWKV_IN, T)
    W = RWKV_W
    vec = lambda a: a.reshape(1, -1)

    def bdiag(w):
        L = w.shape[1]
        m = jnp.zeros((2 * L, 2 * W), F32).at[:L, :W].set(w[0]).at[L:, W:].set(w[1])
        return _split(m)

    wuh, wul = bdiag(prm['rwkv_w_up'][j])
    auh, aul = bdiag(prm['rwkv_a_up'][j])
    guh, gul = _split(prm['rwkv_g_up'][j])
    params = [vec(prm['rwkv_mu'][j]), vec(prm['rwkv_w0'][j]), wuh, wul, vec(prm['rwkv_a0'][j]), auh, aul,
              guh, gul, vec(prm['rwkv_k_k'][j]), vec(prm['rwkv_k_a'][j]), vec(prm['rwkv_r_k'][j])]
    row = pl.BlockSpec((tm, W), lambda i: (i, 0))
    return pl.pallas_call(
        functools.partial(_prep_kernel, nts=nts),
        grid=(T // tm,),
        in_specs=[pl.BlockSpec((tm, RWKV_IN), lambda i: (i, 0)), prev, nxt] + [_full(a.shape) for a in params],
        out_specs=[row] * 11,
        out_shape=[jax.ShapeDtypeStruct((T, W), F32)] * 11,
        compiler_params=_cparams(("parallel",)),
        name="rwkv_prep",
    )(zr, zr, zr, *params)


def _stack2(x):
    lane = lax.broadcasted_iota(jnp.int32, x.shape, 1)
    m0 = lane < HEAD_DIM
    zero = jnp.zeros_like(x)
    return jnp.concatenate([jnp.where(m0, x, zero), jnp.where(m0, zero, x)], axis=0)


def _pm(m, x):
    return _dot(m.astype(BF16), _stack2(x.astype(BF16)))


def _wkv_scaled(r, kk, v, lw, b, kd, fwd):
    C = CHUNK
    ti = lax.broadcasted_iota(jnp.int32, (C, C), 0)
    tj = lax.broadcasted_iota(jnp.int32, (C, C), 1)
    cum = jnp.where((tj <= ti) if fwd else (tj >= ti), 1.0, 0.0).astype(BF16)
    lh, ll = _split(lw)
    G = _dot(cum, lh) + _dot(cum, ll)
    g_tot = G[C - 1:C, :] if fwd else G[0:1, :]
    e_neg = jnp.exp(-G)
    rt = r * jnp.exp(G)
    at = -kk * jnp.exp(G - lw)
    kt = kd * e_neg
    bt = b * e_neg
    e_tot = jnp.exp(g_tot)
    kb = jnp.concatenate([kt * e_tot, bt * e_tot], axis=0)
    return dict(at=at, rt=rt, kt=kt, bt=bt, kb=kb, v=v, e_tot=e_tot, fwd=fwd)


def _wkv_pairs(chains):
    C = CHUNK
    trow = lax.broadcasted_iota(jnp.int32, (C, 2 * C), 0)
    tcol = lax.broadcasted_iota(jnp.int32, (C, 2 * C), 1) & (C - 1)
    eye = jnp.where(tcol == trow, 1.0, 0.0)
    hr = lax.broadcasted_iota(jnp.int32, (LANES, LANES), 0) // HEAD_DIM
    hc = lax.broadcasted_iota(jnp.int32, (LANES, LANES), 1) // HEAD_DIM
    for ch in chains:
        ch['x'] = jnp.concatenate([ch['at'], ch['rt']], axis=0).astype(BF16)
    for ch in chains:
        kb2 = jnp.concatenate([_stack2(ch['kt'].astype(BF16)), _stack2(ch['bt'].astype(BF16))], axis=0)
        sc = _dot_nt(ch['x'], kb2)
        ch['sc_k'] = sc[:, :2 * C]
        ch['sc_b'] = sc[:, 2 * C:]
        ch['p1'] = _dot_nt(ch['x'], ch['S'].astype(BF16))
    for ch in chains:
        strict = (tcol < trow) if ch['fwd'] else (tcol > trow)
        incl = (tcol <= trow) if ch['fwd'] else (tcol >= trow)
        ch['a_kk'] = jnp.concatenate([jnp.where(strict, ch['sc_k'][:C], 0.0),
                                      jnp.where(incl, ch['sc_k'][C:], 0.0)], axis=0)
        ch['n'] = jnp.where(strict, ch['sc_b'][:C], 0.0)
        ch['ta'] = jnp.concatenate([eye, jnp.where(incl, ch['sc_b'][C:], 0.0)], axis=0)
    for j in range(6):
        for ch in chains:
            n = ch['n']
            if j < 5:
                m = _pm(jnp.concatenate([ch['ta'], n], axis=0), n)
                ch['n'] = m[2 * C:]
                ch['ta'] = ch['ta'] + m[:2 * C]
            else:
                ch['ta'] = ch['ta'] + _pm(ch['ta'], n)
            if j == 0:
                ch['p2'] = _pm(ch['a_kk'], ch['v'])
    for ch in chains:
        ch['uy'] = _pm(ch['ta'], ch['p1'][:C] + ch['p2'][:C])
    for ch in chains:
        ch['y'] = ch['p1'][C:] + ch['p2'][C:] + ch['uy'][C:]
        vu_t = jnp.concatenate([ch['v'], ch['uy'][:C]], axis=0).T
        upd = _dot(vu_t.astype(BF16), ch['kb'].astype(BF16))
        ch['s_new'] = ch['S'] * ch['e_tot'] + jnp.where(hr == hc, upd, 0.0)


def _scan_kernel(rf, kkf, vf, lwf, bf, kdf, rb, kkb, vb, lwb, bb, kdb, yf_o, yb_o, sf_ref, sb_ref, *, nc, nb):
    @pl.when(pl.program_id(1) == 0)
    def _():
        sf_ref[...] = jnp.zeros_like(sf_ref)
        sb_ref[...] = jnp.zeros_like(sb_ref)

    npairs = RWKV_W // LANES

    def body(c, carry):
        slf = pl.ds(pl.multiple_of(c * CHUNK, CHUNK), CHUNK)
        slb = pl.ds(pl.multiple_of((nc - 1 - c) * CHUNK, CHUNK), CHUNK)
        chains = []
        for bi in range(nb):
            dirs = (_wkv_scaled(*[ref[bi, slf, :] for ref in (rf, kkf, vf, lwf, bf, kdf)], True),
                    _wkv_scaled(*[ref[bi, slb, :] for ref in (rb, kkb, vb, lwb, bb, kdb)], False))
            for d, s_ref in zip(dirs, (sf_ref, sb_ref)):
                for p in range(npairs):
                    sl = slice(p * LANES, (p + 1) * LANES)
                    ch = {k: (val if k == 'fwd' else val[:, sl]) for k, val in d.items()}
                    ch['S'] = s_ref[bi * npairs + p]
                    chains.append(ch)
        _wkv_pairs(chains)
        chains = iter(chains)
        for bi in range(nb):
            for s_ref, y_o, sl in ((sf_ref, yf_o, slf), (sb_ref, yb_o, slb)):
                ys = []
                for p in range(npairs):
                    ch = next(chains)
                    s_ref[bi * npairs + p] = ch['s_new']
                    ys.append(ch['y'])
                y_o[bi, sl, :] = jnp.concatenate(ys, axis=1)
        return carry

    lax.fori_loop(0, nc, body, 0)


def _scan(r, kk, v, lw0, b0, kd0, lw1, b1, kd1, B, S, tb, nb):
    nblk = S // tb
    shp = (B, S, RWKV_W)
    args = [a.reshape(shp) for a in (r, kk, v, lw0, b0, kd0, r, kk, v, lw1, b1, kd1)]
    fwd = pl.BlockSpec((nb, tb, RWKV_W), lambda bi, j: (bi, j, 0))
    bwd = pl.BlockSpec((nb, tb, RWKV_W), lambda bi, j: (bi, nblk - 1 - j, 0))
    yf, yb = pl.pallas_call(
        functools.partial(_scan_kernel, nc=tb // CHUNK, nb=nb),
        grid=(B // nb, nblk),
        in_specs=[fwd] * 6 + [bwd] * 6,
        out_specs=[fwd, bwd],
        out_shape=[jax.ShapeDtypeStruct(shp, F32)] * 2,
        scratch_shapes=[pltpu.VMEM((nb * RWKV_W // LANES, LANES, LANES), F32)] * 2,
        compiler_params=_cparams(("parallel", "arbitrary")),
        name="wkv_scan",
    )(*args)
    return yf.reshape(B * S, RWKV_W), yb.reshape(B * S, RWKV_W)


def _attn_kernel(q_ref, km_ref, vm_ref, kp_ref, vp_ref, kn_ref, vn_ref, o_ref, lse_ref, *, Q, L):
    j = pl.program_id(2)
    q = q_ref[...] * (HEAD_DIM ** -0.5)
    kw = jnp.concatenate([kp_ref[...], km_ref[...], kn_ref[...]], axis=0).astype(BF16)
    vw = jnp.concatenate([vp_ref[...], vm_ref[...], vn_ref[...]], axis=0).astype(BF16)
    Wn = Q + 2 * ATTN_HALF
    qi = lax.broadcasted_iota(jnp.int32, (Q, Wn), 0)
    wc = lax.broadcasted_iota(jnp.int32, (Q, Wn), 1)
    rel = wc - ATTN_HALF - qi
    kpos = j * Q - ATTN_HALF + wc
    valid = jnp.minimum(jnp.minimum(ATTN_HALF - jnp.abs(rel), kpos), L - 1 - kpos) >= 0
    lane = lax.broadcasted_iota(jnp.int32, (Q, ATTN_OUT), 1) // HEAD_DIM
    o_acc = jnp.zeros((Q, ATTN_OUT), F32)
    l_acc = jnp.zeros((Q, ATTN_OUT), F32)
    for hh in range(ATTN_HEADS_PER_GROUP):
        mh = lane == hh
        qh = jnp.where(mh, q, 0.0).astype(BF16)
        s = jnp.where(valid, _dot_nt(qh, kw), NEG_INF)
        m = jnp.max(s, axis=-1, keepdims=True)
        p = jnp.exp(s - m)
        den = jnp.sum(p, axis=-1, keepdims=True)
        oh = _dot(p.astype(BF16), vw) / den
        o_acc = jnp.where(mh, oh, o_acc)
        l_acc = jnp.where(mh, m + jnp.log(den), l_acc)
    o_ref[...] = o_acc
    lse_ref[...] = l_acc


def _attn(qkv, B, S, dil):
    L = S // dil
    Q = min(256, L)
    nq = L // Q
    qb = Q // ATTN_HALF
    nhb = L // ATTN_HALF

    def mid(col):
        return pl.BlockSpec((None, None, Q, ATTN_OUT), lambda bi, r, j: (bi, r, j, col))

    def prev(col):
        return pl.BlockSpec((None, None, ATTN_HALF, ATTN_OUT),
                            lambda bi, r, j: (bi, r, jnp.maximum(j * qb - 1, 0), col))

    def nxt(col):
        return pl.BlockSpec((None, None, ATTN_HALF, ATTN_OUT),
                            lambda bi, r, j: (bi, r, jnp.minimum((j + 1) * qb, nhb - 1), col))

    out = pl.BlockSpec((None, None, Q, ATTN_OUT), lambda bi, r, j: (bi, r, j, 0))
    return pl.pallas_call(
        functools.partial(_attn_kernel, Q=Q, L=L),
        grid=(B, dil, nq),
        in_specs=[mid(0), mid(1), mid(2), prev(1), prev(2), nxt(1), nxt(2)],
        out_specs=[out, out],
        out_shape=[jax.ShapeDtypeStruct((B, dil, L, ATTN_OUT), F32)] * 2,
        compiler_params=_cparams(("parallel", "parallel", "parallel")),
        name="band_attn_d%d" % dil,
    )(qkv, qkv, qkv, qkv, qkv, qkv, qkv)


def _post_kernel(x_ref, yf_ref, yb_ref, bonus_ref, g_ref, o0, l0, o1, l1, o2, l2, lng_ref, lnb_ref, wo_ref,
                 out_ref, oscr, lscr, *, tm):
    ones = _head_ones()
    y = yf_ref[...] + yb_ref[...]
    yc = y - _head_sum(y, ones) * (1.0 / HEAD_DIM)
    var = _head_sum(yc * yc, ones) * (1.0 / HEAD_DIM)
    yn = yc * lax.rsqrt(var + GN_EPS) * lng_ref[...] + lnb_ref[...]
    ya = (yn + bonus_ref[...]) * g_ref[...]
    nlb = ATTN_OUT // LANES
    os_, ls = [], []
    for gi, (o_r, l_r, (_, dil)) in enumerate(zip((o0, o1, o2), (l0, l1, l2), ATTN_PATTERNS)):
        if dil == 1:
            os_.append(o_r[0])
            ls.append(l_r[0])
            continue
        for c in range(nlb):
            for r in range(dil):
                oscr[gi * nlb + c, pl.ds(r, tm // dil, stride=dil), :] = o_r[r, :, c * LANES:(c + 1) * LANES]
                lscr[gi * nlb + c, pl.ds(r, tm // dil, stride=dil), :] = l_r[r, :, c * LANES:(c + 1) * LANES]
        os_.append(jnp.concatenate([oscr[gi * nlb + c] for c in range(nlb)], axis=1))
        ls.append(jnp.concatenate([lscr[gi * nlb + c] for c in range(nlb)], axis=1))
    m = jnp.maximum(jnp.maximum(ls[0], ls[1]), ls[2])
    es = [jnp.exp(l - m) for l in ls]
    num = es[0] * os_[0] + es[1] * os_[1] + es[2] * os_[2]
    yb = num / (es[0] + es[1] + es[2])
    out_ref[...] = (x_ref[...] + _dot(ya.astype(BF16), wo_ref[:RWKV_W, :])
                    + _dot(yb.astype(BF16), wo_ref[RWKV_W:, :]))


def _post(x2, yf, yb, bonus, g, attn, lng, lnb, wo, B, S, tm):
    T = B * S
    nts = S // tm
    row = lambda w: pl.BlockSpec((tm, w), lambda i: (i, 0))
    in_specs = [row(D_MODEL)] + [row(RWKV_W)] * 4
    args = [x2, yf, yb, bonus, g]
    for (o, l), (_, dil) in zip(attn, ATTN_PATTERNS):
        spec = pl.BlockSpec((None, dil, tm // dil, ATTN_OUT), lambda i: (i // nts, 0, i % nts, 0))
        in_specs += [spec, spec]
        args += [o, l]
    in_specs += [_full(lng.shape), _full(lnb.shape), _full(wo.shape)]
    args += [lng, lnb, wo]
    ng = len(ATTN_PATTERNS)
    return pl.pallas_call(
        functools.partial(_post_kernel, tm=tm),
        grid=(T // tm,),
        in_specs=in_specs,
        out_specs=row(D_MODEL),
        out_shape=jax.ShapeDtypeStruct((T, D_MODEL), F32),
        scratch_shapes=[pltpu.VMEM((ng * ATTN_OUT // LANES, tm, LANES), F32)] * 2,
        compiler_params=_cparams(("parallel",)),
        name="mixer_ab_out",
    )(*args)


def _pool_kernel(x_ref, xp_ref, xn_ref, g_ref, win_ref, wg_ref, sc_ref, wo_ref, out_ref, u_ref, *, tm, nts, S):
    it = pl.program_id(0) % nts
    x = x_ref[...]
    g = g_ref[...]
    hp = jnp.where(it == 0, 0.0, _rms(xp_ref[...], g))
    hn = jnp.where(it == nts - 1, 0.0, _rms(xn_ref[...], g))
    h = jnp.concatenate([hp, _rms(x, g), hn], axis=0).astype(BF16)
    u_ref[...] = _dot(h, win_ref[...])
    pos = it * tm + lax.broadcasted_iota(jnp.int32, (tm, 1), 0)
    acc = x
    for gi, win in enumerate(POOL_WINDOWS):
        rad = win // 2
        cols = slice(gi * POOL_GROUP, (gi + 1) * POOL_GROUP)
        ws = u_ref[pl.ds(HALO, tm), cols]
        u_c = ws
        for d in range(1, rad + 1):
            ws = ws + (u_ref[pl.ds(HALO - d, tm), cols] + u_ref[pl.ds(HALO + d, tm), cols])
        cnt = (jnp.minimum(pos + rad + 1, S) - jnp.maximum(pos - rad, 0)).astype(F32)
        dlt = (ws / cnt - u_c).astype(BF16)
        yg = _dot(dlt, wg_ref[gi]) * sc_ref[:, cols]
        acc = acc + _dot(yg.astype(BF16), wo_ref[cols, :])
    out_ref[...] = acc


def _pool(x2, g, w_in, w_group, scale, w_out, B, S, tm):
    T = B * S
    nts = S // tm
    prev, nxt = _halo_specs(tm, D_MODEL, T)
    row = pl.BlockSpec((tm, D_MODEL), lambda i: (i, 0))
    return pl.pallas_call(
        functools.partial(_pool_kernel, tm=tm, nts=nts, S=S),
        grid=(T // tm,),
        in_specs=[row, prev, nxt, _full(g.shape), _full(w_in.shape), _full(w_group.shape), _full(scale.shape),
                  _full(w_out.shape)],
        out_specs=row,
        out_shape=jax.ShapeDtypeStruct((T, D_MODEL), F32),
        scratch_shapes=[pltpu.VMEM((tm + 2 * HALO, POOL_W), F32)],
        compiler_params=_cparams(("parallel",)),
        name="pool_mixer",
    )(x2, x2, x2, g, w_in, w_group, scale, w_out)


def _gelu(x):
    return x * (0.5 * (1.0 + jnp.tanh(math.sqrt(2.0 / math.pi) * (x + 0.044715 * (x * x * x)))))


def _ffn_kernel(x_ref, xp_ref, xn_ref, p_ref, gf_ref, wup_ref, cw_ref, cb_ref, wd_ref,
                gp_ref, wpg_ref, bpg_ref, wpp_ref, gfin_ref, out_ref, h_ref, *, tm, rb, nts, final):
    it = pl.program_id(0) % nts
    g = gf_ref[...]
    h_ref[pl.ds(0, HALO), :] = jnp.where(it == 0, 0.0, _rms(xp_ref[...], g)).astype(BF16)
    h_ref[pl.ds(HALO, tm), :] = _rms(x_ref[...], g).astype(BF16)
    h_ref[pl.ds(HALO + tm, HALO), :] = jnp.where(it == nts - 1, 0.0, _rms(xn_ref[...], g)).astype(BF16)
    n_ext = rb + 2 * HALO
    for s in range(tm // rb):
        r0 = s * rb
        u = _dot(h_ref[pl.ds(r0, n_ext), :], wup_ref[...])
        mid = slice(HALO, HALO + rb)
        c = (pltpu.roll(u, 1, 0)[mid] * cw_ref[0:1, :] + u[mid] * cw_ref[1:2, :]
             + pltpu.roll(u, n_ext - 1, 0)[mid] * cw_ref[2:3, :] + cb_ref[...])
        act = _gelu(c[:, :D_FF]) * c[:, D_FF:]
        rows = pl.ds(r0, rb)
        x2 = x_ref[rows, :] + _dot(act.astype(BF16), wd_ref[...])
        h3 = _rms(x2, gp_ref[...]).astype(BF16)
        gate = _sigmoid(_dot(h3, wpg_ref[...]) + bpg_ref[...])
        x3 = x2 + gate * _dot(p_ref[rows, :].astype(BF16), wpp_ref[...])
        out_ref[rows, :] = _rms(x3, gfin_ref[...]) if final else x3


def _ffn(x2, p_all, layer, w, B, S, tm, rb, final):
    T = B * S
    nt = T // tm
    nts = S // tm
    prev, nxt = _halo_specs(tm, D_MODEL, T)
    row = pl.BlockSpec((tm, D_MODEL), lambda i: (i, 0))
    names = ('gf', 'wup', 'cw', 'cb', 'wd', 'gp', 'wpg', 'bpg', 'wpp', 'gfin')
    in_specs = [row, prev, nxt, pl.BlockSpec((tm, PLE_DIM), lambda i: (layer * nt + i, 0))]
    in_specs += [_resident(w[k].shape) for k in names]
    return pl.pallas_call(
        functools.partial(_ffn_kernel, tm=tm, rb=rb, nts=nts, final=final),
        grid=(nt,),
        in_specs=in_specs,
        out_specs=row,
        out_shape=jax.ShapeDtypeStruct((T, D_MODEL), F32),
        scratch_shapes=[pltpu.VMEM((tm + 2 * HALO, D_MODEL), BF16)],
        compiler_params=_cparams(("parallel",)),
        name="convffn_ple",
    )(x2, x2, x2, p_all, *[w[k] for k in names])


def _rope_tables(S):
    half = ROT_DIM // 2
    inv = jnp.float32(ROPE_THETA) ** (-jnp.arange(half, dtype=F32) * 2.0 / ROT_DIM)
    ang = jnp.arange(S, dtype=F32)[:, None] * inv[None, :]
    cos = jnp.cos(ang)
    sin = jnp.sin(ang)
    pad1 = jnp.ones((S, HEAD_DIM - ROT_DIM), F32)
    pad0 = jnp.zeros((S, HEAD_DIM - ROT_DIM), F32)
    cos_h = jnp.concatenate([cos, cos, pad1], axis=1)
    sin_h = jnp.concatenate([-sin, sin, pad0], axis=1)
    reps = LANES // HEAD_DIM
    return jnp.tile(cos_h, (1, reps)), jnp.tile(sin_h, (1, reps))


def _pack_ab_w_in(w):
    parts = [w[:, :RWKV_IN]]
    qkv = [w[:, RWKV_IN + s * ATTN_HEADS * HEAD_DIM: RWKV_IN + (s + 1) * ATTN_HEADS * HEAD_DIM] for s in range(3)]
    for g in range(len(ATTN_PATTERNS)):
        parts += [t[:, g * ATTN_OUT:(g + 1) * ATTN_OUT] for t in qkv]
    return jnp.concatenate(parts, axis=1).astype(BF16)


def _prepare(prm):
    vec = lambda a: a.reshape(1, -1).astype(F32)
    pk = {'ab_w_in': [_pack_ab_w_in(prm['ab_w_in'][j]) for j in range(prm['ab_w_in'].shape[0])],
          'ab_w_out': prm['ab_w_out'].astype(BF16),
          'c_w_in': prm['c_w_in'].astype(BF16), 'c_w_group': prm['c_w_group'].astype(BF16),
          'c_w_out': prm['c_w_out'].astype(BF16), 'ffn': []}
    for i in range(DEPTH):
        pk['ffn'].append({
            'gf': vec(prm['norm_ffn_g'][i]), 'wup': prm['ffn_w_up'][i].astype(BF16),
            'cw': prm['ffn_conv_w'][i], 'cb': vec(prm['ffn_conv_b'][i]),
            'wd': prm['ffn_w_down'][i].astype(BF16), 'gp': vec(prm['norm_ple_g'][i]),
            'wpg': prm['ple_w_gate'][i].astype(BF16), 'bpg': vec(prm['ple_b_gate'][i]),
            'wpp': prm['ple_w_proj'][i].astype(BF16), 'gfin': vec(prm['norm_final_g'])})
    return pk


def _trunk(x, p, prm, pk, tiles):
    B, S, _ = x.shape
    T = B * S
    vec = lambda a: a.reshape(1, -1)
    x2 = x.reshape(T, D_MODEL)
    p_all = p.reshape(p.shape[0] * T, PLE_DIM)
    cos_t, sin_t = _rope_tables(S)
    for i in range(DEPTH):
        j = i // 2
        gmix = vec(prm['norm_mix_g'][i])
        if i % 2 == 0:
            zr, q0, q1, q2 = _inproj(x2, gmix, pk['ab_w_in'][j], cos_t, sin_t, B, S, tiles['inproj'])
            r, kk, v, g, bonus, lw0, b0, kd0, lw1, b1, kd1 = _prep(zr, prm, j, B, S, tiles['prep'])
            yf, yb = _scan(r, kk, v, lw0, b0, kd0, lw1, b1, kd1, B, S, tiles['scan'], tiles['scan_b'])
            attn = [_attn(q, B, S, dil) for q, (_, dil) in zip((q0, q1, q2), ATTN_PATTERNS)]
            x2 = _post(x2, yf, yb, bonus, g, attn, vec(prm['rwkv_ln_g'][j]), vec(prm['rwkv_ln_b'][j]),
                       pk['ab_w_out'][j], B, S, tiles['post'])
        else:
            x2 = _pool(x2, gmix, pk['c_w_in'][j], pk['c_w_group'][j], vec(prm['c_scale'][j]), pk['c_w_out'][j],
                       B, S, tiles['pool'])
        x2 = _ffn(x2, p_all, i, pk['ffn'][i], B, S, tiles['ffn_m'], tiles['ffn_f'], final=(i == DEPTH - 1))
    return x2.reshape(B, S, D_MODEL)


_TILES = {'inproj': 512, 'prep': 256, 'scan': 256, 'scan_b': 2, 'post': 512, 'pool': 512, 'ffn_m': 512, 'ffn_f': 256}


def kernel(x_prompt, x_sample, p_prompt, p_sample, ab_w_in, ab_w_out, rwkv_mu, rwkv_w0, rwkv_w_up, rwkv_a0, rwkv_a_up, rwkv_g_up, rwkv_k_k, rwkv_k_a, rwkv_r_k, rwkv_ln_g, rwkv_ln_b, c_w_in, c_w_group, c_scale, c_w_out, norm_mix_g, norm_ffn_g, norm_ple_g, norm_final_g, ffn_w_up, ffn_conv_w, ffn_conv_b, ffn_w_down, ple_w_proj, ple_w_gate, ple_b_gate):
    prm = {
        'ab_w_in': ab_w_in, 'ab_w_out': ab_w_out, 'rwkv_mu': rwkv_mu, 'rwkv_w0': rwkv_w0,
        'rwkv_w_up': rwkv_w_up, 'rwkv_a0': rwkv_a0, 'rwkv_a_up': rwkv_a_up, 'rwkv_g_up': rwkv_g_up,
        'rwkv_k_k': rwkv_k_k, 'rwkv_k_a': rwkv_k_a, 'rwkv_r_k': rwkv_r_k, 'rwkv_ln_g': rwkv_ln_g,
        'rwkv_ln_b': rwkv_ln_b, 'c_w_in': c_w_in, 'c_w_group': c_w_group, 'c_scale': c_scale,
        'c_w_out': c_w_out, 'norm_mix_g': norm_mix_g, 'norm_ffn_g': norm_ffn_g, 'norm_ple_g': norm_ple_g,
        'norm_final_g': norm_final_g, 'ffn_w_up': ffn_w_up, 'ffn_conv_w': ffn_conv_w,
        'ffn_conv_b': ffn_conv_b, 'ffn_w_down': ffn_w_down, 'ple_w_proj': ple_w_proj,
        'ple_w_gate': ple_w_gate, 'ple_b_gate': ple_b_gate,
    }
    pk = _prepare(prm)
    y_prompt = _trunk(x_prompt, p_prompt, prm, pk, _TILES)
    y_sample = _trunk(x_sample, p_sample, prm, pk, _TILES)
    return (y_prompt, y_sample)
```

```python
import functools
import math

import jax
import jax.numpy as jnp
from jax import lax
from jax.experimental import pallas as pl
from jax.experimental.pallas import tpu as pltpu

D_MODEL = 1024
DEPTH = 4
PLE_DIM = 256
HEAD_DIM = 64
RMS_EPS = 1e-6
RWKV_HEADS = 8
RWKV_W = RWKV_HEADS * HEAD_DIM
DECAY_LORA = 64
ICLR_LORA = 64
GATE_LORA = 128
RWKV_IN = 3 * RWKV_W + 2 * DECAY_LORA + 2 * ICLR_LORA + GATE_LORA
GN_EPS = 64e-5
ATTN_PATTERNS = ((128, 1), (512, 4), (2048, 16))
ATTN_HEADS_PER_GROUP = 4
ATTN_HEADS = ATTN_HEADS_PER_GROUP * len(ATTN_PATTERNS)
ATTN_OUT = ATTN_HEADS_PER_GROUP * HEAD_DIM
ATTN_GROUP_IN = 3 * ATTN_OUT
ATTN_HALF = 64
ROPE_THETA = 500000.0
ROT_DIM = HEAD_DIM // 4
NEG_INF = -1e30
AB_OUT = RWKV_W + ATTN_OUT
POOL_WINDOWS = (2, 4, 8, 16)
POOL_GROUP = 256
POOL_W = len(POOL_WINDOWS) * POOL_GROUP
D_FF = 2816

LANES = 128
SUBLANES = 8
HALO = SUBLANES
CHUNK = 64
SCAN_GROUP_LANES = LANES
VMEM_LIMIT = 56 * 1024 * 1024

F32 = jnp.float32
BF16 = jnp.bfloat16


def _dot(a, b):
    return jnp.dot(a, b, preferred_element_type=F32)


def _dot_nt(a, b):
    return lax.dot_general(a, b, (((1,), (1,)), ((), ())), preferred_element_type=F32)


def _split(x):
    hi = x.astype(BF16)
    lo = (x - hi.astype(F32)).astype(BF16)
    return hi, lo


def _dot3(ap, bp):
    return _dot(ap[0], bp[0]) + (_dot(ap[0], bp[1]) + _dot(ap[1], bp[0]))


def _dot3_nt(ap, bp):
    return _dot_nt(ap[0], bp[0]) + (_dot_nt(ap[0], bp[1]) + _dot_nt(ap[1], bp[0]))


def _rms(x, g):
    return x * lax.rsqrt(jnp.mean(x * x, axis=-1, keepdims=True) + RMS_EPS) * g


def _sigmoid(x):
    return 1.0 / (1.0 + jnp.exp(-x))


def _head_ones():
    r = lax.broadcasted_iota(jnp.int32, (LANES, LANES), 0) // HEAD_DIM
    c = lax.broadcasted_iota(jnp.int32, (LANES, LANES), 1) // HEAD_DIM
    return jnp.where(r == c, 1.0, 0.0).astype(BF16)


def _head_sum(x, ones):
    outs = []
    for j in range(x.shape[1] // LANES):
        hi, lo = _split(x[:, j * LANES:(j + 1) * LANES])
        outs.append(_dot(hi, ones) + _dot(lo, ones))
    return jnp.concatenate(outs, axis=1)


def _shift_rows(z, prev_row, next_row):
    n = z.shape[0]
    row = lax.broadcasted_iota(jnp.int32, z.shape, 0)
    zp = jnp.where(row == 0, prev_row, pltpu.roll(z, 1, 0))
    zn = jnp.where(row == n - 1, next_row, pltpu.roll(z, n - 1, 0))
    return zp, zn


def _cparams(sem):
    return pltpu.CompilerParams(dimension_semantics=sem, vmem_limit_bytes=VMEM_LIMIT)


def _halo_specs(tm, width, n_rows):
    nb = n_rows // HALO
    step = tm // HALO
    prev = pl.BlockSpec((HALO, width), lambda i: (jnp.maximum(i * step - 1, 0), 0))
    nxt = pl.BlockSpec((HALO, width), lambda i: (jnp.minimum((i + 1) * step, nb - 1), 0))
    return prev, nxt


def _full(shape):
    nd = len(shape)
    return pl.BlockSpec(shape, lambda *_: (0,) * nd)


def _resident(shape):
    nd = len(shape)
    return pl.BlockSpec(shape, lambda *_: (0,) * nd, pipeline_mode=pl.Buffered(1))


def _inproj_kernel(x_ref, xp_ref, xn_ref, g_ref, w_ref, cos_ref, sin_ref,
                   mu_ref, w0_ref, wuh_ref, wul_ref, a0_ref, auh_ref, aul_ref, guh_ref, gul_ref, kk_ref, ka_ref, rk_ref,
                   r_o, kk_o, v_o, g_o, bonus_o, lw0_o, b0_o, kd0_o, lw1_o, b1_o, kd1_o, q0_ref, q1_ref, q2_ref,
                   scr_ref, *, tm, nts):
    it = pl.program_id(0) % nts
    gm = g_ref[...]
    hm = _rms(x_ref[...], gm)
    hp = jnp.where(it == 0, 0.0, _rms(xp_ref[...], gm))
    hn = jnp.where(it == nts - 1, 0.0, _rms(xn_ref[...], gm))
    h = hm.astype(BF16)
    z_ext = _dot(jnp.concatenate([hp, hm, hn], axis=0).astype(BF16), w_ref[:, :RWKV_IN])
    mid = slice(HALO, HALO + tm)
    z = z_ext[mid]
    zs = 0.5 * (pltpu.roll(z_ext, 1, 0)[mid] + pltpu.roll(z_ext, tm + 2 * HALO - 1, 0)[mid])
    zz = z + mu_ref[...] * (zs - z)
    W = RWKV_W
    r = zz[:, 0:W]
    k = zz[:, W:2 * W]
    v = zz[:, 2 * W:3 * W]
    o = 3 * W
    wd = jnp.tanh(zz[:, o:o + 2 * DECAY_LORA])
    o += 2 * DECAY_LORA
    ad = zz[:, o:o + 2 * ICLR_LORA]
    o += 2 * ICLR_LORA
    gd = _sigmoid(zz[:, o:])
    wlog = w0_ref[...] + _dot3(_split(wd), (wuh_ref[...], wul_ref[...]))
    aa = _sigmoid(a0_ref[...] + _dot3(_split(ad), (auh_ref[...], aul_ref[...])))
    g_o[...] = _dot3(_split(gd), (guh_ref[...], gul_ref[...]))
    ones = _head_ones()
    kkv = k * kk_ref[...]
    kkn = kkv / jnp.sqrt(_head_sum(kkv * kkv, ones) + 1e-12)
    r_o[...] = r
    kk_o[...] = kkn
    v_o[...] = v
    ka = ka_ref[...]
    kd_sum = None
    for d, (lw_o, b_o, kd_o) in enumerate(((lw0_o, b0_o, kd0_o), (lw1_o, b1_o, kd1_o))):
        a_d = aa[:, d * W:(d + 1) * W]
        lw_o[...] = -math.exp(-0.5) * _sigmoid(wlog[:, d * W:(d + 1) * W])
        kd = k * (1.0 + (a_d - 1.0) * ka)
        kd_o[...] = kd
        b_o[...] = kkn * a_d
        kd_sum = kd if kd_sum is None else kd_sum + kd
    bonus_o[...] = _head_sum(r * kd_sum * rk_ref[...], ones) * v
    cs = cos_ref[...]
    sn = sin_ref[...]
    lane = lax.broadcasted_iota(jnp.int32, (tm, LANES), 1)
    first = (lane & (HEAD_DIM - 1)) < (ROT_DIM // 2)
    outs = (q0_ref, q1_ref, q2_ref)
    for g, (_, dil) in enumerate(ATTN_PATTERNS):
        base = RWKV_IN + g * ATTN_GROUP_IN
        t = _dot(h, w_ref[:, base:base + ATTN_GROUP_IN])
        for c in range(ATTN_GROUP_IN // LANES):
            tc = t[:, c * LANES:(c + 1) * LANES]
            if c < 2 * ATTN_OUT // LANES:
                rot = jnp.where(first, pltpu.roll(tc, LANES - ROT_DIM // 2, 1), pltpu.roll(tc, ROT_DIM // 2, 1))
                tc = tc * cs + rot * sn
            if dil == 1:
                outs[g][0, :, c * LANES:(c + 1) * LANES] = tc
            else:
                scr_ref[c] = tc
        if dil > 1:
            for c in range(ATTN_GROUP_IN // LANES):
                for r in range(dil):
                    outs[g][r, :, c * LANES:(c + 1) * LANES] = scr_ref[c, pl.ds(r, tm // dil, stride=dil), :]


def _inproj(x2, g, w, cos_t, sin_t, prm, j, B, S, tm):
    T = B * S
    nts = S // tm
    W = RWKV_W
    vec = lambda a: a.reshape(1, -1)

    def bdiag(m):
        L = m.shape[1]
        return _split(jnp.zeros((2 * L, 2 * W), F32).at[:L, :W].set(m[0]).at[L:, W:].set(m[1]))

    wuh, wul = bdiag(prm['rwkv_w_up'][j])
    auh, aul = bdiag(prm['rwkv_a_up'][j])
    guh, gul = _split(prm['rwkv_g_up'][j])
    params = [vec(prm['rwkv_mu'][j]), vec(prm['rwkv_w0'][j]), wuh, wul, vec(prm['rwkv_a0'][j]), auh, aul,
              guh, gul, vec(prm['rwkv_k_k'][j]), vec(prm['rwkv_k_a'][j]), vec(prm['rwkv_r_k'][j])]
    prev, nxt = _halo_specs(tm, D_MODEL, T)
    row = pl.BlockSpec((tm, W), lambda i: (i, 0))
    out_shape = [jax.ShapeDtypeStruct((T, W), F32)] * 11
    out_specs = [row] * 11
    for _, dil in ATTN_PATTERNS:
        out_shape.append(jax.ShapeDtypeStruct((B, dil, S // dil, ATTN_GROUP_IN), F32))
        out_specs.append(pl.BlockSpec((None, dil, tm // dil, ATTN_GROUP_IN),
                                      lambda i: (i // nts, 0, i % nts, 0)))
    return pl.pallas_call(
        functools.partial(_inproj_kernel, tm=tm, nts=nts),
        grid=(T // tm,),
        in_specs=[pl.BlockSpec((tm, D_MODEL), lambda i: (i, 0)), prev, nxt,
                  _resident((1, D_MODEL)),
                  _resident(w.shape),
                  pl.BlockSpec((tm, LANES), lambda i: (i % nts, 0)),
                  pl.BlockSpec((tm, LANES), lambda i: (i % nts, 0))] + [_resident(a.shape) for a in params],
        out_specs=out_specs,
        out_shape=out_shape,
        scratch_shapes=[pltpu.VMEM((ATTN_GROUP_IN // LANES, tm, LANES), F32)],
        compiler_params=_cparams(("parallel",)),
        name="inproj",
    )(x2, x2, x2, g, w, cos_t, sin_t, *params)


def _stack2(x):
    head = lax.broadcasted_iota(jnp.int32, x.shape, 1) // HEAD_DIM
    zero = jnp.zeros_like(x)
    return jnp.concatenate([jnp.where(head == h, x, zero) for h in range(x.shape[1] // HEAD_DIM)], axis=0)


def _pm(m, x):
    return _dot(m.astype(BF16), _stack2(x.astype(BF16)))


def _wkv_scaled(r, kk, v, lw, b, kd, fwd):
    C = CHUNK
    ti = lax.broadcasted_iota(jnp.int32, (C, C), 0)
    tj = lax.broadcasted_iota(jnp.int32, (C, C), 1)
    cum = jnp.where((tj <= ti) if fwd else (tj >= ti), 1.0, 0.0).astype(BF16)
    lh, ll = _split(lw)
    G = _dot(cum, lh) + _dot(cum, ll)
    g_tot = G[C - 1:C, :] if fwd else G[0:1, :]
    e_neg = jnp.exp(-G)
    rt = r * jnp.exp(G)
    at = -kk * jnp.exp(G - lw)
    kt = kd * e_neg
    bt = b * e_neg
    e_tot = jnp.exp(g_tot)
    kb = jnp.concatenate([kt * e_tot, bt * e_tot], axis=0)
    return dict(at=at, rt=rt, kt=kt, bt=bt, kb=kb, v=v, e_tot=e_tot, fwd=fwd)


def _wkv_pairs(chains):
    C = CHUNK
    hw = chains[0]['at'].shape[1]
    gw = hw // HEAD_DIM * C
    trow = lax.broadcasted_iota(jnp.int32, (C, gw), 0)
    tcol = lax.broadcasted_iota(jnp.int32, (C, gw), 1) & (C - 1)
    eye = jnp.where(tcol == trow, 1.0, 0.0)
    hr = lax.broadcasted_iota(jnp.int32, (hw, hw), 0) // HEAD_DIM
    hc = lax.broadcasted_iota(jnp.int32, (hw, hw), 1) // HEAD_DIM
    for ch in chains:
        ch['x'] = jnp.concatenate([ch['at'], ch['rt']], axis=0).astype(BF16)
    for ch in chains:
        kb2 = jnp.concatenate([_stack2(ch['kt'].astype(BF16)), _stack2(ch['bt'].astype(BF16))], axis=0)
        sc = _dot_nt(ch['x'], kb2)
        ch['sc_k'] = sc[:, :gw]
        ch['sc_b'] = sc[:, gw:]
        ch['p1'] = _dot_nt(ch['x'], ch['S'].astype(BF16))
    for ch in chains:
        strict = (tcol < trow) if ch['fwd'] else (tcol > trow)
        incl = (tcol <= trow) if ch['fwd'] else (tcol >= trow)
        ch['a_kk'] = jnp.concatenate([jnp.where(strict, ch['sc_k'][:C], 0.0),
                                      jnp.where(incl, ch['sc_k'][C:], 0.0)], axis=0)
        ch['n'] = jnp.where(strict, ch['sc_b'][:C], 0.0)
        ch['ta'] = jnp.concatenate([eye, jnp.where(incl, ch['sc_b'][C:], 0.0)], axis=0)
    for j in range(6):
        for ch in chains:
            n = ch['n']
            if j < 5:
                m = _pm(jnp.concatenate([ch['ta'], n], axis=0), n)
                ch['n'] = m[2 * C:]
                ch['ta'] = ch['ta'] + m[:2 * C]
            else:
                ch['ta'] = ch['ta'] + _pm(ch['ta'], n)
            if j == 0:
                ch['p2'] = _pm(ch['a_kk'], ch['v'])
    for ch in chains:
        ch['uy'] = _pm(ch['ta'], ch['p1'][:C] + ch['p2'][:C])
    for ch in chains:
        ch['y'] = ch['p1'][C:] + ch['p2'][C:] + ch['uy'][C:]
        vu_t = jnp.concatenate([ch['v'], ch['uy'][:C]], axis=0).T
        upd = _dot(vu_t.astype(BF16), ch['kb'].astype(BF16))
        ch['s_new'] = ch['S'] * ch['e_tot'] + jnp.where(hr == hc, upd, 0.0)


def _scan_kernel(rf, kkf, vf, lwf, bf, kdf, rb, kkb, vb, lwb, bb, kdb, yf_o, yb_o, sf_ref, sb_ref, *, nc, nb):
    @pl.when(pl.program_id(1) == 0)
    def _():
        sf_ref[...] = jnp.zeros_like(sf_ref)
        sb_ref[...] = jnp.zeros_like(sb_ref)

    npairs = RWKV_W // SCAN_GROUP_LANES

    def body(c, carry):
        slf = pl.ds(pl.multiple_of(c * CHUNK, CHUNK), CHUNK)
        slb = pl.ds(pl.multiple_of((nc - 1 - c) * CHUNK, CHUNK), CHUNK)
        chains = []
        for bi in range(nb):
            dirs = (_wkv_scaled(*[ref[bi, slf, :] for ref in (rf, kkf, vf, lwf, bf, kdf)], True),
                    _wkv_scaled(*[ref[bi, slb, :] for ref in (rb, kkb, vb, lwb, bb, kdb)], False))
            for d, s_ref in zip(dirs, (sf_ref, sb_ref)):
                for p in range(npairs):
                    sl = slice(p * SCAN_GROUP_LANES, (p + 1) * SCAN_GROUP_LANES)
                    ch = {k: (val if k == 'fwd' else val[:, sl]) for k, val in d.items()}
                    ch['S'] = s_ref[bi * npairs + p]
                    chains.append(ch)
        _wkv_pairs(chains)
        chains = iter(chains)
        for bi in range(nb):
            for s_ref, y_o, sl in ((sf_ref, yf_o, slf), (sb_ref, yb_o, slb)):
                ys = []
                for p in range(npairs):
                    ch = next(chains)
                    s_ref[bi * npairs + p] = ch['s_new']
                    ys.append(ch['y'])
                y_o[bi, sl, :] = jnp.concatenate(ys, axis=1)
        return carry

    lax.fori_loop(0, nc, body, 0)


def _scan(r, kk, v, lw0, b0, kd0, lw1, b1, kd1, B, S, tb, nb):
    nblk = S // tb
    shp = (B, S, RWKV_W)
    args = [a.reshape(shp) for a in (r, kk, v, lw0, b0, kd0, r, kk, v, lw1, b1, kd1)]
    fwd = pl.BlockSpec((nb, tb, RWKV_W), lambda bi, j: (bi, j, 0))
    bwd = pl.BlockSpec((nb, tb, RWKV_W), lambda bi, j: (bi, nblk - 1 - j, 0))
    yf, yb = pl.pallas_call(
        functools.partial(_scan_kernel, nc=tb // CHUNK, nb=nb),
        grid=(B // nb, nblk),
        in_specs=[fwd] * 6 + [bwd] * 6,
        out_specs=[fwd, bwd],
        out_shape=[jax.ShapeDtypeStruct(shp, F32)] * 2,
        scratch_shapes=[pltpu.VMEM((nb * RWKV_W // SCAN_GROUP_LANES, SCAN_GROUP_LANES, SCAN_GROUP_LANES), F32)] * 2,
        compiler_params=_cparams(("parallel", "arbitrary")),
        name="wkv_scan",
    )(*args)
    return yf.reshape(B * S, RWKV_W), yb.reshape(B * S, RWKV_W)


def _attn_kernel(q_ref, km_ref, vm_ref, kp_ref, vp_ref, kn_ref, vn_ref, o_ref, lse_ref, *, Q, L, dil):
    j = pl.program_id(1)
    Wn = Q + 2 * ATTN_HALF
    qi = lax.broadcasted_iota(jnp.int32, (Q, Wn), 0)
    wc = lax.broadcasted_iota(jnp.int32, (Q, Wn), 1)
    rel = wc - ATTN_HALF - qi
    kpos = j * Q - ATTN_HALF + wc
    valid = jnp.minimum(jnp.minimum(ATTN_HALF - jnp.abs(rel), kpos), L - 1 - kpos) >= 0
    lane = lax.broadcasted_iota(jnp.int32, (Q, ATTN_OUT), 1) // HEAD_DIM

    def residue(r, carry):
        q = q_ref[r] * (HEAD_DIM ** -0.5)
        kw = jnp.concatenate([kp_ref[r], km_ref[r], kn_ref[r]], axis=0).astype(BF16)
        vw = jnp.concatenate([vp_ref[r], vm_ref[r], vn_ref[r]], axis=0).astype(BF16)
        o_acc = jnp.zeros((Q, ATTN_OUT), F32)
        l_acc = jnp.zeros((Q, ATTN_OUT), F32)
        for hh in range(ATTN_HEADS_PER_GROUP):
            mh = lane == hh
            qh = jnp.where(mh, q, 0.0).astype(BF16)
            s = jnp.where(valid, _dot_nt(qh, kw), NEG_INF)
            m = jnp.max(s, axis=-1, keepdims=True)
            p = jnp.exp(s - m)
            den = jnp.sum(p, axis=-1, keepdims=True)
            oh = _dot(p.astype(BF16), vw) / den
            o_acc = jnp.where(mh, oh, o_acc)
            l_acc = jnp.where(mh, m + jnp.log(den), l_acc)
        o_ref[r] = o_acc
        lse_ref[r] = l_acc
        return carry

    lax.fori_loop(0, dil, residue, 0)


def _attn(qkv, B, S, dil):
    L = S // dil
    Q = min(256, L, 2048 // dil)
    nq = L // Q
    qb = Q // ATTN_HALF
    nhb = L // ATTN_HALF

    def mid(col):
        return pl.BlockSpec((None, dil, Q, ATTN_OUT), lambda bi, j: (bi, 0, j, col))

    def prev(col):
        return pl.BlockSpec((None, dil, ATTN_HALF, ATTN_OUT),
                            lambda bi, j: (bi, 0, jnp.maximum(j * qb - 1, 0), col))

    def nxt(col):
        return pl.BlockSpec((None, dil, ATTN_HALF, ATTN_OUT),
                            lambda bi, j: (bi, 0, jnp.minimum((j + 1) * qb, nhb - 1), col))

    out = pl.BlockSpec((None, dil, Q, ATTN_OUT), lambda bi, j: (bi, 0, j, 0))
    return pl.pallas_call(
        functools.partial(_attn_kernel, Q=Q, L=L, dil=dil),
        grid=(B, nq),
        in_specs=[mid(0), mid(1), mid(2), prev(1), prev(2), nxt(1), nxt(2)],
        out_specs=[out, out],
        out_shape=[jax.ShapeDtypeStruct((B, dil, L, ATTN_OUT), F32)] * 2,
        compiler_params=_cparams(("parallel", "parallel")),
        name="band_attn_d%d" % dil,
    )(qkv, qkv, qkv, qkv, qkv, qkv, qkv)


def _post_kernel(x_ref, yf_ref, yb_ref, bonus_ref, g_ref, o0, l0, o1, l1, o2, l2, lng_ref, lnb_ref, wo_ref,
                 out_ref, oscr, lscr, *, tm):
    ones = _head_ones()
    y = yf_ref[...] + yb_ref[...]
    yc = y - _head_sum(y, ones) * (1.0 / HEAD_DIM)
    var = _head_sum(yc * yc, ones) * (1.0 / HEAD_DIM)
    yn = yc * lax.rsqrt(var + GN_EPS) * lng_ref[...] + lnb_ref[...]
    ya = (yn + bonus_ref[...]) * g_ref[...]
    nlb = ATTN_OUT // LANES
    os_, ls = [], []
    for gi, (o_r, l_r, (_, dil)) in enumerate(zip((o0, o1, o2), (l0, l1, l2), ATTN_PATTERNS)):
        if dil == 1:
            os_.append(o_r[0])
            ls.append(l_r[0])
            continue
        for c in range(nlb):
            for r in range(dil):
                oscr[gi * nlb + c, pl.ds(r, tm // dil, stride=dil), :] = o_r[r, :, c * LANES:(c + 1) * LANES]
                lscr[gi * nlb + c, pl.ds(r, tm // dil, stride=dil), :] = l_r[r, :, c * LANES:(c + 1) * LANES]
        os_.append(jnp.concatenate([oscr[gi * nlb + c] for c in range(nlb)], axis=1))
        ls.append(jnp.concatenate([lscr[gi * nlb + c] for c in range(nlb)], axis=1))
    m = jnp.maximum(jnp.maximum(ls[0], ls[1]), ls[2])
    es = [jnp.exp(l - m) for l in ls]
    num = es[0] * os_[0] + es[1] * os_[1] + es[2] * os_[2]
    yb = num / (es[0] + es[1] + es[2])
    out_ref[...] = (x_ref[...] + _dot(ya.astype(BF16), wo_ref[:RWKV_W, :])
                    + _dot(yb.astype(BF16), wo_ref[RWKV_W:, :]))


def _post(x2, yf, yb, bonus, g, attn, lng, lnb, wo, B, S, tm):
    T = B * S
    nts = S // tm
    row = lambda w: pl.BlockSpec((tm, w), lambda i: (i, 0))
    in_specs = [row(D_MODEL)] + [row(RWKV_W)] * 4
    args = [x2, yf, yb, bonus, g]
    for (o, l), (_, dil) in zip(attn, ATTN_PATTERNS):
        spec = pl.BlockSpec((None, dil, tm // dil, ATTN_OUT), lambda i: (i // nts, 0, i % nts, 0))
        in_specs += [spec, spec]
        args += [o, l]
    in_specs += [_full(lng.shape), _full(lnb.shape), _full(wo.shape)]
    args += [lng, lnb, wo]
    ng = len(ATTN_PATTERNS)
    return pl.pallas_call(
        functools.partial(_post_kernel, tm=tm),
        grid=(T // tm,),
        in_specs=in_specs,
        out_specs=row(D_MODEL),
        out_shape=jax.ShapeDtypeStruct((T, D_MODEL), F32),
        scratch_shapes=[pltpu.VMEM((ng * ATTN_OUT // LANES, tm, LANES), F32)] * 2,
        compiler_params=_cparams(("parallel",)),
        name="mixer_ab_out",
    )(*args)


def _pool_kernel(x_ref, xp_ref, xn_ref, g_ref, win_ref, wg_ref, sc_ref, wo_ref, out_ref, u_ref, *, tm, nts, S):
    it = pl.program_id(0) % nts
    x = x_ref[...]
    g = g_ref[...]
    hp = jnp.where(it == 0, 0.0, _rms(xp_ref[...], g))
    hn = jnp.where(it == nts - 1, 0.0, _rms(xn_ref[...], g))
    h = jnp.concatenate([hp, _rms(x, g), hn], axis=0).astype(BF16)
    u_ref[...] = _dot(h, win_ref[...])
    pos = it * tm + lax.broadcasted_iota(jnp.int32, (tm, 1), 0)
    acc = x
    for gi, win in enumerate(POOL_WINDOWS):
        rad = win // 2
        cols = slice(gi * POOL_GROUP, (gi + 1) * POOL_GROUP)
        ws = u_ref[pl.ds(HALO, tm), cols]
        u_c = ws
        for d in range(1, rad + 1):
            ws = ws + (u_ref[pl.ds(HALO - d, tm), cols] + u_ref[pl.ds(HALO + d, tm), cols])
        cnt = (jnp.minimum(pos + rad + 1, S) - jnp.maximum(pos - rad, 0)).astype(F32)
        dlt = (ws / cnt - u_c).astype(BF16)
        yg = _dot(dlt, wg_ref[gi]) * sc_ref[:, cols]
        acc = acc + _dot(yg.astype(BF16), wo_ref[cols, :])
    out_ref[...] = acc


def _pool(x2, g, w_in, w_group, scale, w_out, B, S, tm):
    T = B * S
    nts = S // tm
    prev, nxt = _halo_specs(tm, D_MODEL, T)
    row = pl.BlockSpec((tm, D_MODEL), lambda i: (i, 0))
    return pl.pallas_call(
        functools.partial(_pool_kernel, tm=tm, nts=nts, S=S),
        grid=(T // tm,),
        in_specs=[row, prev, nxt, _full(g.shape), _full(w_in.shape), _full(w_group.shape), _full(scale.shape),
                  _full(w_out.shape)],
        out_specs=row,
        out_shape=jax.ShapeDtypeStruct((T, D_MODEL), F32),
        scratch_shapes=[pltpu.VMEM((tm + 2 * HALO, POOL_W), F32)],
        compiler_params=_cparams(("parallel",)),
        name="pool_mixer",
    )(x2, x2, x2, g, w_in, w_group, scale, w_out)


def _gelu(x):
    return x * (0.5 * (1.0 + jnp.tanh(math.sqrt(2.0 / math.pi) * (x + 0.044715 * (x * x * x)))))


def _ffn_kernel(x_ref, xp_ref, xn_ref, p_ref, gf_ref, wup_ref, cw_ref, cb_ref, wd_ref,
                gp_ref, wpg_ref, bpg_ref, wpp_ref, gfin_ref, out_ref, h_ref, *, tm, rb, nts, final):
    it = pl.program_id(0) % nts
    g = gf_ref[...]
    h_ref[pl.ds(0, HALO), :] = jnp.where(it == 0, 0.0, _rms(xp_ref[...], g)).astype(BF16)
    h_ref[pl.ds(HALO, tm), :] = _rms(x_ref[...], g).astype(BF16)
    h_ref[pl.ds(HALO + tm, HALO), :] = jnp.where(it == nts - 1, 0.0, _rms(xn_ref[...], g)).astype(BF16)
    n_ext = rb + 2 * HALO
    for s in range(tm // rb):
        r0 = s * rb
        u = _dot(h_ref[pl.ds(r0, n_ext), :], wup_ref[...])
        mid = slice(HALO, HALO + rb)
        c = (pltpu.roll(u, 1, 0)[mid] * cw_ref[0:1, :] + u[mid] * cw_ref[1:2, :]
             + pltpu.roll(u, n_ext - 1, 0)[mid] * cw_ref[2:3, :] + cb_ref[...])
        act = _gelu(c[:, :D_FF]) * c[:, D_FF:]
        rows = pl.ds(r0, rb)
        x2 = x_ref[rows, :] + _dot(act.astype(BF16), wd_ref[...])
        h3 = _rms(x2, gp_ref[...]).astype(BF16)
        gate = _sigmoid(_dot(h3, wpg_ref[...]) + bpg_ref[...])
        x3 = x2 + gate * _dot(p_ref[rows, :].astype(BF16), wpp_ref[...])
        out_ref[rows, :] = _rms(x3, gfin_ref[...]) if final else x3


def _ffn(x2, p_all, layer, w, B, S, tm, rb, final):
    T = B * S
    nt = T // tm
    nts = S // tm
    prev, nxt = _halo_specs(tm, D_MODEL, T)
    row = pl.BlockSpec((tm, D_MODEL), lambda i: (i, 0))
    names = ('gf', 'wup', 'cw', 'cb', 'wd', 'gp', 'wpg', 'bpg', 'wpp', 'gfin')
    in_specs = [row, prev, nxt, pl.BlockSpec((tm, PLE_DIM), lambda i: (layer * nt + i, 0))]
    in_specs += [_resident(w[k].shape) for k in names]
    return pl.pallas_call(
        functools.partial(_ffn_kernel, tm=tm, rb=rb, nts=nts, final=final),
        grid=(nt,),
        in_specs=in_specs,
        out_specs=row,
        out_shape=jax.ShapeDtypeStruct((T, D_MODEL), F32),
        scratch_shapes=[pltpu.VMEM((tm + 2 * HALO, D_MODEL), BF16)],
        compiler_params=_cparams(("parallel",)),
        name="convffn_ple",
    )(x2, x2, x2, p_all, *[w[k] for k in names])


def _rope_tables(S):
    half = ROT_DIM // 2
    inv = jnp.float32(ROPE_THETA) ** (-jnp.arange(half, dtype=F32) * 2.0 / ROT_DIM)
    ang = jnp.arange(S, dtype=F32)[:, None] * inv[None, :]
    cos = jnp.cos(ang)
    sin = jnp.sin(ang)
    pad1 = jnp.ones((S, HEAD_DIM - ROT_DIM), F32)
    pad0 = jnp.zeros((S, HEAD_DIM - ROT_DIM), F32)
    cos_h = jnp.concatenate([cos, cos, pad1], axis=1)
    sin_h = jnp.concatenate([-sin, sin, pad0], axis=1)
    reps = LANES // HEAD_DIM
    return jnp.tile(cos_h, (1, reps)), jnp.tile(sin_h, (1, reps))


def _pack_ab_w_in(w):
    parts = [w[:, :RWKV_IN]]
    qkv = [w[:, RWKV_IN + s * ATTN_HEADS * HEAD_DIM: RWKV_IN + (s + 1) * ATTN_HEADS * HEAD_DIM] for s in range(3)]
    for g in range(len(ATTN_PATTERNS)):
        parts += [t[:, g * ATTN_OUT:(g + 1) * ATTN_OUT] for t in qkv]
    return jnp.concatenate(parts, axis=1).astype(BF16)


def _prepare(prm):
    vec = lambda a: a.reshape(1, -1).astype(F32)
    pk = {'ab_w_in': [_pack_ab_w_in(prm['ab_w_in'][j]) for j in range(prm['ab_w_in'].shape[0])],
          'ab_w_out': prm['ab_w_out'].astype(BF16),
          'c_w_in': prm['c_w_in'].astype(BF16), 'c_w_group': prm['c_w_group'].astype(BF16),
          'c_w_out': prm['c_w_out'].astype(BF16), 'ffn': []}
    for i in range(DEPTH):
        pk['ffn'].append({
            'gf': vec(prm['norm_ffn_g'][i]), 'wup': prm['ffn_w_up'][i].astype(BF16),
            'cw': prm['ffn_conv_w'][i], 'cb': vec(prm['ffn_conv_b'][i]),
            'wd': prm['ffn_w_down'][i].astype(BF16), 'gp': vec(prm['norm_ple_g'][i]),
            'wpg': prm['ple_w_gate'][i].astype(BF16), 'bpg': vec(prm['ple_b_gate'][i]),
            'wpp': prm['ple_w_proj'][i].astype(BF16), 'gfin': vec(prm['norm_final_g'])})
    return pk


def _trunk(x, p, prm, pk, tiles):
    B, S, _ = x.shape
    T = B * S
    vec = lambda a: a.reshape(1, -1)
    x2 = x.reshape(T, D_MODEL)
    p_all = p.reshape(p.shape[0] * T, PLE_DIM)
    cos_t, sin_t = _rope_tables(S)
    for i in range(DEPTH):
        j = i // 2
        gmix = vec(prm['norm_mix_g'][i])
        if i % 2 == 0:
            (r, kk, v, g, bonus, lw0, b0, kd0, lw1, b1, kd1, q0, q1, q2) = _inproj(
                x2, gmix, pk['ab_w_in'][j], cos_t, sin_t, prm, j, B, S, tiles['inproj'])
            yf, yb = _scan(r, kk, v, lw0, b0, kd0, lw1, b1, kd1, B, S, tiles['scan'], tiles['scan_b'])
            attn = [_attn(q, B, S, dil) for q, (_, dil) in zip((q0, q1, q2), ATTN_PATTERNS)]
            x2 = _post(x2, yf, yb, bonus, g, attn, vec(prm['rwkv_ln_g'][j]), vec(prm['rwkv_ln_b'][j]),
                       pk['ab_w_out'][j], B, S, tiles['post'])
        else:
            x2 = _pool(x2, gmix, pk['c_w_in'][j], pk['c_w_group'][j], vec(prm['c_scale'][j]), pk['c_w_out'][j],
                       B, S, tiles['pool'])
        x2 = _ffn(x2, p_all, i, pk['ffn'][i], B, S, tiles['ffn_m'], tiles['ffn_f'], final=(i == DEPTH - 1))
    return x2.reshape(B, S, D_MODEL)


_TILES = {'inproj': 256, 'scan': 256, 'scan_b': 2, 'post': 512, 'pool': 512, 'ffn_m': 512, 'ffn_f': 256}


def kernel(x_prompt, x_sample, p_prompt, p_sample, ab_w_in, ab_w_out, rwkv_mu, rwkv_w0, rwkv_w_up, rwkv_a0, rwkv_a_up, rwkv_g_up, rwkv_k_k, rwkv_k_a, rwkv_r_k, rwkv_ln_g, rwkv_ln_b, c_w_in, c_w_group, c_scale, c_w_out, norm_mix_g, norm_ffn_g, norm_ple_g, norm_final_g, ffn_w_up, ffn_conv_w, ffn_conv_b, ffn_w_down, ple_w_proj, ple_w_gate, ple_b_gate):
    prm = {
        'ab_w_in': ab_w_in, 'ab_w_out': ab_w_out, 'rwkv_mu': rwkv_mu, 'rwkv_w0': rwkv_w0,
        'rwkv_w_up': rwkv_w_up, 'rwkv_a0': rwkv_a0, 'rwkv_a_up': rwkv_a_up, 'rwkv_g_up': rwkv_g_up,
        'rwkv_k_k': rwkv_k_k, 'rwkv_k_a': rwkv_k_a, 'rwkv_r_k': rwkv_r_k, 'rwkv_ln_g': rwkv_ln_g,
        'rwkv_ln_b': rwkv_ln_b, 'c_w_in': c_w_in, 'c_w_group': c_w_group, 'c_scale': c_scale,
        'c_w_out': c_w_out, 'norm_mix_g': norm_mix_g, 'norm_ffn_g': norm_ffn_g, 'norm_ple_g': norm_ple_g,
        'norm_final_g': norm_final_g, 'ffn_w_up': ffn_w_up, 'ffn_conv_w': ffn_conv_w,
        'ffn_conv_b': ffn_conv_b, 'ffn_w_down': ffn_w_down, 'ple_w_proj': ple_w_proj,
        'ple_w_gate': ple_w_gate, 'ple_b_gate': ple_b_gate,
    }
    pk = _prepare(prm)
    y_prompt = _trunk(x_prompt, p_prompt, prm, pk, _TILES)
    y_sample = _trunk(x_sample, p_sample, prm, pk, _TILES)
    return (y_prompt, y_sample)
```

```python
import functools
import math

import jax
import jax.numpy as jnp
from jax import lax
from jax.experimental import pallas as pl
from jax.experimental.pallas import tpu as pltpu

D_MODEL = 1024
DEPTH = 4
PLE_DIM = 256
HEAD_DIM = 64
RMS_EPS = 1e-6
RWKV_HEADS = 8
RWKV_W = RWKV_HEADS * HEAD_DIM
DECAY_LORA = 64
ICLR_LORA = 64
GATE_LORA = 128
RWKV_IN = 3 * RWKV_W + 2 * DECAY_LORA + 2 * ICLR_LORA + GATE_LORA
GN_EPS = 64e-5
ATTN_PATTERNS = ((128, 1), (512, 4), (2048, 16))
ATTN_HEADS_PER_GROUP = 4
ATTN_HEADS = ATTN_HEADS_PER_GROUP * len(ATTN_PATTERNS)
ATTN_OUT = ATTN_HEADS_PER_GROUP * HEAD_DIM
ATTN_GROUP_IN = 3 * ATTN_OUT
ATTN_HALF = 64
ROPE_THETA = 500000.0
ROT_DIM = HEAD_DIM // 4
NEG_INF = -1e30
AB_OUT = RWKV_W + ATTN_OUT
POOL_WINDOWS = (2, 4, 8, 16)
POOL_GROUP = 256
POOL_W = len(POOL_WINDOWS) * POOL_GROUP
D_FF = 2816

LANES = 128
SUBLANES = 8
HALO = SUBLANES
CHUNK = 64
SCAN_GROUP_LANES = LANES
VMEM_LIMIT = 56 * 1024 * 1024

F32 = jnp.float32
BF16 = jnp.bfloat16


def _dot(a, b):
    return jnp.dot(a, b, preferred_element_type=F32)


def _dot_nt(a, b):
    return lax.dot_general(a, b, (((1,), (1,)), ((), ())), preferred_element_type=F32)


def _split(x):
    hi = x.astype(BF16)
    lo = (x - hi.astype(F32)).astype(BF16)
    return hi, lo


def _dot3(ap, bp):
    return _dot(ap[0], bp[0]) + (_dot(ap[0], bp[1]) + _dot(ap[1], bp[0]))


def _dot3_nt(ap, bp):
    return _dot_nt(ap[0], bp[0]) + (_dot_nt(ap[0], bp[1]) + _dot_nt(ap[1], bp[0]))


def _rms(x, g):
    return x * lax.rsqrt(jnp.mean(x * x, axis=-1, keepdims=True) + RMS_EPS) * g


def _sigmoid(x):
    return 1.0 / (1.0 + jnp.exp(-x))


def _head_ones():
    r = lax.broadcasted_iota(jnp.int32, (LANES, LANES), 0) // HEAD_DIM
    c = lax.broadcasted_iota(jnp.int32, (LANES, LANES), 1) // HEAD_DIM
    return jnp.where(r == c, 1.0, 0.0).astype(BF16)


def _head_sum(x, ones):
    outs = []
    for j in range(x.shape[1] // LANES):
        hi, lo = _split(x[:, j * LANES:(j + 1) * LANES])
        outs.append(_dot(hi, ones) + _dot(lo, ones))
    return jnp.concatenate(outs, axis=1)


def _shift_rows(z, prev_row, next_row):
    n = z.shape[0]
    row = lax.broadcasted_iota(jnp.int32, z.shape, 0)
    zp = jnp.where(row == 0, prev_row, pltpu.roll(z, 1, 0))
    zn = jnp.where(row == n - 1, next_row, pltpu.roll(z, n - 1, 0))
    return zp, zn


def _cparams(sem):
    return pltpu.CompilerParams(dimension_semantics=sem, vmem_limit_bytes=VMEM_LIMIT)


def _halo_specs(tm, width, n_rows):
    nb = n_rows // HALO
    step = tm // HALO
    prev = pl.BlockSpec((HALO, width), lambda i: (jnp.maximum(i * step - 1, 0), 0))
    nxt = pl.BlockSpec((HALO, width), lambda i: (jnp.minimum((i + 1) * step, nb - 1), 0))
    return prev, nxt


def _full(shape):
    nd = len(shape)
    return pl.BlockSpec(shape, lambda *_: (0,) * nd)


def _resident(shape):
    nd = len(shape)
    return pl.BlockSpec(shape, lambda *_: (0,) * nd, pipeline_mode=pl.Buffered(1))


def _inproj_kernel(x_ref, xp_ref, xn_ref, g_ref, w_ref, cos_ref, sin_ref,
                   mu_ref, w0_ref, wuh_ref, wul_ref, a0_ref, auh_ref, aul_ref, guh_ref, gul_ref, kk_ref, ka_ref, rk_ref,
                   r_o, kk_o, v_o, g_o, bonus_o, lw0_o, b0_o, kd0_o, lw1_o, b1_o, kd1_o, q0_ref, q1_ref, q2_ref,
                   scr_ref, *, tm, nts):
    it = pl.program_id(0) % nts
    gm = g_ref[...]
    hm = _rms(x_ref[...], gm)
    hp = jnp.where(it == 0, 0.0, _rms(xp_ref[...], gm))
    hn = jnp.where(it == nts - 1, 0.0, _rms(xn_ref[...], gm))
    h = hm.astype(BF16)
    z_ext = _dot(jnp.concatenate([hp, hm, hn], axis=0).astype(BF16), w_ref[:, :RWKV_IN])
    mid = slice(HALO, HALO + tm)
    z = z_ext[mid]
    zs = 0.5 * (pltpu.roll(z_ext, 1, 0)[mid] + pltpu.roll(z_ext, tm + 2 * HALO - 1, 0)[mid])
    zz = z + mu_ref[...] * (zs - z)
    W = RWKV_W
    r = zz[:, 0:W]
    k = zz[:, W:2 * W]
    v = zz[:, 2 * W:3 * W]
    o = 3 * W
    wd = jnp.tanh(zz[:, o:o + 2 * DECAY_LORA])
    o += 2 * DECAY_LORA
    ad = zz[:, o:o + 2 * ICLR_LORA]
    o += 2 * ICLR_LORA
    gd = _sigmoid(zz[:, o:])
    wlog = w0_ref[...] + _dot3(_split(wd), (wuh_ref[...], wul_ref[...]))
    aa = _sigmoid(a0_ref[...] + _dot3(_split(ad), (auh_ref[...], aul_ref[...])))
    g_o[...] = _dot3(_split(gd), (guh_ref[...], gul_ref[...]))
    ones = _head_ones()
    kkv = k * kk_ref[...]
    kkn = kkv / jnp.sqrt(_head_sum(kkv * kkv, ones) + 1e-12)
    r_o[...] = r
    kk_o[...] = kkn
    v_o[...] = v
    ka = ka_ref[...]
    kd_sum = None
    for d, (lw_o, b_o, kd_o) in enumerate(((lw0_o, b0_o, kd0_o), (lw1_o, b1_o, kd1_o))):
        a_d = aa[:, d * W:(d + 1) * W]
        lw_o[...] = -math.exp(-0.5) * _sigmoid(wlog[:, d * W:(d + 1) * W])
        kd = k * (1.0 + (a_d - 1.0) * ka)
        kd_o[...] = kd
        b_o[...] = kkn * a_d
        kd_sum = kd if kd_sum is None else kd_sum + kd
    bonus_o[...] = _head_sum(r * kd_sum * rk_ref[...], ones) * v
    cs = cos_ref[...]
    sn = sin_ref[...]
    lane = lax.broadcasted_iota(jnp.int32, (tm, LANES), 1)
    first = (lane & (HEAD_DIM - 1)) < (ROT_DIM // 2)
    outs = (q0_ref, q1_ref, q2_ref)
    for g, (_, dil) in enumerate(ATTN_PATTERNS):
        base = RWKV_IN + g * ATTN_GROUP_IN
        t = _dot(h, w_ref[:, base:base + ATTN_GROUP_IN])
        for c in range(ATTN_GROUP_IN // LANES):
            tc = t[:, c * LANES:(c + 1) * LANES]
            if c < 2 * ATTN_OUT // LANES:
                rot = jnp.where(first, pltpu.roll(tc, LANES - ROT_DIM // 2, 1), pltpu.roll(tc, ROT_DIM // 2, 1))
                tc = tc * cs + rot * sn
            if dil == 1:
                outs[g][0, :, c * LANES:(c + 1) * LANES] = tc
            else:
                scr_ref[c] = tc
        if dil > 1:
            for c in range(ATTN_GROUP_IN // LANES):
                for r in range(dil):
                    outs[g][r, :, c * LANES:(c + 1) * LANES] = scr_ref[c, pl.ds(r, tm // dil, stride=dil), :]


def _inproj(x2, g, w, cos_t, sin_t, prm, j, B, S, tm):
    T = B * S
    nts = S // tm
    W = RWKV_W
    vec = lambda a: a.reshape(1, -1)

    def bdiag(m):
        L = m.shape[1]
        return _split(jnp.zeros((2 * L, 2 * W), F32).at[:L, :W].set(m[0]).at[L:, W:].set(m[1]))

    wuh, wul = bdiag(prm['rwkv_w_up'][j])
    auh, aul = bdiag(prm['rwkv_a_up'][j])
    guh, gul = _split(prm['rwkv_g_up'][j])
    params = [vec(prm['rwkv_mu'][j]), vec(prm['rwkv_w0'][j]), wuh, wul, vec(prm['rwkv_a0'][j]), auh, aul,
              guh, gul, vec(prm['rwkv_k_k'][j]), vec(prm['rwkv_k_a'][j]), vec(prm['rwkv_r_k'][j])]
    prev, nxt = _halo_specs(tm, D_MODEL, T)
    row = pl.BlockSpec((tm, W), lambda i: (i, 0))
    out_shape = [jax.ShapeDtypeStruct((T, W), F32)] * 11
    out_specs = [row] * 11
    for _, dil in ATTN_PATTERNS:
        out_shape.append(jax.ShapeDtypeStruct((B, dil, S // dil, ATTN_GROUP_IN), F32))
        out_specs.append(pl.BlockSpec((None, dil, tm // dil, ATTN_GROUP_IN),
                                      lambda i: (i // nts, 0, i % nts, 0)))
    return pl.pallas_call(
        functools.partial(_inproj_kernel, tm=tm, nts=nts),
        grid=(T // tm,),
        in_specs=[pl.BlockSpec((tm, D_MODEL), lambda i: (i, 0)), prev, nxt,
                  _resident((1, D_MODEL)),
                  _resident(w.shape),
                  pl.BlockSpec((tm, LANES), lambda i: (i % nts, 0)),
                  pl.BlockSpec((tm, LANES), lambda i: (i % nts, 0))] + [_resident(a.shape) for a in params],
        out_specs=out_specs,
        out_shape=out_shape,
        scratch_shapes=[pltpu.VMEM((ATTN_GROUP_IN // LANES, tm, LANES), F32)],
        compiler_params=_cparams(("parallel",)),
        name="inproj",
    )(x2, x2, x2, g, w, cos_t, sin_t, *params)


def _stack2(x):
    head = lax.broadcasted_iota(jnp.int32, x.shape, 1) // HEAD_DIM
    zero = jnp.zeros_like(x)
    return jnp.concatenate([jnp.where(head == h, x, zero) for h in range(x.shape[1] // HEAD_DIM)], axis=0)


def _pm(m, x):
    return _dot(m.astype(BF16), _stack2(x.astype(BF16)))


def _wkv_scaled(r, kk, v, lw, b, kd, fwd):
    C = CHUNK
    ti = lax.broadcasted_iota(jnp.int32, (C, C), 0)
    tj = lax.broadcasted_iota(jnp.int32, (C, C), 1)
    cum = jnp.where((tj <= ti) if fwd else (tj >= ti), 1.0, 0.0).astype(BF16)
    lh, ll = _split(lw)
    G = _dot(cum, lh) + _dot(cum, ll)
    g_tot = G[C - 1:C, :] if fwd else G[0:1, :]
    e_neg = jnp.exp(-G)
    rt = r * jnp.exp(G)
    at = -kk * jnp.exp(G - lw)
    kt = kd * e_neg
    bt = b * e_neg
    e_tot = jnp.exp(g_tot)
    kb = jnp.concatenate([kt * e_tot, bt * e_tot], axis=0)
    return dict(at=at, rt=rt, kt=kt, bt=bt, kb=kb, v=v, e_tot=e_tot, fwd=fwd)


def _wkv_pairs(chains):
    C = CHUNK
    hw = chains[0]['at'].shape[1]
    gw = hw // HEAD_DIM * C
    trow = lax.broadcasted_iota(jnp.int32, (C, gw), 0)
    tcol = lax.broadcasted_iota(jnp.int32, (C, gw), 1) & (C - 1)
    eye = jnp.where(tcol == trow, 1.0, 0.0)
    hr = lax.broadcasted_iota(jnp.int32, (hw, hw), 0) // HEAD_DIM
    hc = lax.broadcasted_iota(jnp.int32, (hw, hw), 1) // HEAD_DIM
    for ch in chains:
        ch['x'] = jnp.concatenate([ch['at'], ch['rt']], axis=0).astype(BF16)
    for ch in chains:
        kb2 = jnp.concatenate([_stack2(ch['kt'].astype(BF16)), _stack2(ch['bt'].astype(BF16))], axis=0)
        sc = _dot_nt(ch['x'], kb2)
        ch['sc_k'] = sc[:, :gw]
        ch['sc_b'] = sc[:, gw:]
        ch['p1'] = _dot_nt(ch['x'], ch['S'].astype(BF16))
    for ch in chains:
        strict = (tcol < trow) if ch['fwd'] else (tcol > trow)
        incl = (tcol <= trow) if ch['fwd'] else (tcol >= trow)
        ch['a_kk'] = jnp.concatenate([jnp.where(strict, ch['sc_k'][:C], 0.0),
                                      jnp.where(incl, ch['sc_k'][C:], 0.0)], axis=0)
        ch['n'] = jnp.where(strict, ch['sc_b'][:C], 0.0)
        ch['a_rb'] = jnp.where(incl, ch['sc_b'][C:], 0.0)
        ch['t'] = eye + ch['n']
    for j in range(6):
        for ch in chains:
            n = ch['n']
            if j == 0:
                ch['n'] = _pm(n, n)
                ch['p2'] = _pm(ch['a_kk'], ch['v'])
            elif j < 5:
                m = _pm(jnp.concatenate([ch['t'], n], axis=0), n)
                ch['t'] = ch['t'] + m[:C]
                ch['n'] = m[C:]
            else:
                ch['t'] = ch['t'] + _pm(ch['t'], n)
    for ch in chains:
        ch['u'] = _pm(ch['t'], ch['p1'][:C] + ch['p2'][:C])
    for ch in chains:
        ch['y'] = ch['p1'][C:] + ch['p2'][C:] + _pm(ch['a_rb'], ch['u'])
        vu_t = jnp.concatenate([ch['v'], ch['u']], axis=0).T
        upd = _dot(vu_t.astype(BF16), ch['kb'].astype(BF16))
        ch['s_new'] = ch['S'] * ch['e_tot'] + jnp.where(hr == hc, upd, 0.0)


def _scan_kernel(rf, kkf, vf, lwf, bf, kdf, rb, kkb, vb, lwb, bb, kdb, yf_o, yb_o, sf_ref, sb_ref, *, nc, nb):
    @pl.when(pl.program_id(1) == 0)
    def _():
        sf_ref[...] = jnp.zeros_like(sf_ref)
        sb_ref[...] = jnp.zeros_like(sb_ref)

    npairs = RWKV_W // SCAN_GROUP_LANES

    def body(c, carry):
        slf = pl.ds(pl.multiple_of(c * CHUNK, CHUNK), CHUNK)
        slb = pl.ds(pl.multiple_of((nc - 1 - c) * CHUNK, CHUNK), CHUNK)
        chains = []
        for bi in range(nb):
            dirs = (_wkv_scaled(*[ref[bi, slf, :] for ref in (rf, kkf, vf, lwf, bf, kdf)], True),
                    _wkv_scaled(*[ref[bi, slb, :] for ref in (rb, kkb, vb, lwb, bb, kdb)], False))
            for d, s_ref in zip(dirs, (sf_ref, sb_ref)):
                for p in range(npairs):
                    sl = slice(p * SCAN_GROUP_LANES, (p + 1) * SCAN_GROUP_LANES)
                    ch = {k: (val if k == 'fwd' else val[:, sl]) for k, val in d.items()}
                    ch['S'] = s_ref[bi * npairs + p]
                    chains.append(ch)
        _wkv_pairs(chains)
        chains = iter(chains)
        for bi in range(nb):
            for s_ref, y_o, sl in ((sf_ref, yf_o, slf), (sb_ref, yb_o, slb)):
                ys = []
                for p in range(npairs):
                    ch = next(chains)
                    s_ref[bi * npairs + p] = ch['s_new']
                    ys.append(ch['y'])
                y_o[bi, sl, :] = jnp.concatenate(ys, axis=1)
        return carry

    lax.fori_loop(0, nc, body, 0)


def _scan(r, kk, v, lw0, b0, kd0, lw1, b1, kd1, B, S, tb, nb):
    assert B % nb == 0 and S % tb == 0 and tb % CHUNK == 0
    nblk = S // tb
    shp = (B, S, RWKV_W)
    args = [a.reshape(shp) for a in (r, kk, v, lw0, b0, kd0, r, kk, v, lw1, b1, kd1)]
    fwd = pl.BlockSpec((nb, tb, RWKV_W), lambda bi, j: (bi, j, 0))
    bwd = pl.BlockSpec((nb, tb, RWKV_W), lambda bi, j: (bi, nblk - 1 - j, 0))
    yf, yb = pl.pallas_call(
        functools.partial(_scan_kernel, nc=tb // CHUNK, nb=nb),
        grid=(B // nb, nblk),
        in_specs=[fwd] * 6 + [bwd] * 6,
        out_specs=[fwd, bwd],
        out_shape=[jax.ShapeDtypeStruct(shp, F32)] * 2,
        scratch_shapes=[pltpu.VMEM((nb * RWKV_W // SCAN_GROUP_LANES, SCAN_GROUP_LANES, SCAN_GROUP_LANES), F32)] * 2,
        compiler_params=_cparams(("parallel", "arbitrary")),
        name="wkv_scan",
    )(*args)
    return yf.reshape(B * S, RWKV_W), yb.reshape(B * S, RWKV_W)


def _attn_kernel(band_ref, q_ref, km_ref, vm_ref, kp_ref, vp_ref, kn_ref, vn_ref, o_ref, lse_ref, *, Q, L, rpb):
    j = pl.program_id(2)
    Wn = Q + 2 * ATTN_HALF
    kpos = j * Q - ATTN_HALF + lax.broadcasted_iota(jnp.int32, (1, Wn), 1)
    bias = band_ref[...] + jnp.where(jnp.minimum(kpos, L - 1 - kpos) >= 0, 0.0, NEG_INF)
    lane = lax.broadcasted_iota(jnp.int32, (Q, ATTN_OUT), 1) // HEAD_DIM

    def residue(r, carry):
        q = q_ref[r] * (HEAD_DIM ** -0.5)
        kw = jnp.concatenate([kp_ref[r], km_ref[r], kn_ref[r]], axis=0).astype(BF16)
        vw = jnp.concatenate([vp_ref[r], vm_ref[r], vn_ref[r]], axis=0).astype(BF16)
        heads = range(ATTN_HEADS_PER_GROUP)
        ss = [_dot_nt(jnp.where(lane == hh, q, 0.0).astype(BF16), kw) + bias for hh in heads]
        ms = [jnp.max(s, axis=-1, keepdims=True) for s in ss]
        ps = [jnp.exp(s - m) for s, m in zip(ss, ms)]
        dens = [jnp.sum(p, axis=-1, keepdims=True) for p in ps]
        ohs = [_dot(p.astype(BF16), vw) for p in ps]
        o_acc = jnp.zeros((Q, ATTN_OUT), F32)
        l_acc = jnp.zeros((Q, ATTN_OUT), F32)
        for hh in heads:
            o_acc = jnp.where(lane == hh, ohs[hh] / dens[hh], o_acc)
            l_acc = jnp.where(lane == hh, ms[hh] + jnp.log(dens[hh]), l_acc)
        o_ref[r] = o_acc
        lse_ref[r] = l_acc
        return carry

    lax.fori_loop(0, rpb, residue, 0)


def _attn(qkv, B, S, dil):
    L = S // dil
    Q = min(256, L)
    rpb = min(dil, 2048 // Q)
    nq = L // Q
    qb = Q // ATTN_HALF
    nhb = L // ATTN_HALF
    Wn = Q + 2 * ATTN_HALF
    rel = jnp.arange(Wn)[None, :] - ATTN_HALF - jnp.arange(Q)[:, None]
    band = jnp.where(jnp.abs(rel) <= ATTN_HALF, 0.0, NEG_INF).astype(F32)

    def mid(col):
        return pl.BlockSpec((None, rpb, Q, ATTN_OUT), lambda bi, rb, j: (bi, rb, j, col))

    def prev(col):
        return pl.BlockSpec((None, rpb, ATTN_HALF, ATTN_OUT),
                            lambda bi, rb, j: (bi, rb, jnp.maximum(j * qb - 1, 0), col))

    def nxt(col):
        return pl.BlockSpec((None, rpb, ATTN_HALF, ATTN_OUT),
                            lambda bi, rb, j: (bi, rb, jnp.minimum((j + 1) * qb, nhb - 1), col))

    out = pl.BlockSpec((None, rpb, Q, ATTN_OUT), lambda bi, rb, j: (bi, rb, j, 0))
    return pl.pallas_call(
        functools.partial(_attn_kernel, Q=Q, L=L, rpb=rpb),
        grid=(B, dil // rpb, nq),
        in_specs=[_resident(band.shape), mid(0), mid(1), mid(2), prev(1), prev(2), nxt(1), nxt(2)],
        out_specs=[out, out],
        out_shape=[jax.ShapeDtypeStruct((B, dil, L, ATTN_OUT), F32)] * 2,
        compiler_params=_cparams(("parallel", "parallel", "parallel")),
        name="band_attn_d%d" % dil,
    )(band, qkv, qkv, qkv, qkv, qkv, qkv, qkv)


def _post_kernel(x_ref, yf_ref, yb_ref, bonus_ref, g_ref, o0, l0, o1, l1, o2, l2, lng_ref, lnb_ref, wo_ref,
                 out_ref, oscr, lscr, *, tm):
    ones = _head_ones()
    y = yf_ref[...] + yb_ref[...]
    yc = y - _head_sum(y, ones) * (1.0 / HEAD_DIM)
    var = _head_sum(yc * yc, ones) * (1.0 / HEAD_DIM)
    yn = yc * lax.rsqrt(var + GN_EPS) * lng_ref[...] + lnb_ref[...]
    ya = (yn + bonus_ref[...]) * g_ref[...]
    nlb = ATTN_OUT // LANES
    os_, ls = [], []
    for gi, (o_r, l_r, (_, dil)) in enumerate(zip((o0, o1, o2), (l0, l1, l2), ATTN_PATTERNS)):
        if dil == 1:
            os_.append(o_r[0])
            ls.append(l_r[0])
            continue
        for c in range(nlb):
            for r in range(dil):
                oscr[gi * nlb + c, pl.ds(r, tm // dil, stride=dil), :] = o_r[r, :, c * LANES:(c + 1) * LANES]
                lscr[gi * nlb + c, pl.ds(r, tm // dil, stride=dil), :] = l_r[r, :, c * LANES:(c + 1) * LANES]
        os_.append(jnp.concatenate([oscr[gi * nlb + c] for c in range(nlb)], axis=1))
        ls.append(jnp.concatenate([lscr[gi * nlb + c] for c in range(nlb)], axis=1))
    m = jnp.maximum(jnp.maximum(ls[0], ls[1]), ls[2])
    es = [jnp.exp(l - m) for l in ls]
    num = es[0] * os_[0] + es[1] * os_[1] + es[2] * os_[2]
    yb = num / (es[0] + es[1] + es[2])
    out_ref[...] = (x_ref[...] + _dot(ya.astype(BF16), wo_ref[:RWKV_W, :])
                    + _dot(yb.astype(BF16), wo_ref[RWKV_W:, :]))


def _post(x2, yf, yb, bonus, g, attn, lng, lnb, wo, B, S, tm):
    T = B * S
    nts = S // tm
    row = lambda w: pl.BlockSpec((tm, w), lambda i: (i, 0))
    in_specs = [row(D_MODEL)] + [row(RWKV_W)] * 4
    args = [x2, yf, yb, bonus, g]
    for (o, l), (_, dil) in zip(attn, ATTN_PATTERNS):
        spec = pl.BlockSpec((None, dil, tm // dil, ATTN_OUT), lambda i: (i // nts, 0, i % nts, 0))
        in_specs += [spec, spec]
        args += [o, l]
    in_specs += [_full(lng.shape), _full(lnb.shape), _full(wo.shape)]
    args += [lng, lnb, wo]
    ng = len(ATTN_PATTERNS)
    return pl.pallas_call(
        functools.partial(_post_kernel, tm=tm),
        grid=(T // tm,),
        in_specs=in_specs,
        out_specs=row(D_MODEL),
        out_shape=jax.ShapeDtypeStruct((T, D_MODEL), F32),
        scratch_shapes=[pltpu.VMEM((ng * ATTN_OUT // LANES, tm, LANES), F32)] * 2,
        compiler_params=_cparams(("parallel",)),
        name="mixer_ab_out",
    )(*args)


def _pool_kernel(x_ref, xp_ref, xn_ref, g_ref, win_ref, wg_ref, sc_ref, wo_ref, out_ref, u_ref, *, tm, nts, S):
    it = pl.program_id(0) % nts
    x = x_ref[...]
    g = g_ref[...]
    hp = jnp.where(it == 0, 0.0, _rms(xp_ref[...], g))
    hn = jnp.where(it == nts - 1, 0.0, _rms(xn_ref[...], g))
    h = jnp.concatenate([hp, _rms(x, g), hn], axis=0).astype(BF16)
    u_ref[...] = _dot(h, win_ref[...])
    pos = it * tm + lax.broadcasted_iota(jnp.int32, (tm, 1), 0)
    acc = x
    for gi, win in enumerate(POOL_WINDOWS):
        rad = win // 2
        cols = slice(gi * POOL_GROUP, (gi + 1) * POOL_GROUP)
        ws = u_ref[pl.ds(HALO, tm), cols]
        u_c = ws
        for d in range(1, rad + 1):
            ws = ws + (u_ref[pl.ds(HALO - d, tm), cols] + u_ref[pl.ds(HALO + d, tm), cols])
        cnt = (jnp.minimum(pos + rad + 1, S) - jnp.maximum(pos - rad, 0)).astype(F32)
        dlt = (ws / cnt - u_c).astype(BF16)
        yg = _dot(dlt, wg_ref[gi]) * sc_ref[:, cols]
        acc = acc + _dot(yg.astype(BF16), wo_ref[cols, :])
    out_ref[...] = acc


def _pool(x2, g, w_in, w_group, scale, w_out, B, S, tm):
    T = B * S
    nts = S // tm
    prev, nxt = _halo_specs(tm, D_MODEL, T)
    row = pl.BlockSpec((tm, D_MODEL), lambda i: (i, 0))
    return pl.pallas_call(
        functools.partial(_pool_kernel, tm=tm, nts=nts, S=S),
        grid=(T // tm,),
        in_specs=[row, prev, nxt, _full(g.shape), _full(w_in.shape), _full(w_group.shape), _full(scale.shape),
                  _full(w_out.shape)],
        out_specs=row,
        out_shape=jax.ShapeDtypeStruct((T, D_MODEL), F32),
        scratch_shapes=[pltpu.VMEM((tm + 2 * HALO, POOL_W), F32)],
        compiler_params=_cparams(("parallel",)),
        name="pool_mixer",
    )(x2, x2, x2, g, w_in, w_group, scale, w_out)


def _gelu(x):
    return x * (0.5 * (1.0 + jnp.tanh(math.sqrt(2.0 / math.pi) * (x + 0.044715 * (x * x * x)))))


def _ffn_kernel(x_ref, xp_ref, xn_ref, p_ref, gf_ref, wup_ref, cw_ref, cb_ref, wd_ref,
                gp_ref, wpg_ref, bpg_ref, wpp_ref, gfin_ref, out_ref, h_ref, *, tm, rb, nts, final):
    it = pl.program_id(0) % nts
    g = gf_ref[...]
    h_ref[pl.ds(0, HALO), :] = jnp.where(it == 0, 0.0, _rms(xp_ref[...], g)).astype(BF16)
    h_ref[pl.ds(HALO, tm), :] = _rms(x_ref[...], g).astype(BF16)
    h_ref[pl.ds(HALO + tm, HALO), :] = jnp.where(it == nts - 1, 0.0, _rms(xn_ref[...], g)).astype(BF16)
    n_ext = rb + 2 * HALO
    for s in range(tm // rb):
        r0 = s * rb
        u = _dot(h_ref[pl.ds(r0, n_ext), :], wup_ref[...])
        mid = slice(HALO, HALO + rb)
        c = (pltpu.roll(u, 1, 0)[mid] * cw_ref[0:1, :] + u[mid] * cw_ref[1:2, :]
             + pltpu.roll(u, n_ext - 1, 0)[mid] * cw_ref[2:3, :] + cb_ref[...])
        act = _gelu(c[:, :D_FF]) * c[:, D_FF:]
        rows = pl.ds(r0, rb)
        x2 = x_ref[rows, :] + _dot(act.astype(BF16), wd_ref[...])
        h3 = _rms(x2, gp_ref[...]).astype(BF16)
        gate = _sigmoid(_dot(h3, wpg_ref[...]) + bpg_ref[...])
        x3 = x2 + gate * _dot(p_ref[rows, :].astype(BF16), wpp_ref[...])
        out_ref[rows, :] = _rms(x3, gfin_ref[...]) if final else x3


def _ffn(x2, p_all, layer, w, B, S, tm, rb, final):
    T = B * S
    nt = T // tm
    nts = S // tm
    prev, nxt = _halo_specs(tm, D_MODEL, T)
    row = pl.BlockSpec((tm, D_MODEL), lambda i: (i, 0))
    names = ('gf', 'wup', 'cw', 'cb', 'wd', 'gp', 'wpg', 'bpg', 'wpp', 'gfin')
    in_specs = [row, prev, nxt, pl.BlockSpec((tm, PLE_DIM), lambda i: (layer * nt + i, 0))]
    in_specs += [_resident(w[k].shape) for k in names]
    return pl.pallas_call(
        functools.partial(_ffn_kernel, tm=tm, rb=rb, nts=nts, final=final),
        grid=(nt,),
        in_specs=in_specs,
        out_specs=row,
        out_shape=jax.ShapeDtypeStruct((T, D_MODEL), F32),
        scratch_shapes=[pltpu.VMEM((tm + 2 * HALO, D_MODEL), BF16)],
        compiler_params=_cparams(("parallel",)),
        name="convffn_ple",
    )(x2, x2, x2, p_all, *[w[k] for k in names])


def _rope_tables(S):
    half = ROT_DIM // 2
    inv = jnp.float32(ROPE_THETA) ** (-jnp.arange(half, dtype=F32) * 2.0 / ROT_DIM)
    ang = jnp.arange(S, dtype=F32)[:, None] * inv[None, :]
    cos = jnp.cos(ang)
    sin = jnp.sin(ang)
    pad1 = jnp.ones((S, HEAD_DIM - ROT_DIM), F32)
    pad0 = jnp.zeros((S, HEAD_DIM - ROT_DIM), F32)
    cos_h = jnp.concatenate([cos, cos, pad1], axis=1)
    sin_h = jnp.concatenate([-sin, sin, pad0], axis=1)
    reps = LANES // HEAD_DIM
    return jnp.tile(cos_h, (1, reps)), jnp.tile(sin_h, (1, reps))


def _pack_ab_w_in(w):
    parts = [w[:, :RWKV_IN]]
    qkv = [w[:, RWKV_IN + s * ATTN_HEADS * HEAD_DIM: RWKV_IN + (s + 1) * ATTN_HEADS * HEAD_DIM] for s in range(3)]
    for g in range(len(ATTN_PATTERNS)):
        parts += [t[:, g * ATTN_OUT:(g + 1) * ATTN_OUT] for t in qkv]
    return jnp.concatenate(parts, axis=1).astype(BF16)


def _prepare(prm):
    vec = lambda a: a.reshape(1, -1).astype(F32)
    pk = {'ab_w_in': [_pack_ab_w_in(prm['ab_w_in'][j]) for j in range(prm['ab_w_in'].shape[0])],
          'ab_w_out': prm['ab_w_out'].astype(BF16),
          'c_w_in': prm['c_w_in'].astype(BF16), 'c_w_group': prm['c_w_group'].astype(BF16),
          'c_w_out': prm['c_w_out'].astype(BF16), 'ffn': []}
    for i in range(DEPTH):
        pk['ffn'].append({
            'gf': vec(prm['norm_ffn_g'][i]), 'wup': prm['ffn_w_up'][i].astype(BF16),
            'cw': prm['ffn_conv_w'][i], 'cb': vec(prm['ffn_conv_b'][i]),
            'wd': prm['ffn_w_down'][i].astype(BF16), 'gp': vec(prm['norm_ple_g'][i]),
            'wpg': prm['ple_w_gate'][i].astype(BF16), 'bpg': vec(prm['ple_b_gate'][i]),
            'wpp': prm['ple_w_proj'][i].astype(BF16), 'gfin': vec(prm['norm_final_g'])})
    return pk


def _trunk(x, p, prm, pk, tiles):
    B, S, _ = x.shape
    T = B * S
    vec = lambda a: a.reshape(1, -1)
    x2 = x.reshape(T, D_MODEL)
    p_all = p.reshape(p.shape[0] * T, PLE_DIM)
    cos_t, sin_t = _rope_tables(S)
    for i in range(DEPTH):
        j = i // 2
        gmix = vec(prm['norm_mix_g'][i])
        if i % 2 == 0:
            (r, kk, v, g, bonus, lw0, b0, kd0, lw1, b1, kd1, q0, q1, q2) = _inproj(
                x2, gmix, pk['ab_w_in'][j], cos_t, sin_t, prm, j, B, S, tiles['inproj'])
            yf, yb = _scan(r, kk, v, lw0, b0, kd0, lw1, b1, kd1, B, S, tiles['scan'], tiles['scan_b'])
            attn = [_attn(q, B, S, dil) for q, (_, dil) in zip((q0, q1, q2), ATTN_PATTERNS)]
            x2 = _post(x2, yf, yb, bonus, g, attn, vec(prm['rwkv_ln_g'][j]), vec(prm['rwkv_ln_b'][j]),
                       pk['ab_w_out'][j], B, S, tiles['post'])
        else:
            x2 = _pool(x2, gmix, pk['c_w_in'][j], pk['c_w_group'][j], vec(prm['c_scale'][j]), pk['c_w_out'][j],
                       B, S, tiles['pool'])
        x2 = _ffn(x2, p_all, i, pk['ffn'][i], B, S, tiles['ffn_m'], tiles['ffn_f'], final=(i == DEPTH - 1))
    return x2.reshape(B, S, D_MODEL)


_TILES = {'inproj': 256, 'scan': 256, 'scan_b': 2, 'post': 512, 'pool': 512, 'ffn_m': 512, 'ffn_f': 256}


def kernel(x_prompt, x_sample, p_prompt, p_sample, ab_w_in, ab_w_out, rwkv_mu, rwkv_w0, rwkv_w_up, rwkv_a0, rwkv_a_up, rwkv_g_up, rwkv_k_k, rwkv_k_a, rwkv_r_k, rwkv_ln_g, rwkv_ln_b, c_w_in, c_w_group, c_scale, c_w_out, norm_mix_g, norm_ffn_g, norm_ple_g, norm_final_g, ffn_w_up, ffn_conv_w, ffn_conv_b, ffn_w_down, ple_w_proj, ple_w_gate, ple_b_gate):
    prm = {
        'ab_w_in': ab_w_in, 'ab_w_out': ab_w_out, 'rwkv_mu': rwkv_mu, 'rwkv_w0': rwkv_w0,
        'rwkv_w_up': rwkv_w_up, 'rwkv_a0': rwkv_a0, 'rwkv_a_up': rwkv_a_up, 'rwkv_g_up': rwkv_g_up,
        'rwkv_k_k': rwkv_k_k, 'rwkv_k_a': rwkv_k_a, 'rwkv_r_k': rwkv_r_k, 'rwkv_ln_g': rwkv_ln_g,
        'rwkv_ln_b': rwkv_ln_b, 'c_w_in': c_w_in, 'c_w_group': c_w_group, 'c_scale': c_scale,
        'c_w_out': c_w_out, 'norm_mix_g': norm_mix_g, 'norm_ffn_g': norm_ffn_g, 'norm_ple_g': norm_ple_g,
        'norm_final_g': norm_final_g, 'ffn_w_up': ffn_w_up, 'ffn_conv_w': ffn_conv_w,
        'ffn_conv_b': ffn_conv_b, 'ffn_w_down': ffn_w_down, 'ple_w_proj': ple_w_proj,
        'ple_w_gate': ple_w_gate, 'ple_b_gate': ple_b_gate,
    }
    pk = _prepare(prm)
    y_prompt = _trunk(x_prompt, p_prompt, prm, pk, _TILES)
    y_sample = _trunk(x_sample, p_sample, prm, pk, _TILES)
    return (y_prompt, y_sample)
```

```python
import functools
import math

import jax
import jax.numpy as jnp
from jax import lax
from jax.experimental import pallas as pl
from jax.experimental.pallas import tpu as pltpu

D_MODEL = 1024
DEPTH = 4
PLE_DIM = 256
HEAD_DIM = 64
RMS_EPS = 1e-6
RWKV_HEADS = 8
RWKV_W = RWKV_HEADS * HEAD_DIM
DECAY_LORA = 64
ICLR_LORA = 64
GATE_LORA = 128
RWKV_IN = 3 * RWKV_W + 2 * DECAY_LORA + 2 * ICLR_LORA + GATE_LORA
GN_EPS = 64e-5
ATTN_PATTERNS = ((128, 1), (512, 4), (2048, 16))
ATTN_HEADS_PER_GROUP = 4
ATTN_HEADS = ATTN_HEADS_PER_GROUP * len(ATTN_PATTERNS)
ATTN_OUT = ATTN_HEADS_PER_GROUP * HEAD_DIM
ATTN_GROUP_IN = 3 * ATTN_OUT
ATTN_HALF = 64
ROPE_THETA = 500000.0
ROT_DIM = HEAD_DIM // 4
NEG_INF = -1e30
AB_OUT = RWKV_W + ATTN_OUT
POOL_WINDOWS = (2, 4, 8, 16)
POOL_GROUP = 256
POOL_W = len(POOL_WINDOWS) * POOL_GROUP
D_FF = 2816

LANES = 128
SUBLANES = 8
HALO = SUBLANES
CHUNK = 64
MXU_DIM = 256
SCAN_GROUP_LANES = LANES
VMEM_LIMIT = 56 * 1024 * 1024

F32 = jnp.float32
BF16 = jnp.bfloat16


def _dot(a, b):
    return jnp.dot(a, b, preferred_element_type=F32)


def _dot_nt(a, b):
    return lax.dot_general(a, b, (((1,), (1,)), ((), ())), preferred_element_type=F32)


def _split(x):
    hi = x.astype(BF16)
    lo = (x - hi.astype(F32)).astype(BF16)
    return hi, lo


def _rms(x, g):
    return x * lax.rsqrt(jnp.mean(x * x, axis=-1, keepdims=True) + RMS_EPS) * g


def _sigmoid(x):
    return 1.0 / (1.0 + jnp.exp(-x))


def _head_ones():
    r = lax.broadcasted_iota(jnp.int32, (MXU_DIM, MXU_DIM), 0) // HEAD_DIM
    c = lax.broadcasted_iota(jnp.int32, (MXU_DIM, MXU_DIM), 1) // HEAD_DIM
    return jnp.where(r == c, 1.0, 0.0).astype(BF16)


def _head_sum(x, ones):
    outs = []
    for j in range(x.shape[1] // MXU_DIM):
        hi, lo = _split(x[:, j * MXU_DIM:(j + 1) * MXU_DIM])
        outs.append(_dot(hi, ones) + _dot(lo, ones))
    return jnp.concatenate(outs, axis=1)


def _cparams(sem):
    return pltpu.CompilerParams(dimension_semantics=sem, vmem_limit_bytes=VMEM_LIMIT)


def _halo_specs(tm, width, n_rows):
    nb = n_rows // HALO
    step = tm // HALO
    prev = pl.BlockSpec((HALO, width), lambda i: (jnp.maximum(i * step - 1, 0), 0))
    nxt = pl.BlockSpec((HALO, width), lambda i: (jnp.minimum((i + 1) * step, nb - 1), 0))
    return prev, nxt


def _full(shape):
    nd = len(shape)
    return pl.BlockSpec(shape, lambda *_: (0,) * nd)


def _resident(shape):
    nd = len(shape)
    return pl.BlockSpec(shape, lambda *_: (0,) * nd, pipeline_mode=pl.Buffered(1))


def _inproj_kernel(x_ref, xp_ref, xn_ref, g_ref, w_ref, cos_ref, sin_ref,
                   mu_ref, w0_ref, wu_ref, a0_ref, au_ref, gu_ref, kk_ref, ka_ref, rk_ref,
                   r_o, kk_o, v_o, g_o, bonus_o, lw0_o, b0_o, kd0_o, lw1_o, b1_o, kd1_o, q0_ref, q1_ref, q2_ref,
                   scr_ref, *, tm, nts):
    it = pl.program_id(0) % nts
    gm = g_ref[...]
    hm = _rms(x_ref[...], gm)
    hp = jnp.where(it == 0, 0.0, _rms(xp_ref[...], gm))
    hn = jnp.where(it == nts - 1, 0.0, _rms(xn_ref[...], gm))
    h = hm.astype(BF16)
    z_ext = _dot(jnp.concatenate([hp, hm, hn], axis=0).astype(BF16), w_ref[:, :RWKV_IN])
    t_attn = [_dot(h, w_ref[:, RWKV_IN + g * ATTN_GROUP_IN:RWKV_IN + (g + 1) * ATTN_GROUP_IN])
              for g in range(len(ATTN_PATTERNS))]
    mid = slice(HALO, HALO + tm)
    z = z_ext[mid]
    zs = 0.5 * (pltpu.roll(z_ext, 1, 0)[mid] + pltpu.roll(z_ext, tm + 2 * HALO - 1, 0)[mid])
    zz = z + mu_ref[...] * (zs - z)
    W = RWKV_W
    r = zz[:, 0:W]
    k = zz[:, W:2 * W]
    v = zz[:, 2 * W:3 * W]
    o = 3 * W
    wd = jnp.tanh(zz[:, o:o + 2 * DECAY_LORA])
    o += 2 * DECAY_LORA
    ad = zz[:, o:o + 2 * ICLR_LORA]
    o += 2 * ICLR_LORA
    gd = _sigmoid(zz[:, o:])
    wlog = w0_ref[...] + _dot(wd.astype(BF16), wu_ref[...])
    aa = _sigmoid(a0_ref[...] + _dot(ad.astype(BF16), au_ref[...]))
    g_o[...] = _dot(gd.astype(BF16), gu_ref[...])
    ones = _head_ones()
    kkv = k * kk_ref[...]
    kkn = kkv / jnp.sqrt(_head_sum(kkv * kkv, ones) + 1e-12)
    r_o[...] = r
    kk_o[...] = kkn
    v_o[...] = v
    ka = ka_ref[...]
    kd_sum = None
    for d, (lw_o, b_o, kd_o) in enumerate(((lw0_o, b0_o, kd0_o), (lw1_o, b1_o, kd1_o))):
        a_d = aa[:, d * W:(d + 1) * W]
        lw_o[...] = -math.exp(-0.5) * _sigmoid(wlog[:, d * W:(d + 1) * W])
        kd = k * (1.0 + (a_d - 1.0) * ka)
        kd_o[...] = kd
        b_o[...] = kkn * a_d
        kd_sum = kd if kd_sum is None else kd_sum + kd
    bonus_o[...] = _head_sum(r * kd_sum * rk_ref[...], ones) * v
    cs = cos_ref[...]
    sn = sin_ref[...]
    lane = lax.broadcasted_iota(jnp.int32, (tm, LANES), 1)
    first = (lane & (HEAD_DIM - 1)) < (ROT_DIM // 2)
    outs = (q0_ref, q1_ref, q2_ref)
    for g, (_, dil) in enumerate(ATTN_PATTERNS):
        t = t_attn[g]
        for c in range(ATTN_GROUP_IN // LANES):
            tc = t[:, c * LANES:(c + 1) * LANES]
            if c < 2 * ATTN_OUT // LANES:
                rot = jnp.where(first, pltpu.roll(tc, LANES - ROT_DIM // 2, 1), pltpu.roll(tc, ROT_DIM // 2, 1))
                tc = tc * cs + rot * sn
            if dil == 1:
                outs[g][0, :, c * LANES:(c + 1) * LANES] = tc
            else:
                scr_ref[c] = tc
        if dil > 1:
            for c in range(ATTN_GROUP_IN // LANES):
                for r in range(dil):
                    outs[g][r, :, c * LANES:(c + 1) * LANES] = scr_ref[c, pl.ds(r, tm // dil, stride=dil), :]


def _inproj(x2, g, w, cos_t, sin_t, prm, j, B, S, tm):
    T = B * S
    nts = S // tm
    W = RWKV_W
    vec = lambda a: a.reshape(1, -1)

    def bdiag(m):
        L = m.shape[1]
        return jnp.zeros((2 * L, 2 * W), F32).at[:L, :W].set(m[0]).at[L:, W:].set(m[1]).astype(BF16)

    params = [vec(prm['rwkv_mu'][j]), vec(prm['rwkv_w0'][j]), bdiag(prm['rwkv_w_up'][j]),
              vec(prm['rwkv_a0'][j]), bdiag(prm['rwkv_a_up'][j]), prm['rwkv_g_up'][j].astype(BF16),
              vec(prm['rwkv_k_k'][j]), vec(prm['rwkv_k_a'][j]), vec(prm['rwkv_r_k'][j])]
    prev, nxt = _halo_specs(tm, D_MODEL, T)
    row = pl.BlockSpec((tm, W), lambda i: (i, 0))
    out_shape = [jax.ShapeDtypeStruct((T, W), F32)] * 11
    out_specs = [row] * 11
    for _, dil in ATTN_PATTERNS:
        out_shape.append(jax.ShapeDtypeStruct((B, dil, S // dil, ATTN_GROUP_IN), F32))
        out_specs.append(pl.BlockSpec((None, dil, tm // dil, ATTN_GROUP_IN),
                                      lambda i: (i // nts, 0, i % nts, 0)))
    return pl.pallas_call(
        functools.partial(_inproj_kernel, tm=tm, nts=nts),
        grid=(T // tm,),
        in_specs=[pl.BlockSpec((tm, D_MODEL), lambda i: (i, 0)), prev, nxt,
                  _resident((1, D_MODEL)),
                  _resident(w.shape),
                  pl.BlockSpec((tm, LANES), lambda i: (i % nts, 0)),
                  pl.BlockSpec((tm, LANES), lambda i: (i % nts, 0))] + [_resident(a.shape) for a in params],
        out_specs=out_specs,
        out_shape=out_shape,
        scratch_shapes=[pltpu.VMEM((ATTN_GROUP_IN // LANES, tm, LANES), F32)],
        compiler_params=_cparams(("parallel",)),
        name="inproj",
    )(x2, x2, x2, g, w, cos_t, sin_t, *params)


def _stack2(x):
    head = lax.broadcasted_iota(jnp.int32, x.shape, 1) // HEAD_DIM
    zero = jnp.zeros_like(x)
    return jnp.concatenate([jnp.where(head == h, x, zero) for h in range(x.shape[1] // HEAD_DIM)], axis=0)


def _pm(m, x):
    return _dot(m.astype(BF16), _stack2(x.astype(BF16)))


def _wkv_scaled(r, kk, v, lw, b, kd, fwd):
    C = CHUNK
    ti = lax.broadcasted_iota(jnp.int32, (C, C), 0)
    tj = lax.broadcasted_iota(jnp.int32, (C, C), 1)
    cum = jnp.where((tj <= ti) if fwd else (tj >= ti), 1.0, 0.0).astype(BF16)
    lh, ll = _split(lw)
    G = _dot(cum, lh) + _dot(cum, ll)
    g_tot = G[C - 1:C, :] if fwd else G[0:1, :]
    e_neg = jnp.exp(-G)
    rt = r * jnp.exp(G)
    at = -kk * jnp.exp(G - lw)
    kt = kd * e_neg
    bt = b * e_neg
    e_tot = jnp.exp(g_tot)
    kb = jnp.concatenate([kt * e_tot, bt * e_tot], axis=0)
    return dict(at=at, rt=rt, kt=kt, bt=bt, kb=kb, v=v, e_tot=e_tot, fwd=fwd)


def _wkv_pairs(chains):
    C = CHUNK
    hw = chains[0]['at'].shape[1]
    gw = hw // HEAD_DIM * C
    trow = lax.broadcasted_iota(jnp.int32, (C, gw), 0)
    tcol = lax.broadcasted_iota(jnp.int32, (C, gw), 1) & (C - 1)
    eye = jnp.where(tcol == trow, 1.0, 0.0)
    hr = lax.broadcasted_iota(jnp.int32, (hw, hw), 0) // HEAD_DIM
    hc = lax.broadcasted_iota(jnp.int32, (hw, hw), 1) // HEAD_DIM
    for ch in chains:
        ch['x'] = jnp.concatenate([ch['at'], ch['rt']], axis=0).astype(BF16)
    for ch in chains:
        kb2 = jnp.concatenate([_stack2(ch['kt'].astype(BF16)), _stack2(ch['bt'].astype(BF16))], axis=0)
        sc = _dot_nt(ch['x'], kb2)
        ch['sc_k'] = sc[:, :gw]
        ch['sc_b'] = sc[:, gw:]
        ch['p1'] = _dot_nt(ch['x'], ch['S'].astype(BF16))
    for ch in chains:
        strict = (tcol < trow) if ch['fwd'] else (tcol > trow)
        incl = (tcol <= trow) if ch['fwd'] else (tcol >= trow)
        ch['a_kk'] = jnp.concatenate([jnp.where(strict, ch['sc_k'][:C], 0.0),
                                      jnp.where(incl, ch['sc_k'][C:], 0.0)], axis=0)
        ch['n'] = jnp.where(strict, ch['sc_b'][:C], 0.0)
        ch['a_rb'] = jnp.where(incl, ch['sc_b'][C:], 0.0)
        ch['t'] = eye + ch['n']
    for j in range(6):
        for ch in chains:
            n = ch['n']
            if j == 0:
                ch['n'] = _pm(n, n)
                ch['p2'] = _pm(ch['a_kk'], ch['v'])
            elif j < 5:
                m = _pm(jnp.concatenate([ch['t'], n], axis=0), n)
                ch['t'] = ch['t'] + m[:C]
                ch['n'] = m[C:]
            else:
                ch['t'] = ch['t'] + _pm(ch['t'], n)
    for ch in chains:
        ch['u'] = _pm(ch['t'], ch['p1'][:C] + ch['p2'][:C])
    for ch in chains:
        ch['y'] = ch['p1'][C:] + ch['p2'][C:] + _pm(ch['a_rb'], ch['u'])
        vu_t = jnp.concatenate([ch['v'], ch['u']], axis=0).T
        upd = _dot(vu_t.astype(BF16), ch['kb'].astype(BF16))
        ch['s_new'] = ch['S'] * ch['e_tot'] + jnp.where(hr == hc, upd, 0.0)


def _scan_kernel(rf, kkf, vf, lwf, bf, kdf, rb, kkb, vb, lwb, bb, kdb, yf_o, yb_o, sf_ref, sb_ref, *, nc, nb):
    @pl.when(pl.program_id(1) == 0)
    def _():
        sf_ref[...] = jnp.zeros_like(sf_ref)
        sb_ref[...] = jnp.zeros_like(sb_ref)

    npairs = RWKV_W // SCAN_GROUP_LANES

    def body(c, carry):
        slf = pl.ds(pl.multiple_of(c * CHUNK, CHUNK), CHUNK)
        slb = pl.ds(pl.multiple_of((nc - 1 - c) * CHUNK, CHUNK), CHUNK)
        chains = []
        for bi in range(nb):
            dirs = (_wkv_scaled(*[ref[bi, slf, :] for ref in (rf, kkf, vf, lwf, bf, kdf)], True),
                    _wkv_scaled(*[ref[bi, slb, :] for ref in (rb, kkb, vb, lwb, bb, kdb)], False))
            for d, s_ref in zip(dirs, (sf_ref, sb_ref)):
                for p in range(npairs):
                    sl = slice(p * SCAN_GROUP_LANES, (p + 1) * SCAN_GROUP_LANES)
                    ch = {k: (val if k == 'fwd' else val[:, sl]) for k, val in d.items()}
                    ch['S'] = s_ref[bi * npairs + p]
                    chains.append(ch)
        _wkv_pairs(chains)
        chains = iter(chains)
        for bi in range(nb):
            for s_ref, y_o, sl in ((sf_ref, yf_o, slf), (sb_ref, yb_o, slb)):
                ys = []
                for p in range(npairs):
                    ch = next(chains)
                    s_ref[bi * npairs + p] = ch['s_new']
                    ys.append(ch['y'])
                y_o[bi, sl, :] = jnp.concatenate(ys, axis=1)
        return carry

    lax.fori_loop(0, nc, body, 0)


def _scan(r, kk, v, lw0, b0, kd0, lw1, b1, kd1, B, S, tb, nb):
    assert B % nb == 0 and S % tb == 0 and tb % CHUNK == 0
    nblk = S // tb
    shp = (B, S, RWKV_W)
    args = [a.reshape(shp) for a in (r, kk, v, lw0, b0, kd0, r, kk, v, lw1, b1, kd1)]
    fwd = pl.BlockSpec((nb, tb, RWKV_W), lambda bi, j: (bi, j, 0))
    bwd = pl.BlockSpec((nb, tb, RWKV_W), lambda bi, j: (bi, nblk - 1 - j, 0))
    yf, yb = pl.pallas_call(
        functools.partial(_scan_kernel, nc=tb // CHUNK, nb=nb),
        grid=(B // nb, nblk),
        in_specs=[fwd] * 6 + [bwd] * 6,
        out_specs=[fwd, bwd],
        out_shape=[jax.ShapeDtypeStruct(shp, F32)] * 2,
        scratch_shapes=[pltpu.VMEM((nb * RWKV_W // SCAN_GROUP_LANES, SCAN_GROUP_LANES, SCAN_GROUP_LANES), F32)] * 2,
        compiler_params=_cparams(("parallel", "arbitrary")),
        name="wkv_scan",
    )(*args)
    return yf.reshape(B * S, RWKV_W), yb.reshape(B * S, RWKV_W)


def _attn_kernel(band_ref, q_ref, km_ref, vm_ref, kp_ref, vp_ref, kn_ref, vn_ref, o_ref, lse_ref, *, Q, L, rpb):
    j = pl.program_id(2)
    Wn = Q + 2 * ATTN_HALF
    kpos = j * Q - ATTN_HALF + lax.broadcasted_iota(jnp.int32, (1, Wn), 1)
    bias = band_ref[...] + jnp.where(jnp.minimum(kpos, L - 1 - kpos) >= 0, 0.0, NEG_INF)
    lane = lax.broadcasted_iota(jnp.int32, (Q, ATTN_OUT), 1) // HEAD_DIM

    def residue(r, carry):
        q = q_ref[r] * (HEAD_DIM ** -0.5)
        kw = jnp.concatenate([kp_ref[r], km_ref[r], kn_ref[r]], axis=0).astype(BF16)
        vw = jnp.concatenate([vp_ref[r], vm_ref[r], vn_ref[r]], axis=0).astype(BF16)
        heads = range(ATTN_HEADS_PER_GROUP)
        ss = [_dot_nt(jnp.where(lane == hh, q, 0.0).astype(BF16), kw) + bias for hh in heads]
        ms = [jnp.max(s, axis=-1, keepdims=True) for s in ss]
        ps = [jnp.exp(s - m) for s, m in zip(ss, ms)]
        dens = [jnp.sum(p, axis=-1, keepdims=True) for p in ps]
        ohs = [_dot(p.astype(BF16), vw) for p in ps]
        o_acc = jnp.zeros((Q, ATTN_OUT), F32)
        l_acc = jnp.zeros((Q, ATTN_OUT), F32)
        for hh in heads:
            o_acc = jnp.where(lane == hh, ohs[hh] / dens[hh], o_acc)
            l_acc = jnp.where(lane == hh, ms[hh] + jnp.log(dens[hh]), l_acc)
        o_ref[r] = o_acc
        lse_ref[r] = l_acc
        return carry

    lax.fori_loop(0, rpb, residue, 0)


def _attn(qkv, B, S, dil):
    L = S // dil
    Q = min(256, L)
    rpb = min(dil, 2048 // Q)
    nq = L // Q
    qb = Q // ATTN_HALF
    nhb = L // ATTN_HALF
    Wn = Q + 2 * ATTN_HALF
    rel = jnp.arange(Wn)[None, :] - ATTN_HALF - jnp.arange(Q)[:, None]
    band = jnp.where(jnp.abs(rel) <= ATTN_HALF, 0.0, NEG_INF).astype(F32)

    def mid(col):
        return pl.BlockSpec((None, rpb, Q, ATTN_OUT), lambda bi, rb, j: (bi, rb, j, col))

    def prev(col):
        return pl.BlockSpec((None, rpb, ATTN_HALF, ATTN_OUT),
                            lambda bi, rb, j: (bi, rb, jnp.maximum(j * qb - 1, 0), col))

    def nxt(col):
        return pl.BlockSpec((None, rpb, ATTN_HALF, ATTN_OUT),
                            lambda bi, rb, j: (bi, rb, jnp.minimum((j + 1) * qb, nhb - 1), col))

    out = pl.BlockSpec((None, rpb, Q, ATTN_OUT), lambda bi, rb, j: (bi, rb, j, 0))
    return pl.pallas_call(
        functools.partial(_attn_kernel, Q=Q, L=L, rpb=rpb),
        grid=(B, dil // rpb, nq),
        in_specs=[_resident(band.shape), mid(0), mid(1), mid(2), prev(1), prev(2), nxt(1), nxt(2)],
        out_specs=[out, out],
        out_shape=[jax.ShapeDtypeStruct((B, dil, L, ATTN_OUT), F32)] * 2,
        compiler_params=_cparams(("parallel", "parallel", "parallel")),
        name="band_attn_d%d" % dil,
    )(band, qkv, qkv, qkv, qkv, qkv, qkv, qkv)


def _post_kernel(x_ref, yf_ref, yb_ref, bonus_ref, g_ref, o0, l0, o1, l1, o2, l2, lng_ref, lnb_ref, wo_ref,
                 out_ref, oscr, lscr, *, tm):
    ones = _head_ones()
    y = yf_ref[...] + yb_ref[...]
    yc = y - _head_sum(y, ones) * (1.0 / HEAD_DIM)
    var = _head_sum(yc * yc, ones) * (1.0 / HEAD_DIM)
    yn = yc * lax.rsqrt(var + GN_EPS) * lng_ref[...] + lnb_ref[...]
    ya = (yn + bonus_ref[...]) * g_ref[...]
    nlb = ATTN_OUT // LANES
    os_, ls = [], []
    for gi, (o_r, l_r, (_, dil)) in enumerate(zip((o0, o1, o2), (l0, l1, l2), ATTN_PATTERNS)):
        if dil == 1:
            os_.append(o_r[0])
            ls.append(l_r[0])
            continue
        for c in range(nlb):
            for r in range(dil):
                oscr[gi * nlb + c, pl.ds(r, tm // dil, stride=dil), :] = o_r[r, :, c * LANES:(c + 1) * LANES]
                lscr[gi * nlb + c, pl.ds(r, tm // dil, stride=dil), :] = l_r[r, :, c * LANES:(c + 1) * LANES]
        os_.append(jnp.concatenate([oscr[gi * nlb + c] for c in range(nlb)], axis=1))
        ls.append(jnp.concatenate([lscr[gi * nlb + c] for c in range(nlb)], axis=1))
    m = jnp.maximum(jnp.maximum(ls[0], ls[1]), ls[2])
    es = [jnp.exp(l - m) for l in ls]
    num = es[0] * os_[0] + es[1] * os_[1] + es[2] * os_[2]
    yb = num / (es[0] + es[1] + es[2])
    out_ref[...] = (x_ref[...] + _dot(ya.astype(BF16), wo_ref[:RWKV_W, :])
                    + _dot(yb.astype(BF16), wo_ref[RWKV_W:, :]))


def _post(x2, yf, yb, bonus, g, attn, lng, lnb, wo, B, S, tm):
    T = B * S
    nts = S // tm
    row = lambda w: pl.BlockSpec((tm, w), lambda i: (i, 0))
    in_specs = [row(D_MODEL)] + [row(RWKV_W)] * 4
    args = [x2, yf, yb, bonus, g]
    for (o, l), (_, dil) in zip(attn, ATTN_PATTERNS):
        spec = pl.BlockSpec((None, dil, tm // dil, ATTN_OUT), lambda i: (i // nts, 0, i % nts, 0))
        in_specs += [spec, spec]
        args += [o, l]
    in_specs += [_full(lng.shape), _full(lnb.shape), _full(wo.shape)]
    args += [lng, lnb, wo]
    ng = len(ATTN_PATTERNS)
    return pl.pallas_call(
        functools.partial(_post_kernel, tm=tm),
        grid=(T // tm,),
        in_specs=in_specs,
        out_specs=row(D_MODEL),
        out_shape=jax.ShapeDtypeStruct((T, D_MODEL), F32),
        scratch_shapes=[pltpu.VMEM((ng * ATTN_OUT // LANES, tm, LANES), F32)] * 2,
        compiler_params=_cparams(("parallel",)),
        name="mixer_ab_out",
    )(*args)


def _window_sum(ue, rad):
    n = ue.shape[0]
    mid = slice(HALO, n - HALO)
    up = lambda a, k: pltpu.roll(a, n - k, 0)
    down = lambda a, k: pltpu.roll(a, k, 0)
    if rad < 4:
        ws = ue[mid]
        for d in range(1, rad + 1):
            ws = ws + (down(ue, d)[mid] + up(ue, d)[mid])
        return ws
    run, width = ue, 1
    while width < 2 * rad:
        run = run + up(run, width)
        width *= 2
    return down(run, rad)[mid] + up(ue, rad)[mid]


def _pool_kernel(x_ref, xp_ref, xn_ref, g_ref, win_ref, wg_ref, sc_ref, wo_ref, out_ref, *, tm, nts, S):
    it = pl.program_id(0) % nts
    x = x_ref[...]
    g = g_ref[...]
    hp = jnp.where(it == 0, 0.0, _rms(xp_ref[...], g))
    hn = jnp.where(it == nts - 1, 0.0, _rms(xn_ref[...], g))
    h = jnp.concatenate([hp, _rms(x, g), hn], axis=0).astype(BF16)
    u = _dot(h, win_ref[...])
    pos = it * tm + lax.broadcasted_iota(jnp.int32, (tm, 1), 0)
    acc = x
    for gi, win in enumerate(POOL_WINDOWS):
        rad = win // 2
        cols = slice(gi * POOL_GROUP, (gi + 1) * POOL_GROUP)
        ue = u[:, cols]
        cnt = (jnp.minimum(pos + rad + 1, S) - jnp.maximum(pos - rad, 0)).astype(F32)
        dlt = (_window_sum(ue, rad) / cnt - ue[HALO:HALO + tm]).astype(BF16)
        yg = _dot(dlt, wg_ref[gi]) * sc_ref[:, cols]
        acc = acc + _dot(yg.astype(BF16), wo_ref[cols, :])
    out_ref[...] = acc


def _pool(x2, g, w_in, w_group, scale, w_out, B, S, tm):
    T = B * S
    nts = S // tm
    prev, nxt = _halo_specs(tm, D_MODEL, T)
    row = pl.BlockSpec((tm, D_MODEL), lambda i: (i, 0))
    return pl.pallas_call(
        functools.partial(_pool_kernel, tm=tm, nts=nts, S=S),
        grid=(T // tm,),
        in_specs=[row, prev, nxt, _full(g.shape), _full(w_in.shape), _full(w_group.shape), _full(scale.shape),
                  _full(w_out.shape)],
        out_specs=row,
        out_shape=jax.ShapeDtypeStruct((T, D_MODEL), F32),
        compiler_params=_cparams(("parallel",)),
        name="pool_mixer",
    )(x2, x2, x2, g, w_in, w_group, scale, w_out)


def _gelu(x):
    c = math.sqrt(2.0 / math.pi)
    hx = 0.5 * x
    return hx + hx * jnp.tanh(x * (c + (c * 0.044715) * (x * x)))


def _ffn_kernel(x_ref, xp_ref, xn_ref, p_ref, gf_ref, wup_ref, cw_ref, cb_ref, wd_ref,
                gp_ref, wpg_ref, bpg_ref, wpp_ref, gfin_ref, out_ref, *, tm, rb, nts, final):
    it = pl.program_id(0) % nts
    g = gf_ref[...]
    n_ext = rb + 2 * HALO
    nsub = tm // rb
    for s in range(nsub):
        r0 = s * rb
        top = jnp.where(it == 0, 0.0, xp_ref[...]) if s == 0 else x_ref[pl.ds(r0 - HALO, HALO), :]
        bot = jnp.where(it == nts - 1, 0.0, xn_ref[...]) if s == nsub - 1 else x_ref[pl.ds(r0 + rb, HALO), :]
        h = _rms(jnp.concatenate([top, x_ref[pl.ds(r0, rb), :], bot], axis=0), g).astype(BF16)
        u = _dot(h, wup_ref[...])
        mid = slice(HALO, HALO + rb)
        c = (pltpu.roll(u, 1, 0)[mid] * cw_ref[0:1, :] + u[mid] * cw_ref[1:2, :]
             + pltpu.roll(u, n_ext - 1, 0)[mid] * cw_ref[2:3, :] + cb_ref[...])
        act = _gelu(c[:, :D_FF]) * c[:, D_FF:]
        rows = pl.ds(r0, rb)
        x2 = x_ref[rows, :] + _dot(act.astype(BF16), wd_ref[...])
        h3 = _rms(x2, gp_ref[...]).astype(BF16)
        gate = _sigmoid(_dot(h3, wpg_ref[...]) + bpg_ref[...])
        x3 = x2 + gate * _dot(p_ref[rows, :].astype(BF16), wpp_ref[...])
        out_ref[rows, :] = _rms(x3, gfin_ref[...]) if final else x3


def _ffn(x2, p_all, layer, w, B, S, tm, rb, final):
    T = B * S
    nt = T // tm
    nts = S // tm
    prev, nxt = _halo_specs(tm, D_MODEL, T)
    row = pl.BlockSpec((tm, D_MODEL), lambda i: (i, 0))
    names = ('gf', 'wup', 'cw', 'cb', 'wd', 'gp', 'wpg', 'bpg', 'wpp', 'gfin')
    in_specs = [row, prev, nxt, pl.BlockSpec((tm, PLE_DIM), lambda i: (layer * nt + i, 0))]
    in_specs += [_resident(w[k].shape) for k in names]
    return pl.pallas_call(
        functools.partial(_ffn_kernel, tm=tm, rb=rb, nts=nts, final=final),
        grid=(nt,),
        in_specs=in_specs,
        out_specs=row,
        out_shape=jax.ShapeDtypeStruct((T, D_MODEL), F32),
        compiler_params=_cparams(("parallel",)),
        name="convffn_ple",
    )(x2, x2, x2, p_all, *[w[k] for k in names])


def _rope_tables(S):
    half = ROT_DIM // 2
    inv = jnp.float32(ROPE_THETA) ** (-jnp.arange(half, dtype=F32) * 2.0 / ROT_DIM)
    ang = jnp.arange(S, dtype=F32)[:, None] * inv[None, :]
    cos = jnp.cos(ang)
    sin = jnp.sin(ang)
    pad1 = jnp.ones((S, HEAD_DIM - ROT_DIM), F32)
    pad0 = jnp.zeros((S, HEAD_DIM - ROT_DIM), F32)
    cos_h = jnp.concatenate([cos, cos, pad1], axis=1)
    sin_h = jnp.concatenate([-sin, sin, pad0], axis=1)
    reps = LANES // HEAD_DIM
    return jnp.tile(cos_h, (1, reps)), jnp.tile(sin_h, (1, reps))


def _pack_ab_w_in(w):
    parts = [w[:, :RWKV_IN]]
    qkv = [w[:, RWKV_IN + s * ATTN_HEADS * HEAD_DIM: RWKV_IN + (s + 1) * ATTN_HEADS * HEAD_DIM] for s in range(3)]
    for g in range(len(ATTN_PATTERNS)):
        parts += [t[:, g * ATTN_OUT:(g + 1) * ATTN_OUT] for t in qkv]
    return jnp.concatenate(parts, axis=1).astype(BF16)


def _prepare(prm):
    vec = lambda a: a.reshape(1, -1).astype(F32)
    pk = {'ab_w_in': [_pack_ab_w_in(prm['ab_w_in'][j]) for j in range(prm['ab_w_in'].shape[0])],
          'ab_w_out': prm['ab_w_out'].astype(BF16),
          'c_w_in': prm['c_w_in'].astype(BF16), 'c_w_group': prm['c_w_group'].astype(BF16),
          'c_w_out': prm['c_w_out'].astype(BF16), 'ffn': []}
    for i in range(DEPTH):
        pk['ffn'].append({
            'gf': vec(prm['norm_ffn_g'][i]), 'wup': prm['ffn_w_up'][i].astype(BF16),
            'cw': prm['ffn_conv_w'][i], 'cb': vec(prm['ffn_conv_b'][i]),
            'wd': prm['ffn_w_down'][i].astype(BF16), 'gp': vec(prm['norm_ple_g'][i]),
            'wpg': prm['ple_w_gate'][i].astype(BF16), 'bpg': vec(prm['ple_b_gate'][i]),
            'wpp': prm['ple_w_proj'][i].astype(BF16), 'gfin': vec(prm['norm_final_g'])})
    return pk


def _trunk(x, p, prm, pk, tiles):
    B, S, _ = x.shape
    T = B * S
    vec = lambda a: a.reshape(1, -1)
    x2 = x.reshape(T, D_MODEL)
    p_all = p.reshape(p.shape[0] * T, PLE_DIM)
    cos_t, sin_t = _rope_tables(S)
    for i in range(DEPTH):
        j = i // 2
        gmix = vec(prm['norm_mix_g'][i])
        if i % 2 == 0:
            (r, kk, v, g, bonus, lw0, b0, kd0, lw1, b1, kd1, q0, q1, q2) = _inproj(
                x2, gmix, pk['ab_w_in'][j], cos_t, sin_t, prm, j, B, S, tiles['inproj'])
            nb = min(B, tiles['scan_b'])
            yf, yb = _scan(r, kk, v, lw0, b0, kd0, lw1, b1, kd1, B, S, tiles['scan_rows'] // nb, nb)
            attn = [_attn(q, B, S, dil) for q, (_, dil) in zip((q0, q1, q2), ATTN_PATTERNS)]
            x2 = _post(x2, yf, yb, bonus, g, attn, vec(prm['rwkv_ln_g'][j]), vec(prm['rwkv_ln_b'][j]),
                       pk['ab_w_out'][j], B, S, tiles['post'])
        else:
            x2 = _pool(x2, gmix, pk['c_w_in'][j], pk['c_w_group'][j], vec(prm['c_scale'][j]), pk['c_w_out'][j],
                       B, S, tiles['pool'])
        x2 = _ffn(x2, p_all, i, pk['ffn'][i], B, S, tiles['ffn_m'], tiles['ffn_f'], final=(i == DEPTH - 1))
    return x2.reshape(B, S, D_MODEL)


_TILES = {'inproj': 256, 'scan_rows': 512, 'scan_b': 4, 'post': 512, 'pool': 512, 'ffn_m': 512, 'ffn_f': 256}


def kernel(x_prompt, x_sample, p_prompt, p_sample, ab_w_in, ab_w_out, rwkv_mu, rwkv_w0, rwkv_w_up, rwkv_a0, rwkv_a_up, rwkv_g_up, rwkv_k_k, rwkv_k_a, rwkv_r_k, rwkv_ln_g, rwkv_ln_b, c_w_in, c_w_group, c_scale, c_w_out, norm_mix_g, norm_ffn_g, norm_ple_g, norm_final_g, ffn_w_up, ffn_conv_w, ffn_conv_b, ffn_w_down, ple_w_proj, ple_w_gate, ple_b_gate):
    prm = {
        'ab_w_in': ab_w_in, 'ab_w_out': ab_w_out, 'rwkv_mu': rwkv_mu, 'rwkv_w0': rwkv_w0,
        'rwkv_w_up': rwkv_w_up, 'rwkv_a0': rwkv_a0, 'rwkv_a_up': rwkv_a_up, 'rwkv_g_up': rwkv_g_up,
        'rwkv_k_k': rwkv_k_k, 'rwkv_k_a': rwkv_k_a, 'rwkv_r_k': rwkv_r_k, 'rwkv_ln_g': rwkv_ln_g,
        'rwkv_ln_b': rwkv_ln_b, 'c_w_in': c_w_in, 'c_w_group': c_w_group, 'c_scale': c_scale,
        'c_w_out': c_w_out, 'norm_mix_g': norm_mix_g, 'norm_ffn_g': norm_ffn_g, 'norm_ple_g': norm_ple_g,
        'norm_final_g': norm_final_g, 'ffn_w_up': ffn_w_up, 'ffn_conv_w': ffn_conv_w,
        'ffn_conv_b': ffn_conv_b, 'ffn_w_down': ffn_w_down, 'ple_w_proj': ple_w_proj,
        'ple_w_gate': ple_w_gate, 'ple_b_gate': ple_b_gate,
    }
    pk = _prepare(prm)
    y_prompt = _trunk(x_prompt, p_prompt, prm, pk, _TILES)
    y_sample = _trunk(x_sample, p_sample, prm, pk, _TILES)
    return (y_prompt, y_sample)
```

```python
import functools
import math

import jax
import jax.numpy as jnp
from jax import lax
from jax.experimental import pallas as pl
from jax.experimental.pallas import tpu as pltpu

D_MODEL = 1024
DEPTH = 4
PLE_DIM = 256
HEAD_DIM = 64
RMS_EPS = 1e-6
RWKV_HEADS = 8
RWKV_W = RWKV_HEADS * HEAD_DIM
DECAY_LORA = 64
ICLR_LORA = 64
GATE_LORA = 128
RWKV_IN = 3 * RWKV_W + 2 * DECAY_LORA + 2 * ICLR_LORA + GATE_LORA
GN_EPS = 64e-5
ATTN_PATTERNS = ((128, 1), (512, 4), (2048, 16))
ATTN_HEADS_PER_GROUP = 4
ATTN_HEADS = ATTN_HEADS_PER_GROUP * len(ATTN_PATTERNS)
ATTN_OUT = ATTN_HEADS_PER_GROUP * HEAD_DIM
ATTN_GROUP_IN = 3 * ATTN_OUT
ATTN_HALF = 64
ATTN_Q_ROWS = 256
ATTN_STEP_TOKENS = 2048
ROPE_THETA = 500000.0
ROT_DIM = HEAD_DIM // 4
NEG_INF = -1e30
AB_OUT = RWKV_W + ATTN_OUT
POOL_WINDOWS = (2, 4, 8, 16)
POOL_GROUP = 256
POOL_W = len(POOL_WINDOWS) * POOL_GROUP
D_FF = 2816

LANES = 128
SUBLANES = 8
HALO = SUBLANES
CHUNK = 64
MXU_DIM = 256
SCAN_GROUP_LANES = LANES
VMEM_BYTES = 64 * 1024 * 1024
VMEM_LIMIT = VMEM_BYTES // 8 * 7
assert all(win // (2 * dil) == ATTN_HALF for win, dil in ATTN_PATTERNS)
assert HALO >= max(POOL_WINDOWS) // 2

F32 = jnp.float32
BF16 = jnp.bfloat16


def _dot(a, b):
    return jnp.dot(a, b, preferred_element_type=F32)


def _dot_nt(a, b):
    return lax.dot_general(a, b, (((1,), (1,)), ((), ())), preferred_element_type=F32)


def _split(x):
    hi = x.astype(BF16)
    lo = (x - hi.astype(F32)).astype(BF16)
    return hi, lo


def _rms(x, g):
    return x * lax.rsqrt(jnp.mean(x * x, axis=-1, keepdims=True) + RMS_EPS) * g


def _sigmoid(x):
    return 1.0 / (1.0 + jnp.exp(-x))


def _head_ones():
    r = lax.broadcasted_iota(jnp.int32, (MXU_DIM, MXU_DIM), 0) // HEAD_DIM
    c = lax.broadcasted_iota(jnp.int32, (MXU_DIM, MXU_DIM), 1) // HEAD_DIM
    return jnp.where(r == c, 1.0, 0.0).astype(BF16)


def _head_sum(x, ones):
    outs = []
    for j in range(x.shape[1] // MXU_DIM):
        hi, lo = _split(x[:, j * MXU_DIM:(j + 1) * MXU_DIM])
        outs.append(_dot(hi, ones) + _dot(lo, ones))
    return jnp.concatenate(outs, axis=1)


def _cparams(sem):
    return pltpu.CompilerParams(dimension_semantics=sem, vmem_limit_bytes=VMEM_LIMIT)


def _halo_specs(tm, width, n_rows):
    nb = n_rows // HALO
    step = tm // HALO
    prev = pl.BlockSpec((HALO, width), lambda i: (jnp.maximum(i * step - 1, 0), 0))
    nxt = pl.BlockSpec((HALO, width), lambda i: (jnp.minimum((i + 1) * step, nb - 1), 0))
    return prev, nxt


def _full(shape):
    nd = len(shape)
    return pl.BlockSpec(shape, lambda *_: (0,) * nd)


def _resident(shape):
    nd = len(shape)
    return pl.BlockSpec(shape, lambda *_: (0,) * nd, pipeline_mode=pl.Buffered(1))


def _inproj_kernel(x_ref, xp_ref, xn_ref, g_ref, w_ref, cos_ref, sin_ref,
                   mu_ref, w0_ref, wu_ref, a0_ref, au_ref, gu_ref, kk_ref, ka_ref, rk_ref,
                   r_o, kk_o, v_o, g_o, bonus_o, lw0_o, b0_o, kd0_o, lw1_o, b1_o, kd1_o, q0_ref, q1_ref, q2_ref,
                   scr_ref, *, tm, nts):
    it = pl.program_id(0) % nts
    gm = g_ref[...]
    hm = _rms(x_ref[...], gm)
    hp = jnp.where(it == 0, 0.0, _rms(xp_ref[...], gm))
    hn = jnp.where(it == nts - 1, 0.0, _rms(xn_ref[...], gm))
    h = hm.astype(BF16)
    z_ext = _dot(jnp.concatenate([hp, hm, hn], axis=0).astype(BF16), w_ref[:, :RWKV_IN])
    t_attn = [_dot(h, w_ref[:, RWKV_IN + g * ATTN_GROUP_IN:RWKV_IN + (g + 1) * ATTN_GROUP_IN])
              for g in range(len(ATTN_PATTERNS))]
    mid = slice(HALO, HALO + tm)
    z = z_ext[mid]
    zs = 0.5 * (pltpu.roll(z_ext, 1, 0)[mid] + pltpu.roll(z_ext, tm + 2 * HALO - 1, 0)[mid])
    zz = z + mu_ref[...] * (zs - z)
    W = RWKV_W
    r = zz[:, 0:W]
    k = zz[:, W:2 * W]
    v = zz[:, 2 * W:3 * W]
    o = 3 * W
    wd = jnp.tanh(zz[:, o:o + 2 * DECAY_LORA])
    o += 2 * DECAY_LORA
    ad = zz[:, o:o + 2 * ICLR_LORA]
    o += 2 * ICLR_LORA
    gd = _sigmoid(zz[:, o:])
    wlog = w0_ref[...] + _dot(wd.astype(BF16), wu_ref[...])
    aa = _sigmoid(a0_ref[...] + _dot(ad.astype(BF16), au_ref[...]))
    g_o[...] = _dot(gd.astype(BF16), gu_ref[...])
    ones = _head_ones()
    kkv = k * kk_ref[...]
    kkn = kkv / jnp.sqrt(_head_sum(kkv * kkv, ones) + 1e-12)
    r_o[...] = r
    kk_o[...] = kkn
    v_o[...] = v
    ka = ka_ref[...]
    kd_sum = None
    for d, (lw_o, b_o, kd_o) in enumerate(((lw0_o, b0_o, kd0_o), (lw1_o, b1_o, kd1_o))):
        a_d = aa[:, d * W:(d + 1) * W]
        lw_o[...] = -math.exp(-0.5) * _sigmoid(wlog[:, d * W:(d + 1) * W])
        kd = k * (1.0 + (a_d - 1.0) * ka)
        kd_o[...] = kd
        b_o[...] = kkn * a_d
        kd_sum = kd if kd_sum is None else kd_sum + kd
    bonus_o[...] = _head_sum(r * kd_sum * rk_ref[...], ones) * v
    cs = cos_ref[...]
    sn = sin_ref[...]
    lane = lax.broadcasted_iota(jnp.int32, (tm, LANES), 1)
    first = (lane & (HEAD_DIM - 1)) < (ROT_DIM // 2)
    outs = (q0_ref, q1_ref, q2_ref)
    for g, (_, dil) in enumerate(ATTN_PATTERNS):
        t = t_attn[g]
        for c in range(ATTN_GROUP_IN // LANES):
            tc = t[:, c * LANES:(c + 1) * LANES]
            if c < 2 * ATTN_OUT // LANES:
                rot = jnp.where(first, pltpu.roll(tc, LANES - ROT_DIM // 2, 1), pltpu.roll(tc, ROT_DIM // 2, 1))
                tc = tc * cs + rot * sn
            if dil == 1:
                outs[g][0, :, c * LANES:(c + 1) * LANES] = tc
            else:
                scr_ref[c] = tc
        if dil > 1:
            for c in range(ATTN_GROUP_IN // LANES):
                for r in range(dil):
                    outs[g][r, :, c * LANES:(c + 1) * LANES] = scr_ref[c, pl.ds(r, tm // dil, stride=dil), :]


def _inproj(x2, g, w, cos_t, sin_t, prm, j, B, S, tm):
    T = B * S
    nts = S // tm
    W = RWKV_W
    vec = lambda a: a.reshape(1, -1)

    def bdiag(m):
        L = m.shape[1]
        return jnp.zeros((2 * L, 2 * W), F32).at[:L, :W].set(m[0]).at[L:, W:].set(m[1]).astype(BF16)

    params = [vec(prm['rwkv_mu'][j]), vec(prm['rwkv_w0'][j]), bdiag(prm['rwkv_w_up'][j]),
              vec(prm['rwkv_a0'][j]), bdiag(prm['rwkv_a_up'][j]), prm['rwkv_g_up'][j].astype(BF16),
              vec(prm['rwkv_k_k'][j]), vec(prm['rwkv_k_a'][j]), vec(prm['rwkv_r_k'][j])]
    prev, nxt = _halo_specs(tm, D_MODEL, T)
    row = pl.BlockSpec((tm, W), lambda i: (i, 0))
    out_shape = [jax.ShapeDtypeStruct((T, W), F32)] * 11
    out_specs = [row] * 11
    for _, dil in ATTN_PATTERNS:
        out_shape.append(jax.ShapeDtypeStruct((B, dil, S // dil, ATTN_GROUP_IN), F32))
        out_specs.append(pl.BlockSpec((None, dil, tm // dil, ATTN_GROUP_IN),
                                      lambda i: (i // nts, 0, i % nts, 0)))
    return pl.pallas_call(
        functools.partial(_inproj_kernel, tm=tm, nts=nts),
        grid=(T // tm,),
        in_specs=[pl.BlockSpec((tm, D_MODEL), lambda i: (i, 0)), prev, nxt,
                  _resident((1, D_MODEL)),
                  _resident(w.shape),
                  pl.BlockSpec((tm, LANES), lambda i: (i % nts, 0)),
                  pl.BlockSpec((tm, LANES), lambda i: (i % nts, 0))] + [_resident(a.shape) for a in params],
        out_specs=out_specs,
        out_shape=out_shape,
        scratch_shapes=[pltpu.VMEM((ATTN_GROUP_IN // LANES, tm, LANES), F32)],
        compiler_params=_cparams(("parallel",)),
        name="inproj",
    )(x2, x2, x2, g, w, cos_t, sin_t, *params)


def _stack2(x):
    head = lax.broadcasted_iota(jnp.int32, x.shape, 1) // HEAD_DIM
    zero = jnp.zeros_like(x)
    return jnp.concatenate([jnp.where(head == h, x, zero) for h in range(x.shape[1] // HEAD_DIM)], axis=0)


def _pm(m, x):
    return _dot(m.astype(BF16), _stack2(x.astype(BF16)))


def _wkv_scaled(r, kk, v, lw, b, kd, fwd):
    C = CHUNK
    ti = lax.broadcasted_iota(jnp.int32, (C, C), 0)
    tj = lax.broadcasted_iota(jnp.int32, (C, C), 1)
    cum = jnp.where((tj <= ti) if fwd else (tj >= ti), 1.0, 0.0).astype(BF16)
    lh, ll = _split(lw)
    G = _dot(cum, lh) + _dot(cum, ll)
    g_tot = G[C - 1:C, :] if fwd else G[0:1, :]
    e_neg = jnp.exp(-G)
    rt = r * jnp.exp(G)
    at = -kk * jnp.exp(G - lw)
    kt = kd * e_neg
    bt = b * e_neg
    e_tot = jnp.exp(g_tot)
    kb = jnp.concatenate([kt * e_tot, bt * e_tot], axis=0)
    return dict(at=at, rt=rt, kt=kt, bt=bt, kb=kb, v=v, e_tot=e_tot, fwd=fwd)


def _wkv_pairs(chains):
    C = CHUNK
    hw = chains[0]['at'].shape[1]
    gw = hw // HEAD_DIM * C
    trow = lax.broadcasted_iota(jnp.int32, (C, gw), 0)
    tcol = lax.broadcasted_iota(jnp.int32, (C, gw), 1) & (C - 1)
    eye = jnp.where(tcol == trow, 1.0, 0.0)
    hr = lax.broadcasted_iota(jnp.int32, (hw, hw), 0) // HEAD_DIM
    hc = lax.broadcasted_iota(jnp.int32, (hw, hw), 1) // HEAD_DIM
    for ch in chains:
        ch['x'] = jnp.concatenate([ch['at'], ch['rt']], axis=0).astype(BF16)
    for ch in chains:
        kb2 = jnp.concatenate([_stack2(ch['kt'].astype(BF16)), _stack2(ch['bt'].astype(BF16))], axis=0)
        sc = _dot_nt(ch['x'], kb2)
        ch['sc_k'] = sc[:, :gw]
        ch['sc_b'] = sc[:, gw:]
        ch['p1'] = _dot_nt(ch['x'], ch['S'].astype(BF16))
    for ch in chains:
        strict = (tcol < trow) if ch['fwd'] else (tcol > trow)
        incl = (tcol <= trow) if ch['fwd'] else (tcol >= trow)
        ch['a_kk'] = jnp.concatenate([jnp.where(strict, ch['sc_k'][:C], 0.0),
                                      jnp.where(incl, ch['sc_k'][C:], 0.0)], axis=0)
        ch['n'] = jnp.where(strict, ch['sc_b'][:C], 0.0)
        ch['a_rb'] = jnp.where(incl, ch['sc_b'][C:], 0.0)
        ch['t'] = eye + ch['n']
    for j in range(6):
        for ch in chains:
            n = ch['n']
            if j == 0:
                ch['n'] = _pm(n, n)
                ch['p2'] = _pm(ch['a_kk'], ch['v'])
            elif j < 5:
                m = _pm(jnp.concatenate([ch['t'], n], axis=0), n)
                ch['t'] = ch['t'] + m[:C]
                ch['n'] = m[C:]
            else:
                ch['t'] = ch['t'] + _pm(ch['t'], n)
    for ch in chains:
        ch['u'] = _pm(ch['t'], ch['p1'][:C] + ch['p2'][:C])
    for ch in chains:
        ch['y'] = ch['p1'][C:] + ch['p2'][C:] + _pm(ch['a_rb'], ch['u'])
        vu_t = jnp.concatenate([ch['v'], ch['u']], axis=0).T
        upd = _dot(vu_t.astype(BF16), ch['kb'].astype(BF16))
        ch['s_new'] = ch['S'] * ch['e_tot'] + jnp.where(hr == hc, upd, 0.0)


def _scan_kernel(rf, kkf, vf, lwf, bf, kdf, rb, kkb, vb, lwb, bb, kdb, yf_o, yb_o, sf_ref, sb_ref, *, nc, nb):
    @pl.when(pl.program_id(1) == 0)
    def _():
        sf_ref[...] = jnp.zeros_like(sf_ref)
        sb_ref[...] = jnp.zeros_like(sb_ref)

    npairs = RWKV_W // SCAN_GROUP_LANES

    def body(c, carry):
        slf = pl.ds(pl.multiple_of(c * CHUNK, CHUNK), CHUNK)
        slb = pl.ds(pl.multiple_of((nc - 1 - c) * CHUNK, CHUNK), CHUNK)
        chains = []
        for bi in range(nb):
            dirs = (_wkv_scaled(*[ref[bi, slf, :] for ref in (rf, kkf, vf, lwf, bf, kdf)], True),
                    _wkv_scaled(*[ref[bi, slb, :] for ref in (rb, kkb, vb, lwb, bb, kdb)], False))
            for d, s_ref in zip(dirs, (sf_ref, sb_ref)):
                for p in range(npairs):
                    sl = slice(p * SCAN_GROUP_LANES, (p + 1) * SCAN_GROUP_LANES)
                    ch = {k: (val if k == 'fwd' else val[:, sl]) for k, val in d.items()}
                    ch['S'] = s_ref[bi * npairs + p]
                    chains.append(ch)
        _wkv_pairs(chains)
        chains = iter(chains)
        for bi in range(nb):
            for s_ref, y_o, sl in ((sf_ref, yf_o, slf), (sb_ref, yb_o, slb)):
                ys = []
                for p in range(npairs):
                    ch = next(chains)
                    s_ref[bi * npairs + p] = ch['s_new']
                    ys.append(ch['y'])
                y_o[bi, sl, :] = jnp.concatenate(ys, axis=1)
        return carry

    lax.fori_loop(0, nc, body, 0)


def _scan(r, kk, v, lw0, b0, kd0, lw1, b1, kd1, B, S, tb, nb):
    assert B % nb == 0 and S % tb == 0 and tb % CHUNK == 0
    nblk = S // tb
    shp = (B, S, RWKV_W)
    args = [a.reshape(shp) for a in (r, kk, v, lw0, b0, kd0, r, kk, v, lw1, b1, kd1)]
    fwd = pl.BlockSpec((nb, tb, RWKV_W), lambda bi, j: (bi, j, 0))
    bwd = pl.BlockSpec((nb, tb, RWKV_W), lambda bi, j: (bi, nblk - 1 - j, 0))
    yf, yb = pl.pallas_call(
        functools.partial(_scan_kernel, nc=tb // CHUNK, nb=nb),
        grid=(B // nb, nblk),
        in_specs=[fwd] * 6 + [bwd] * 6,
        out_specs=[fwd, bwd],
        out_shape=[jax.ShapeDtypeStruct(shp, F32)] * 2,
        scratch_shapes=[pltpu.VMEM((nb * RWKV_W // SCAN_GROUP_LANES, SCAN_GROUP_LANES, SCAN_GROUP_LANES), F32)] * 2,
        compiler_params=_cparams(("parallel", "arbitrary")),
        name="wkv_scan",
    )(*args)
    return yf.reshape(B * S, RWKV_W), yb.reshape(B * S, RWKV_W)


def _attn_kernel(band_ref, q_ref, km_ref, vm_ref, kp_ref, vp_ref, kn_ref, vn_ref, o_ref, lse_ref,
                 *, Q, Qs, L, rpb, ru):
    j = pl.program_id(2)
    Wn = Qs + 2 * ATTN_HALF
    heads = range(ATTN_HEADS_PER_GROUP)
    lane = lax.broadcasted_iota(jnp.int32, (Qs, ATTN_OUT), 1) // HEAD_DIM
    biases = []
    for sb in range(Q // Qs):
        kpos = j * Q + sb * Qs - ATTN_HALF + lax.broadcasted_iota(jnp.int32, (1, Wn), 1)
        biases.append(band_ref[...] + jnp.where(jnp.minimum(kpos, L - 1 - kpos) >= 0, 0.0, NEG_INF))

    def body(it, carry):
        units = []
        for rr in range(ru):
            r = it * ru + rr
            k_all = jnp.concatenate([kp_ref[r], km_ref[r], kn_ref[r]], axis=0).astype(BF16)
            v_all = jnp.concatenate([vp_ref[r], vm_ref[r], vn_ref[r]], axis=0).astype(BF16)
            for sb in range(Q // Qs):
                units.append(dict(r=r, rows=pl.ds(sb * Qs, Qs), bias=biases[sb],
                                  kw=k_all[sb * Qs:sb * Qs + Wn], vw=v_all[sb * Qs:sb * Qs + Wn]))
        for un in units:
            q = q_ref[un['r'], un['rows'], :] * (HEAD_DIM ** -0.5)
            un['s'] = [_dot_nt(jnp.where(lane == hh, q, 0.0).astype(BF16), un['kw']) + un['bias'] for hh in heads]
        for un in units:
            un['m'] = [jnp.max(s, axis=-1, keepdims=True) for s in un['s']]
            un['p'] = [jnp.exp(s - m) for s, m in zip(un['s'], un['m'])]
            un['den'] = [jnp.sum(p, axis=-1, keepdims=True) for p in un['p']]
        for un in units:
            un['o'] = [_dot(p.astype(BF16), un['vw']) for p in un['p']]
        for un in units:
            o_acc = jnp.zeros((Qs, ATTN_OUT), F32)
            l_acc = jnp.zeros((Qs, ATTN_OUT), F32)
            for hh in heads:
                o_acc = jnp.where(lane == hh, un['o'][hh] / un['den'][hh], o_acc)
                l_acc = jnp.where(lane == hh, un['m'][hh] + jnp.log(un['den'][hh]), l_acc)
            o_ref[un['r'], un['rows'], :] = o_acc
            lse_ref[un['r'], un['rows'], :] = l_acc
        return carry

    lax.fori_loop(0, rpb // ru, body, 0)


def _attn(qkv, B, S, dil):
    L = S // dil
    Qs = min(ATTN_Q_ROWS, L)
    nsub, ru = (2, 1) if dil == 1 else (1, 2)
    Q = Qs * nsub
    rpb = min(dil, ATTN_STEP_TOKENS // Q)
    assert L % Q == 0 and dil % rpb == 0 and rpb % ru == 0
    nq = L // Q
    qb = Q // ATTN_HALF
    nhb = L // ATTN_HALF
    Wn = Qs + 2 * ATTN_HALF
    rel = jnp.arange(Wn)[None, :] - ATTN_HALF - jnp.arange(Qs)[:, None]
    band = jnp.where(jnp.abs(rel) <= ATTN_HALF, 0.0, NEG_INF).astype(F32)

    def mid(col):
        return pl.BlockSpec((None, rpb, Q, ATTN_OUT), lambda bi, rb, j: (bi, rb, j, col))

    def prev(col):
        return pl.BlockSpec((None, rpb, ATTN_HALF, ATTN_OUT),
                            lambda bi, rb, j: (bi, rb, jnp.maximum(j * qb - 1, 0), col))

    def nxt(col):
        return pl.BlockSpec((None, rpb, ATTN_HALF, ATTN_OUT),
                            lambda bi, rb, j: (bi, rb, jnp.minimum((j + 1) * qb, nhb - 1), col))

    out = pl.BlockSpec((None, rpb, Q, ATTN_OUT), lambda bi, rb, j: (bi, rb, j, 0))
    return pl.pallas_call(
        functools.partial(_attn_kernel, Q=Q, Qs=Qs, L=L, rpb=rpb, ru=ru),
        grid=(B, dil // rpb, nq),
        in_specs=[_resident(band.shape), mid(0), mid(1), mid(2), prev(1), prev(2), nxt(1), nxt(2)],
        out_specs=[out, out],
        out_shape=[jax.ShapeDtypeStruct((B, dil, L, ATTN_OUT), F32)] * 2,
        compiler_params=_cparams(("parallel", "parallel", "parallel")),
        name="band_attn_d%d" % dil,
    )(band, qkv, qkv, qkv, qkv, qkv, qkv, qkv)


def _post_kernel(x_ref, yf_ref, yb_ref, bonus_ref, g_ref, o0, l0, o1, l1, o2, l2, lng_ref, lnb_ref, wo_ref,
                 out_ref, oscr, lscr, *, tm):
    ones = _head_ones()
    y = yf_ref[...] + yb_ref[...]
    yc = y - _head_sum(y, ones) * (1.0 / HEAD_DIM)
    var = _head_sum(yc * yc, ones) * (1.0 / HEAD_DIM)
    yn = yc * lax.rsqrt(var + GN_EPS) * lng_ref[...] + lnb_ref[...]
    ya = (yn + bonus_ref[...]) * g_ref[...]
    nlb = ATTN_OUT // LANES
    os_, ls = [], []
    for gi, (o_r, l_r, (_, dil)) in enumerate(zip((o0, o1, o2), (l0, l1, l2), ATTN_PATTERNS)):
        if dil == 1:
            os_.append(o_r[0])
            ls.append(l_r[0])
            continue
        for c in range(nlb):
            for r in range(dil):
                oscr[gi * nlb + c, pl.ds(r, tm // dil, stride=dil), :] = o_r[r, :, c * LANES:(c + 1) * LANES]
                lscr[gi * nlb + c, pl.ds(r, tm // dil, stride=dil), :] = l_r[r, :, c * LANES:(c + 1) * LANES]
        os_.append(jnp.concatenate([oscr[gi * nlb + c] for c in range(nlb)], axis=1))
        ls.append(jnp.concatenate([lscr[gi * nlb + c] for c in range(nlb)], axis=1))
    m = jnp.maximum(jnp.maximum(ls[0], ls[1]), ls[2])
    es = [jnp.exp(l - m) for l in ls]
    num = es[0] * os_[0] + es[1] * os_[1] + es[2] * os_[2]
    yb = num / (es[0] + es[1] + es[2])
    out_ref[...] = (x_ref[...] + _dot(ya.astype(BF16), wo_ref[:RWKV_W, :])
                    + _dot(yb.astype(BF16), wo_ref[RWKV_W:, :]))


def _post(x2, yf, yb, bonus, g, attn, lng, lnb, wo, B, S, tm):
    T = B * S
    nts = S // tm
    row = lambda w: pl.BlockSpec((tm, w), lambda i: (i, 0))
    in_specs = [row(D_MODEL)] + [row(RWKV_W)] * 4
    args = [x2, yf, yb, bonus, g]
    for (o, l), (_, dil) in zip(attn, ATTN_PATTERNS):
        spec = pl.BlockSpec((None, dil, tm // dil, ATTN_OUT), lambda i: (i // nts, 0, i % nts, 0))
        in_specs += [spec, spec]
        args += [o, l]
    in_specs += [_full(lng.shape), _full(lnb.shape), _full(wo.shape)]
    args += [lng, lnb, wo]
    ng = len(ATTN_PATTERNS)
    return pl.pallas_call(
        functools.partial(_post_kernel, tm=tm),
        grid=(T // tm,),
        in_specs=in_specs,
        out_specs=row(D_MODEL),
        out_shape=jax.ShapeDtypeStruct((T, D_MODEL), F32),
        scratch_shapes=[pltpu.VMEM((ng * ATTN_OUT // LANES, tm, LANES), F32)] * 2,
        compiler_params=_cparams(("parallel",)),
        name="mixer_ab_out",
    )(*args)


def _window_sum(ue, rad):
    n = ue.shape[0]
    mid = slice(HALO, n - HALO)
    up = lambda a, k: pltpu.roll(a, n - k, 0)
    down = lambda a, k: pltpu.roll(a, k, 0)
    if rad < 4:
        ws = ue[mid]
        for d in range(1, rad + 1):
            ws = ws + (down(ue, d)[mid] + up(ue, d)[mid])
        return ws
    run, width = ue, 1
    while width < 2 * rad:
        run = run + up(run, width)
        width *= 2
    return down(run, rad)[mid] + up(ue, rad)[mid]


def _pool_kernel(x_ref, xp_ref, xn_ref, g_ref, win_ref, wg_ref, sc_ref, wo_ref, out_ref, *, tm, nts, S):
    it = pl.program_id(0) % nts
    x = x_ref[...]
    g = g_ref[...]
    hp = jnp.where(it == 0, 0.0, _rms(xp_ref[...], g))
    hn = jnp.where(it == nts - 1, 0.0, _rms(xn_ref[...], g))
    h = jnp.concatenate([hp, _rms(x, g), hn], axis=0).astype(BF16)
    u = _dot(h, win_ref[...])
    pos = it * tm + lax.broadcasted_iota(jnp.int32, (tm, 1), 0)
    acc = x
    for gi, win in enumerate(POOL_WINDOWS):
        rad = win // 2
        cols = slice(gi * POOL_GROUP, (gi + 1) * POOL_GROUP)
        ue = u[:, cols]
        cnt = (jnp.minimum(pos + rad + 1, S) - jnp.maximum(pos - rad, 0)).astype(F32)
        dlt = (_window_sum(ue, rad) / cnt - ue[HALO:HALO + tm]).astype(BF16)
        yg = _dot(dlt, wg_ref[gi]) * sc_ref[:, cols]
        acc = acc + _dot(yg.astype(BF16), wo_ref[cols, :])
    out_ref[...] = acc


def _pool(x2, g, w_in, w_group, scale, w_out, B, S, tm):
    T = B * S
    nts = S // tm
    prev, nxt = _halo_specs(tm, D_MODEL, T)
    row = pl.BlockSpec((tm, D_MODEL), lambda i: (i, 0))
    return pl.pallas_call(
        functools.partial(_pool_kernel, tm=tm, nts=nts, S=S),
        grid=(T // tm,),
        in_specs=[row, prev, nxt, _full(g.shape), _full(w_in.shape), _full(w_group.shape), _full(scale.shape),
                  _full(w_out.shape)],
        out_specs=row,
        out_shape=jax.ShapeDtypeStruct((T, D_MODEL), F32),
        compiler_params=_cparams(("parallel",)),
        name="pool_mixer",
    )(x2, x2, x2, g, w_in, w_group, scale, w_out)


def _gelu(x):
    c = math.sqrt(2.0 / math.pi)
    hx = 0.5 * x
    return hx + hx * jnp.tanh(x * (c + (c * 0.044715) * (x * x)))


def _ffn_kernel(x_ref, xp_ref, xn_ref, p_ref, gf_ref, wup_ref, cw_ref, cb_ref, wd_ref,
                gp_ref, wpg_ref, bpg_ref, wpp_ref, gfin_ref, out_ref, *, tm, rb, nts, final):
    it = pl.program_id(0) % nts
    g = gf_ref[...]
    n_ext = rb + 2 * HALO
    nsub = tm // rb
    for s in range(nsub):
        r0 = s * rb
        top = jnp.where(it == 0, 0.0, xp_ref[...]) if s == 0 else x_ref[pl.ds(r0 - HALO, HALO), :]
        bot = jnp.where(it == nts - 1, 0.0, xn_ref[...]) if s == nsub - 1 else x_ref[pl.ds(r0 + rb, HALO), :]
        h = _rms(jnp.concatenate([top, x_ref[pl.ds(r0, rb), :], bot], axis=0), g).astype(BF16)
        u = _dot(h, wup_ref[...])
        mid = slice(HALO, HALO + rb)
        c = (pltpu.roll(u, 1, 0)[mid] * cw_ref[0:1, :] + u[mid] * cw_ref[1:2, :]
             + pltpu.roll(u, n_ext - 1, 0)[mid] * cw_ref[2:3, :] + cb_ref[...])
        act = _gelu(c[:, :D_FF]) * c[:, D_FF:]
        rows = pl.ds(r0, rb)
        x2 = x_ref[rows, :] + _dot(act.astype(BF16), wd_ref[...])
        h3 = _rms(x2, gp_ref[...]).astype(BF16)
        gate = _sigmoid(_dot(h3, wpg_ref[...]) + bpg_ref[...])
        x3 = x2 + gate * _dot(p_ref[rows, :].astype(BF16), wpp_ref[...])
        out_ref[rows, :] = _rms(x3, gfin_ref[...]) if final else x3


def _ffn(x2, p_all, layer, w, B, S, tm, rb, final):
    T = B * S
    nt = T // tm
    nts = S // tm
    prev, nxt = _halo_specs(tm, D_MODEL, T)
    row = pl.BlockSpec((tm, D_MODEL), lambda i: (i, 0))
    names = ('gf', 'wup', 'cw', 'cb', 'wd', 'gp', 'wpg', 'bpg', 'wpp', 'gfin')
    in_specs = [row, prev, nxt, pl.BlockSpec((tm, PLE_DIM), lambda i: (layer * nt + i, 0))]
    in_specs += [_resident(w[k].shape) for k in names]
    return pl.pallas_call(
        functools.partial(_ffn_kernel, tm=tm, rb=rb, nts=nts, final=final),
        grid=(nt,),
        in_specs=in_specs,
        out_specs=row,
        out_shape=jax.ShapeDtypeStruct((T, D_MODEL), F32),
        compiler_params=_cparams(("parallel",)),
        name="convffn_ple",
    )(x2, x2, x2, p_all, *[w[k] for k in names])


def _rope_tables(S):
    half = ROT_DIM // 2
    inv = jnp.float32(ROPE_THETA) ** (-jnp.arange(half, dtype=F32) * 2.0 / ROT_DIM)
    ang = jnp.arange(S, dtype=F32)[:, None] * inv[None, :]
    cos = jnp.cos(ang)
    sin = jnp.sin(ang)
    pad1 = jnp.ones((S, HEAD_DIM - ROT_DIM), F32)
    pad0 = jnp.zeros((S, HEAD_DIM - ROT_DIM), F32)
    cos_h = jnp.concatenate([cos, cos, pad1], axis=1)
    sin_h = jnp.concatenate([-sin, sin, pad0], axis=1)
    reps = LANES // HEAD_DIM
    return jnp.tile(cos_h, (1, reps)), jnp.tile(sin_h, (1, reps))


def _pack_ab_w_in(w):
    parts = [w[:, :RWKV_IN]]
    qkv = [w[:, RWKV_IN + s * ATTN_HEADS * HEAD_DIM: RWKV_IN + (s + 1) * ATTN_HEADS * HEAD_DIM] for s in range(3)]
    for g in range(len(ATTN_PATTERNS)):
        parts += [t[:, g * ATTN_OUT:(g + 1) * ATTN_OUT] for t in qkv]
    return jnp.concatenate(parts, axis=1).astype(BF16)


def _prepare(prm):
    vec = lambda a: a.reshape(1, -1).astype(F32)
    pk = {'ab_w_in': [_pack_ab_w_in(prm['ab_w_in'][j]) for j in range(prm['ab_w_in'].shape[0])],
          'ab_w_out': prm['ab_w_out'].astype(BF16),
          'c_w_in': prm['c_w_in'].astype(BF16), 'c_w_group': prm['c_w_group'].astype(BF16),
          'c_w_out': prm['c_w_out'].astype(BF16), 'ffn': []}
    for i in range(DEPTH):
        pk['ffn'].append({
            'gf': vec(prm['norm_ffn_g'][i]), 'wup': prm['ffn_w_up'][i].astype(BF16),
            'cw': prm['ffn_conv_w'][i], 'cb': vec(prm['ffn_conv_b'][i]),
            'wd': prm['ffn_w_down'][i].astype(BF16), 'gp': vec(prm['norm_ple_g'][i]),
            'wpg': prm['ple_w_gate'][i].astype(BF16), 'bpg': vec(prm['ple_b_gate'][i]),
            'wpp': prm['ple_w_proj'][i].astype(BF16), 'gfin': vec(prm['norm_final_g'])})
    return pk


def _trunk(x, p, prm, pk, tiles):
    B, S, _ = x.shape
    T = B * S
    vec = lambda a: a.reshape(1, -1)
    x2 = x.reshape(T, D_MODEL)
    p_all = p.reshape(p.shape[0] * T, PLE_DIM)
    cos_t, sin_t = _rope_tables(S)
    for i in range(DEPTH):
        j = i // 2
        gmix = vec(prm['norm_mix_g'][i])
        if i % 2 == 0:
            (r, kk, v, g, bonus, lw0, b0, kd0, lw1, b1, kd1, q0, q1, q2) = _inproj(
                x2, gmix, pk['ab_w_in'][j], cos_t, sin_t, prm, j, B, S, tiles['inproj'])
            nb = min(B, tiles['scan_b'])
            yf, yb = _scan(r, kk, v, lw0, b0, kd0, lw1, b1, kd1, B, S, tiles['scan_rows'] // nb, nb)
            attn = [_attn(q, B, S, dil) for q, (_, dil) in zip((q0, q1, q2), ATTN_PATTERNS)]
            x2 = _post(x2, yf, yb, bonus, g, attn, vec(prm['rwkv_ln_g'][j]), vec(prm['rwkv_ln_b'][j]),
                       pk['ab_w_out'][j], B, S, tiles['post'])
        else:
            x2 = _pool(x2, gmix, pk['c_w_in'][j], pk['c_w_group'][j], vec(prm['c_scale'][j]), pk['c_w_out'][j],
                       B, S, tiles['pool'])
        x2 = _ffn(x2, p_all, i, pk['ffn'][i], B, S, tiles['ffn_m'], tiles['ffn_rb'], final=(i == DEPTH - 1))
    return x2.reshape(B, S, D_MODEL)


_TILES = {'inproj': 256, 'scan_rows': 512, 'scan_b': 4, 'post': 512, 'pool': 512, 'ffn_m': 512, 'ffn_rb': 256}


def kernel(x_prompt, x_sample, p_prompt, p_sample, ab_w_in, ab_w_out, rwkv_mu, rwkv_w0, rwkv_w_up, rwkv_a0, rwkv_a_up, rwkv_g_up, rwkv_k_k, rwkv_k_a, rwkv_r_k, rwkv_ln_g, rwkv_ln_b, c_w_in, c_w_group, c_scale, c_w_out, norm_mix_g, norm_ffn_g, norm_ple_g, norm_final_g, ffn_w_up, ffn_conv_w, ffn_conv_b, ffn_w_down, ple_w_proj, ple_w_gate, ple_b_gate):
    prm = {
        'ab_w_in': ab_w_in, 'ab_w_out': ab_w_out, 'rwkv_mu': rwkv_mu, 'rwkv_w0': rwkv_w0,
        'rwkv_w_up': rwkv_w_up, 'rwkv_a0': rwkv_a0, 'rwkv_a_up': rwkv_a_up, 'rwkv_g_up': rwkv_g_up,
        'rwkv_k_k': rwkv_k_k, 'rwkv_k_a': rwkv_k_a, 'rwkv_r_k': rwkv_r_k, 'rwkv_ln_g': rwkv_ln_g,
        'rwkv_ln_b': rwkv_ln_b, 'c_w_in': c_w_in, 'c_w_group': c_w_group, 'c_scale': c_scale,
        'c_w_out': c_w_out, 'norm_mix_g': norm_mix_g, 'norm_ffn_g': norm_ffn_g, 'norm_ple_g': norm_ple_g,
        'norm_final_g': norm_final_g, 'ffn_w_up': ffn_w_up, 'ffn_conv_w': ffn_conv_w,
        'ffn_conv_b': ffn_conv_b, 'ffn_w_down': ffn_w_down, 'ple_w_proj': ple_w_proj,
        'ple_w_gate': ple_w_gate, 'ple_b_gate': ple_b_gate,
    }
    pk = _prepare(prm)
    y_prompt = _trunk(x_prompt, p_prompt, prm, pk, _TILES)
    y_sample = _trunk(x_sample, p_sample, prm, pk, _TILES)
    return (y_prompt, y_sample)
```

```python
import functools
import math

import jax
import jax.numpy as jnp
from jax import lax
from jax.experimental import pallas as pl
from jax.experimental.pallas import tpu as pltpu

D_MODEL = 1024
DEPTH = 4
PLE_DIM = 256
HEAD_DIM = 64
RMS_EPS = 1e-6
RWKV_HEADS = 8
RWKV_W = RWKV_HEADS * HEAD_DIM
DECAY_LORA = 64
ICLR_LORA = 64
GATE_LORA = 128
RWKV_IN = 3 * RWKV_W + 2 * DECAY_LORA + 2 * ICLR_LORA + GATE_LORA
GN_EPS = 64e-5
ATTN_PATTERNS = ((128, 1), (512, 4), (2048, 16))
ATTN_HEADS_PER_GROUP = 4
ATTN_HEADS = ATTN_HEADS_PER_GROUP * len(ATTN_PATTERNS)
ATTN_OUT = ATTN_HEADS_PER_GROUP * HEAD_DIM
ATTN_GROUP_IN = 3 * ATTN_OUT
ATTN_HALF = 64
ATTN_Q_ROWS = 256
ATTN_STEP_TOKENS = 2048
ROPE_THETA = 500000.0
ROT_DIM = HEAD_DIM // 4
NEG_INF = -1e30
AB_OUT = RWKV_W + ATTN_OUT
POOL_WINDOWS = (2, 4, 8, 16)
POOL_GROUP = 256
POOL_W = len(POOL_WINDOWS) * POOL_GROUP
D_FF = 2816

LANES = 128
SUBLANES = 8
HALO = SUBLANES
CHUNK = 64
MXU_DIM = 256
SCAN_GROUP_LANES = LANES
VMEM_BYTES = 64 * 1024 * 1024
VMEM_LIMIT = VMEM_BYTES // 8 * 7
assert all(win // (2 * dil) == ATTN_HALF for win, dil in ATTN_PATTERNS)
assert HALO >= max(POOL_WINDOWS) // 2

F32 = jnp.float32
BF16 = jnp.bfloat16


def _dot(a, b):
    return jnp.dot(a, b, preferred_element_type=F32)


def _dot_nt(a, b):
    return lax.dot_general(a, b, (((1,), (1,)), ((), ())), preferred_element_type=F32)


def _split(x):
    hi = x.astype(BF16)
    lo = (x - hi.astype(F32)).astype(BF16)
    return hi, lo


def _rms(x, g):
    return x * lax.rsqrt(jnp.mean(x * x, axis=-1, keepdims=True) + RMS_EPS) * g


def _sigmoid(x):
    return 1.0 / (1.0 + jnp.exp(-x))


def _head_ones():
    r = lax.broadcasted_iota(jnp.int32, (MXU_DIM, MXU_DIM), 0) // HEAD_DIM
    c = lax.broadcasted_iota(jnp.int32, (MXU_DIM, MXU_DIM), 1) // HEAD_DIM
    return jnp.where(r == c, 1.0, 0.0).astype(BF16)


def _head_sum(x, ones):
    outs = []
    for j in range(x.shape[1] // MXU_DIM):
        hi, lo = _split(x[:, j * MXU_DIM:(j + 1) * MXU_DIM])
        outs.append(_dot(hi, ones) + _dot(lo, ones))
    return jnp.concatenate(outs, axis=1)


def _cparams(sem):
    return pltpu.CompilerParams(dimension_semantics=sem, vmem_limit_bytes=VMEM_LIMIT)


def _halo_specs(tm, width, n_rows):
    nb = n_rows // HALO
    step = tm // HALO
    prev = pl.BlockSpec((HALO, width), lambda i: (jnp.maximum(i * step - 1, 0), 0))
    nxt = pl.BlockSpec((HALO, width), lambda i: (jnp.minimum((i + 1) * step, nb - 1), 0))
    return prev, nxt


def _resident(shape):
    nd = len(shape)
    return pl.BlockSpec(shape, lambda *_: (0,) * nd, pipeline_mode=pl.Buffered(1))


def _inproj_kernel(x_ref, xp_ref, xn_ref, g_ref, w_ref, cos_ref, sin_ref,
                   mu_ref, w0_ref, wu_ref, a0_ref, au_ref, gu_ref, kk_ref, ka_ref, rk_ref,
                   r_o, kk_o, v_o, g_o, bonus_o, lw0_o, b0_o, kd0_o, lw1_o, b1_o, kd1_o, q0_ref, q1_ref, q2_ref,
                   scr_ref, *, tm, nts):
    it = pl.program_id(0) % nts
    gm = g_ref[...]
    hm = _rms(x_ref[...], gm)
    hp = jnp.where(it == 0, 0.0, _rms(xp_ref[...], gm))
    hn = jnp.where(it == nts - 1, 0.0, _rms(xn_ref[...], gm))
    h = hm.astype(BF16)
    z_ext = _dot(jnp.concatenate([hp, hm, hn], axis=0).astype(BF16), w_ref[:, :RWKV_IN])
    t_attn = [_dot(h, w_ref[:, RWKV_IN + g * ATTN_GROUP_IN:RWKV_IN + (g + 1) * ATTN_GROUP_IN])
              for g in range(len(ATTN_PATTERNS))]
    cs = cos_ref[...]
    sn = sin_ref[...]
    lane = lax.broadcasted_iota(jnp.int32, (tm, LANES), 1)
    first = (lane & (HEAD_DIM - 1)) < (ROT_DIM // 2)
    outs = (q0_ref, q1_ref, q2_ref)
    for g, (_, dil) in enumerate(ATTN_PATTERNS):
        t = t_attn[g]
        for c in range(ATTN_GROUP_IN // LANES):
            tc = t[:, c * LANES:(c + 1) * LANES]
            if c < 2 * ATTN_OUT // LANES:
                rot = jnp.where(first, pltpu.roll(tc, LANES - ROT_DIM // 2, 1), pltpu.roll(tc, ROT_DIM // 2, 1))
                tc = tc * cs + rot * sn
            if dil == 1:
                outs[g][0, :, c * LANES:(c + 1) * LANES] = tc
            else:
                scr_ref[g, c] = tc
        if dil > 1:
            for c in range(ATTN_GROUP_IN // LANES):
                for r in range(dil):
                    outs[g][r, :, c * LANES:(c + 1) * LANES] = scr_ref[g, c, pl.ds(r, tm // dil, stride=dil), :]
    mid = slice(HALO, HALO + tm)
    z = z_ext[mid]
    zs = 0.5 * (pltpu.roll(z_ext, 1, 0)[mid] + pltpu.roll(z_ext, tm + 2 * HALO - 1, 0)[mid])
    zz = z + mu_ref[...] * (zs - z)
    W = RWKV_W
    r = zz[:, 0:W]
    k = zz[:, W:2 * W]
    v = zz[:, 2 * W:3 * W]
    o = 3 * W
    wd = jnp.tanh(zz[:, o:o + 2 * DECAY_LORA])
    o += 2 * DECAY_LORA
    ad = zz[:, o:o + 2 * ICLR_LORA]
    o += 2 * ICLR_LORA
    gd = _sigmoid(zz[:, o:])
    wlog = w0_ref[...] + _dot(wd.astype(BF16), wu_ref[...])
    aa = _sigmoid(a0_ref[...] + _dot(ad.astype(BF16), au_ref[...]))
    g_o[...] = _dot(gd.astype(BF16), gu_ref[...])
    ones = _head_ones()
    kkv = k * kk_ref[...]
    kkn = kkv / jnp.sqrt(_head_sum(kkv * kkv, ones) + 1e-12)
    r_o[...] = r
    kk_o[...] = kkn
    v_o[...] = v
    ka = ka_ref[...]
    kd_sum = None
    for d, (lw_o, b_o, kd_o) in enumerate(((lw0_o, b0_o, kd0_o), (lw1_o, b1_o, kd1_o))):
        a_d = aa[:, d * W:(d + 1) * W]
        lw_o[...] = -math.exp(-0.5) * _sigmoid(wlog[:, d * W:(d + 1) * W])
        kd = k * (1.0 + (a_d - 1.0) * ka)
        kd_o[...] = kd
        b_o[...] = kkn * a_d
        kd_sum = kd if kd_sum is None else kd_sum + kd
    bonus_o[...] = _head_sum(r * kd_sum * rk_ref[...], ones) * v


def _inproj(x2, g, w, cos_t, sin_t, prm, j, B, S, tm):
    T = B * S
    nts = S // tm
    W = RWKV_W
    vec = lambda a: a.reshape(1, -1)

    def bdiag(m):
        L = m.shape[1]
        return jnp.zeros((2 * L, 2 * W), F32).at[:L, :W].set(m[0]).at[L:, W:].set(m[1]).astype(BF16)

    params = [vec(prm['rwkv_mu'][j]), vec(prm['rwkv_w0'][j]), bdiag(prm['rwkv_w_up'][j]),
              vec(prm['rwkv_a0'][j]), bdiag(prm['rwkv_a_up'][j]), prm['rwkv_g_up'][j].astype(BF16),
              vec(prm['rwkv_k_k'][j]), vec(prm['rwkv_k_a'][j]), vec(prm['rwkv_r_k'][j])]
    prev, nxt = _halo_specs(tm, D_MODEL, T)
    row = pl.BlockSpec((tm, W), lambda i: (i, 0))
    out_shape = [jax.ShapeDtypeStruct((T, W), F32)] * 11
    out_specs = [row] * 11
    for _, dil in ATTN_PATTERNS:
        out_shape.append(jax.ShapeDtypeStruct((B, dil, S // dil, ATTN_GROUP_IN), F32))
        out_specs.append(pl.BlockSpec((None, dil, tm // dil, ATTN_GROUP_IN),
                                      lambda i: (i // nts, 0, i % nts, 0)))
    return pl.pallas_call(
        functools.partial(_inproj_kernel, tm=tm, nts=nts),
        grid=(T // tm,),
        in_specs=[pl.BlockSpec((tm, D_MODEL), lambda i: (i, 0)), prev, nxt,
                  _resident((1, D_MODEL)),
                  _resident(w.shape),
                  pl.BlockSpec((tm, LANES), lambda i: (i % nts, 0)),
                  pl.BlockSpec((tm, LANES), lambda i: (i % nts, 0))] + [_resident(a.shape) for a in params],
        out_specs=out_specs,
        out_shape=out_shape,
        scratch_shapes=[pltpu.VMEM((len(ATTN_PATTERNS), ATTN_GROUP_IN // LANES, tm, LANES), F32)],
        compiler_params=_cparams(("parallel",)),
        name="inproj",
    )(x2, x2, x2, g, w, cos_t, sin_t, *params)


def _stack2(x):
    head = lax.broadcasted_iota(jnp.int32, x.shape, 1) // HEAD_DIM
    zero = jnp.zeros_like(x)
    return jnp.concatenate([jnp.where(head == h, x, zero) for h in range(x.shape[1] // HEAD_DIM)], axis=0)


def _pm(m, x):
    return _dot(m.astype(BF16), _stack2(x.astype(BF16)))


def _wkv_scaled(r, kk, v, lw, b, kd, fwd):
    C = CHUNK
    ti = lax.broadcasted_iota(jnp.int32, (C, C), 0)
    tj = lax.broadcasted_iota(jnp.int32, (C, C), 1)
    cum = jnp.where((tj <= ti) if fwd else (tj >= ti), 1.0, 0.0).astype(BF16)
    lh, ll = _split(lw)
    G = _dot(cum, lh) + _dot(cum, ll)
    g_tot = G[C - 1:C, :] if fwd else G[0:1, :]
    e_neg = jnp.exp(-G)
    rt = r * jnp.exp(G)
    at = -kk * jnp.exp(G - lw)
    kt = kd * e_neg
    bt = b * e_neg
    e_tot = jnp.exp(g_tot)
    kb = jnp.concatenate([kt * e_tot, bt * e_tot], axis=0)
    return dict(at=at, rt=rt, kt=kt, bt=bt, kb=kb, v=v, e_tot=e_tot, fwd=fwd)


def _wkv_pairs(chains):
    C = CHUNK
    hw = chains[0]['at'].shape[1]
    gw = hw // HEAD_DIM * C
    trow = lax.broadcasted_iota(jnp.int32, (C, gw), 0)
    tcol = lax.broadcasted_iota(jnp.int32, (C, gw), 1) & (C - 1)
    eye = jnp.where(tcol == trow, 1.0, 0.0)
    hr = lax.broadcasted_iota(jnp.int32, (hw, hw), 0) // HEAD_DIM
    hc = lax.broadcasted_iota(jnp.int32, (hw, hw), 1) // HEAD_DIM
    for ch in chains:
        ch['x'] = jnp.concatenate([ch['at'], ch['rt']], axis=0).astype(BF16)
    for ch in chains:
        kb2 = jnp.concatenate([_stack2(ch['kt'].astype(BF16)), _stack2(ch['bt'].astype(BF16))], axis=0)
        sc = _dot_nt(ch['x'], kb2)
        ch['sc_k'] = sc[:, :gw]
        ch['sc_b'] = sc[:, gw:]
        ch['p1'] = _dot_nt(ch['x'], ch['S'].astype(BF16))
    for ch in chains:
        strict = (tcol < trow) if ch['fwd'] else (tcol > trow)
        incl = (tcol <= trow) if ch['fwd'] else (tcol >= trow)
        ch['a_kk'] = jnp.concatenate([jnp.where(strict, ch['sc_k'][:C], 0.0),
                                      jnp.where(incl, ch['sc_k'][C:], 0.0)], axis=0)
        ch['n'] = jnp.where(strict, ch['sc_b'][:C], 0.0)
        ch['a_rb'] = jnp.where(incl, ch['sc_b'][C:], 0.0)
        ch['t'] = eye + ch['n']
    for j in range(6):
        for ch in chains:
            n = ch['n']
            if j == 0:
                ch['n'] = _pm(n, n)
                ch['p2'] = _pm(ch['a_kk'], ch['v'])
            elif j < 5:
                m = _pm(jnp.concatenate([ch['t'], n], axis=0), n)
                ch['t'] = ch['t'] + m[:C]
                ch['n'] = m[C:]
            else:
                ch['t'] = ch['t'] + _pm(ch['t'], n)
    for ch in chains:
        ch['u'] = _pm(ch['t'], ch['p1'][:C] + ch['p2'][:C])
    for ch in chains:
        ch['y'] = ch['p1'][C:] + ch['p2'][C:] + _pm(ch['a_rb'], ch['u'])
        vu_t = jnp.concatenate([ch['v'], ch['u']], axis=0).T
        upd = _dot(vu_t.astype(BF16), ch['kb'].astype(BF16))
        ch['s_new'] = ch['S'] * ch['e_tot'] + jnp.where(hr == hc, upd, 0.0)


def _scan_kernel(rf, kkf, vf, lwf, bf, kdf, rb, kkb, vb, lwb, bb, kdb, yf_o, yb_o, sf_ref, sb_ref, *, nc, nb):
    @pl.when(pl.program_id(1) == 0)
    def _():
        sf_ref[...] = jnp.zeros_like(sf_ref)
        sb_ref[...] = jnp.zeros_like(sb_ref)

    npairs = RWKV_W // SCAN_GROUP_LANES

    def body(c, carry):
        slf = pl.ds(pl.multiple_of(c * CHUNK, CHUNK), CHUNK)
        slb = pl.ds(pl.multiple_of((nc - 1 - c) * CHUNK, CHUNK), CHUNK)
        chains = []
        for bi in range(nb):
            dirs = (_wkv_scaled(*[ref[bi, slf, :] for ref in (rf, kkf, vf, lwf, bf, kdf)], True),
                    _wkv_scaled(*[ref[bi, slb, :] for ref in (rb, kkb, vb, lwb, bb, kdb)], False))
            for d, s_ref in zip(dirs, (sf_ref, sb_ref)):
                for p in range(npairs):
                    sl = slice(p * SCAN_GROUP_LANES, (p + 1) * SCAN_GROUP_LANES)
                    ch = {k: (val if k == 'fwd' else val[:, sl]) for k, val in d.items()}
                    ch['S'] = s_ref[bi * npairs + p]
                    chains.append(ch)
        _wkv_pairs(chains)
        chains = iter(chains)
        for bi in range(nb):
            for s_ref, y_o, sl in ((sf_ref, yf_o, slf), (sb_ref, yb_o, slb)):
                ys = []
                for p in range(npairs):
                    ch = next(chains)
                    s_ref[bi * npairs + p] = ch['s_new']
                    ys.append(ch['y'])
                y_o[bi, sl, :] = jnp.concatenate(ys, axis=1)
        return carry

    lax.fori_loop(0, nc, body, 0)


def _scan(r, kk, v, lw0, b0, kd0, lw1, b1, kd1, B, S, tb, nb):
    assert B % nb == 0 and S % tb == 0 and tb % CHUNK == 0
    nblk = S // tb
    shp = (B, S, RWKV_W)
    args = [a.reshape(shp) for a in (r, kk, v, lw0, b0, kd0, r, kk, v, lw1, b1, kd1)]
    fwd = pl.BlockSpec((nb, tb, RWKV_W), lambda bi, j: (bi, j, 0))
    bwd = pl.BlockSpec((nb, tb, RWKV_W), lambda bi, j: (bi, nblk - 1 - j, 0))
    yf, yb = pl.pallas_call(
        functools.partial(_scan_kernel, nc=tb // CHUNK, nb=nb),
        grid=(B // nb, nblk),
        in_specs=[fwd] * 6 + [bwd] * 6,
        out_specs=[fwd, bwd],
        out_shape=[jax.ShapeDtypeStruct(shp, F32)] * 2,
        scratch_shapes=[pltpu.VMEM((nb * RWKV_W // SCAN_GROUP_LANES, SCAN_GROUP_LANES, SCAN_GROUP_LANES), F32)] * 2,
        compiler_params=_cparams(("parallel", "arbitrary")),
        name="wkv_scan",
    )(*args)
    return yf.reshape(B * S, RWKV_W), yb.reshape(B * S, RWKV_W)


def _attn_kernel(band_ref, q_ref, km_ref, vm_ref, kp_ref, vp_ref, kn_ref, vn_ref, o_ref, lse_ref,
                 *, Q, Qs, L, rpb, ru):
    j = pl.program_id(2)
    Wn = Qs + 2 * ATTN_HALF
    heads = range(ATTN_HEADS_PER_GROUP)
    lane = lax.broadcasted_iota(jnp.int32, (Qs, ATTN_OUT), 1) // HEAD_DIM
    biases = []
    for sb in range(Q // Qs):
        kpos = j * Q + sb * Qs - ATTN_HALF + lax.broadcasted_iota(jnp.int32, (1, Wn), 1)
        biases.append(band_ref[...] + jnp.where(jnp.minimum(kpos, L - 1 - kpos) >= 0, 0.0, NEG_INF))

    def body(it, carry):
        units = []
        for rr in range(ru):
            r = it * ru + rr
            k_all = jnp.concatenate([kp_ref[r], km_ref[r], kn_ref[r]], axis=0).astype(BF16)
            v_all = jnp.concatenate([vp_ref[r], vm_ref[r], vn_ref[r]], axis=0).astype(BF16)
            for sb in range(Q // Qs):
                units.append(dict(r=r, rows=pl.ds(sb * Qs, Qs), bias=biases[sb],
                                  kw=k_all[sb * Qs:sb * Qs + Wn], vw=v_all[sb * Qs:sb * Qs + Wn]))
        for un in units:
            q = q_ref[un['r'], un['rows'], :] * (HEAD_DIM ** -0.5)
            un['s'] = [_dot_nt(jnp.where(lane == hh, q, 0.0).astype(BF16), un['kw']) + un['bias'] for hh in heads]
        for un in units:
            un['m'] = [jnp.max(s, axis=-1, keepdims=True) for s in un['s']]
            un['p'] = [jnp.exp(s - m) for s, m in zip(un['s'], un['m'])]
            un['den'] = [jnp.sum(p, axis=-1, keepdims=True) for p in un['p']]
        for un in units:
            un['o'] = [_dot(p.astype(BF16), un['vw']) for p in un['p']]
        for un in units:
            o_acc = jnp.zeros((Qs, ATTN_OUT), F32)
            l_acc = jnp.zeros((Qs, ATTN_OUT), F32)
            for hh in heads:
                o_acc = jnp.where(lane == hh, un['o'][hh] / un['den'][hh], o_acc)
                l_acc = jnp.where(lane == hh, un['m'][hh] + jnp.log(un['den'][hh]), l_acc)
            o_ref[un['r'], un['rows'], :] = o_acc
            lse_ref[un['r'], un['rows'], :] = l_acc
        return carry

    lax.fori_loop(0, rpb // ru, body, 0)


def _attn(qkv, B, S, dil):
    L = S // dil
    Qs = min(ATTN_Q_ROWS, L)
    nsub, ru = (2, 1) if dil == 1 else (1, 2)
    Q = Qs * nsub
    rpb = min(dil, ATTN_STEP_TOKENS // Q)
    assert L % Q == 0 and dil % rpb == 0 and rpb % ru == 0
    nq = L // Q
    qb = Q // ATTN_HALF
    nhb = L // ATTN_HALF
    Wn = Qs + 2 * ATTN_HALF
    rel = jnp.arange(Wn)[None, :] - ATTN_HALF - jnp.arange(Qs)[:, None]
    band = jnp.where(jnp.abs(rel) <= ATTN_HALF, 0.0, NEG_INF).astype(F32)

    def mid(col):
        return pl.BlockSpec((None, rpb, Q, ATTN_OUT), lambda bi, rb, j: (bi, rb, j, col))

    def prev(col):
        return pl.BlockSpec((None, rpb, ATTN_HALF, ATTN_OUT),
                            lambda bi, rb, j: (bi, rb, jnp.maximum(j * qb - 1, 0), col))

    def nxt(col):
        return pl.BlockSpec((None, rpb, ATTN_HALF, ATTN_OUT),
                            lambda bi, rb, j: (bi, rb, jnp.minimum((j + 1) * qb, nhb - 1), col))

    out = pl.BlockSpec((None, rpb, Q, ATTN_OUT), lambda bi, rb, j: (bi, rb, j, 0))
    return pl.pallas_call(
        functools.partial(_attn_kernel, Q=Q, Qs=Qs, L=L, rpb=rpb, ru=ru),
        grid=(B, dil // rpb, nq),
        in_specs=[_resident(band.shape), mid(0), mid(1), mid(2), prev(1), prev(2), nxt(1), nxt(2)],
        out_specs=[out, out],
        out_shape=[jax.ShapeDtypeStruct((B, dil, L, ATTN_OUT), F32)] * 2,
        compiler_params=_cparams(("parallel", "parallel", "parallel")),
        name="band_attn_d%d" % dil,
    )(band, qkv, qkv, qkv, qkv, qkv, qkv, qkv)


def _post_kernel(x_ref, yf_ref, yb_ref, bonus_ref, g_ref, o0, l0, o1, l1, o2, l2, lng_ref, lnb_ref, wo_ref,
                 out_ref, oscr, lscr, *, tm):
    ones = _head_ones()
    y = yf_ref[...] + yb_ref[...]
    yc = y - _head_sum(y, ones) * (1.0 / HEAD_DIM)
    var = _head_sum(yc * yc, ones) * (1.0 / HEAD_DIM)
    yn = yc * lax.rsqrt(var + GN_EPS) * lng_ref[...] + lnb_ref[...]
    ya = (yn + bonus_ref[...]) * g_ref[...]
    nlb = ATTN_OUT // LANES
    os_, ls = [], []
    for gi, (o_r, l_r, (_, dil)) in enumerate(zip((o0, o1, o2), (l0, l1, l2), ATTN_PATTERNS)):
        if dil == 1:
            os_.append(o_r[0])
            ls.append(l_r[0])
            continue
        for c in range(nlb):
            for r in range(dil):
                oscr[gi * nlb + c, pl.ds(r, tm // dil, stride=dil), :] = o_r[r, :, c * LANES:(c + 1) * LANES]
                lscr[gi * nlb + c, pl.ds(r, tm // dil, stride=dil), :] = l_r[r, :, c * LANES:(c + 1) * LANES]
        os_.append(jnp.concatenate([oscr[gi * nlb + c] for c in range(nlb)], axis=1))
        ls.append(jnp.concatenate([lscr[gi * nlb + c] for c in range(nlb)], axis=1))
    m = jnp.maximum(jnp.maximum(ls[0], ls[1]), ls[2])
    es = [jnp.exp(l - m) for l in ls]
    num = es[0] * os_[0] + es[1] * os_[1] + es[2] * os_[2]
    yb = num / (es[0] + es[1] + es[2])
    out_ref[...] = (x_ref[...] + _dot(ya.astype(BF16), wo_ref[:RWKV_W, :])
                    + _dot(yb.astype(BF16), wo_ref[RWKV_W:, :]))


def _post(x2, yf, yb, bonus, g, attn, lng, lnb, wo, B, S, tm):
    T = B * S
    nts = S // tm
    row = lambda w: pl.BlockSpec((tm, w), lambda i: (i, 0))
    in_specs = [row(D_MODEL)] + [row(RWKV_W)] * 4
    args = [x2, yf, yb, bonus, g]
    for (o, l), (_, dil) in zip(attn, ATTN_PATTERNS):
        spec = pl.BlockSpec((None, dil, tm // dil, ATTN_OUT), lambda i: (i // nts, 0, i % nts, 0))
        in_specs += [spec, spec]
        args += [o, l]
    in_specs += [_resident(lng.shape), _resident(lnb.shape), _resident(wo.shape)]
    args += [lng, lnb, wo]
    ng = len(ATTN_PATTERNS)
    return pl.pallas_call(
        functools.partial(_post_kernel, tm=tm),
        grid=(T // tm,),
        in_specs=in_specs,
        out_specs=row(D_MODEL),
        out_shape=jax.ShapeDtypeStruct((T, D_MODEL), F32),
        scratch_shapes=[pltpu.VMEM((ng * ATTN_OUT // LANES, tm, LANES), F32)] * 2,
        compiler_params=_cparams(("parallel",)),
        name="mixer_ab_out",
    )(*args)


def _window_sum(ue, rad):
    n = ue.shape[0]
    mid = slice(HALO, n - HALO)
    up = lambda a, k: pltpu.roll(a, n - k, 0)
    down = lambda a, k: pltpu.roll(a, k, 0)
    if rad < 4:
        ws = ue[mid]
        for d in range(1, rad + 1):
            ws = ws + (down(ue, d)[mid] + up(ue, d)[mid])
        return ws
    run, width = ue, 1
    while width < 2 * rad:
        run = run + up(run, width)
        width *= 2
    return down(run, rad)[mid] + up(ue, rad)[mid]


def _pool_kernel(x_ref, xp_ref, xn_ref, g_ref, win_ref, wg_ref, sc_ref, wo_ref, out_ref, *, tm, nts, S):
    it = pl.program_id(0) % nts
    x = x_ref[...]
    g = g_ref[...]
    hp = jnp.where(it == 0, 0.0, _rms(xp_ref[...], g))
    hn = jnp.where(it == nts - 1, 0.0, _rms(xn_ref[...], g))
    h = jnp.concatenate([hp, _rms(x, g), hn], axis=0).astype(BF16)
    u = _dot(h, win_ref[...])
    pos = it * tm + lax.broadcasted_iota(jnp.int32, (tm, 1), 0)
    acc = x
    for gi, win in enumerate(POOL_WINDOWS):
        rad = win // 2
        cols = slice(gi * POOL_GROUP, (gi + 1) * POOL_GROUP)
        ue = u[:, cols]
        cnt = (jnp.minimum(pos + rad + 1, S) - jnp.maximum(pos - rad, 0)).astype(F32)
        dlt = (_window_sum(ue, rad) / cnt - ue[HALO:HALO + tm]).astype(BF16)
        yg = _dot(dlt, wg_ref[gi]) * sc_ref[:, cols]
        acc = acc + _dot(yg.astype(BF16), wo_ref[cols, :])
    out_ref[...] = acc


def _pool(x2, g, w_in, w_group, scale, w_out, B, S, tm):
    T = B * S
    nts = S // tm
    prev, nxt = _halo_specs(tm, D_MODEL, T)
    row = pl.BlockSpec((tm, D_MODEL), lambda i: (i, 0))
    return pl.pallas_call(
        functools.partial(_pool_kernel, tm=tm, nts=nts, S=S),
        grid=(T // tm,),
        in_specs=[row, prev, nxt] + [_resident(a.shape) for a in (g, w_in, w_group, scale, w_out)],
        out_specs=row,
        out_shape=jax.ShapeDtypeStruct((T, D_MODEL), F32),
        compiler_params=_cparams(("parallel",)),
        name="pool_mixer",
    )(x2, x2, x2, g, w_in, w_group, scale, w_out)


def _gelu(x):
    c = math.sqrt(2.0 / math.pi)
    hx = 0.5 * x
    return hx + hx * jnp.tanh(x * (c + (c * 0.044715) * (x * x)))


def _ffn_kernel(x_ref, xp_ref, xn_ref, p_ref, gf_ref, wup_ref, cw_ref, cb_ref, wd_ref,
                gp_ref, wpg_ref, bpg_ref, wpp_ref, gfin_ref, out_ref, *, tm, rb, nts, final):
    it = pl.program_id(0) % nts
    g = gf_ref[...]
    n_ext = rb + 2 * HALO
    nsub = tm // rb
    for s in range(nsub):
        r0 = s * rb
        top = jnp.where(it == 0, 0.0, xp_ref[...]) if s == 0 else x_ref[pl.ds(r0 - HALO, HALO), :]
        bot = jnp.where(it == nts - 1, 0.0, xn_ref[...]) if s == nsub - 1 else x_ref[pl.ds(r0 + rb, HALO), :]
        h = _rms(jnp.concatenate([top, x_ref[pl.ds(r0, rb), :], bot], axis=0), g).astype(BF16)
        u = _dot(h, wup_ref[...])
        mid = slice(HALO, HALO + rb)
        c = (pltpu.roll(u, 1, 0)[mid] * cw_ref[0:1, :] + u[mid] * cw_ref[1:2, :]
             + pltpu.roll(u, n_ext - 1, 0)[mid] * cw_ref[2:3, :] + cb_ref[...])
        act = _gelu(c[:, :D_FF]) * c[:, D_FF:]
        rows = pl.ds(r0, rb)
        x2 = x_ref[rows, :] + _dot(act.astype(BF16), wd_ref[...])
        h3 = _rms(x2, gp_ref[...]).astype(BF16)
        gate = _sigmoid(_dot(h3, wpg_ref[...]) + bpg_ref[...])
        x3 = x2 + gate * _dot(p_ref[rows, :].astype(BF16), wpp_ref[...])
        out_ref[rows, :] = _rms(x3, gfin_ref[...]) if final else x3


def _ffn(x2, p_all, layer, w, B, S, tm, rb, final):
    T = B * S
    nt = T // tm
    nts = S // tm
    prev, nxt = _halo_specs(tm, D_MODEL, T)
    row = pl.BlockSpec((tm, D_MODEL), lambda i: (i, 0))
    names = ('gf', 'wup', 'cw', 'cb', 'wd', 'gp', 'wpg', 'bpg', 'wpp', 'gfin')
    in_specs = [row, prev, nxt, pl.BlockSpec((tm, PLE_DIM), lambda i: (layer * nt + i, 0))]
    in_specs += [_resident(w[k].shape) for k in names]
    return pl.pallas_call(
        functools.partial(_ffn_kernel, tm=tm, rb=rb, nts=nts, final=final),
        grid=(nt,),
        in_specs=in_specs,
        out_specs=row,
        out_shape=jax.ShapeDtypeStruct((T, D_MODEL), F32),
        compiler_params=_cparams(("parallel",)),
        name="convffn_ple",
    )(x2, x2, x2, p_all, *[w[k] for k in names])


def _rope_tables(S):
    half = ROT_DIM // 2
    inv = jnp.float32(ROPE_THETA) ** (-jnp.arange(half, dtype=F32) * 2.0 / ROT_DIM)
    ang = jnp.arange(S, dtype=F32)[:, None] * inv[None, :]
    cos = jnp.cos(ang)
    sin = jnp.sin(ang)
    pad1 = jnp.ones((S, HEAD_DIM - ROT_DIM), F32)
    pad0 = jnp.zeros((S, HEAD_DIM - ROT_DIM), F32)
    cos_h = jnp.concatenate([cos, cos, pad1], axis=1)
    sin_h = jnp.concatenate([-sin, sin, pad0], axis=1)
    reps = LANES // HEAD_DIM
    return jnp.tile(cos_h, (1, reps)), jnp.tile(sin_h, (1, reps))


def _pack_ab_w_in(w):
    parts = [w[:, :RWKV_IN]]
    qkv = [w[:, RWKV_IN + s * ATTN_HEADS * HEAD_DIM: RWKV_IN + (s + 1) * ATTN_HEADS * HEAD_DIM] for s in range(3)]
    for g in range(len(ATTN_PATTERNS)):
        parts += [t[:, g * ATTN_OUT:(g + 1) * ATTN_OUT] for t in qkv]
    return jnp.concatenate(parts, axis=1).astype(BF16)


def _prepare(prm):
    vec = lambda a: a.reshape(1, -1).astype(F32)
    pk = {'ab_w_in': [_pack_ab_w_in(prm['ab_w_in'][j]) for j in range(prm['ab_w_in'].shape[0])],
          'ab_w_out': prm['ab_w_out'].astype(BF16),
          'c_w_in': prm['c_w_in'].astype(BF16), 'c_w_group': prm['c_w_group'].astype(BF16),
          'c_w_out': prm['c_w_out'].astype(BF16), 'ffn': []}
    for i in range(DEPTH):
        pk['ffn'].append({
            'gf': vec(prm['norm_ffn_g'][i]), 'wup': prm['ffn_w_up'][i].astype(BF16),
            'cw': prm['ffn_conv_w'][i], 'cb': vec(prm['ffn_conv_b'][i]),
            'wd': prm['ffn_w_down'][i].astype(BF16), 'gp': vec(prm['norm_ple_g'][i]),
            'wpg': prm['ple_w_gate'][i].astype(BF16), 'bpg': vec(prm['ple_b_gate'][i]),
            'wpp': prm['ple_w_proj'][i].astype(BF16), 'gfin': vec(prm['norm_final_g'])})
    return pk


def _trunk(x, p, prm, pk, tiles):
    B, S, _ = x.shape
    T = B * S
    vec = lambda a: a.reshape(1, -1)
    x2 = x.reshape(T, D_MODEL)
    p_all = p.reshape(p.shape[0] * T, PLE_DIM)
    cos_t, sin_t = _rope_tables(S)
    for i in range(DEPTH):
        j = i // 2
        gmix = vec(prm['norm_mix_g'][i])
        if i % 2 == 0:
            (r, kk, v, g, bonus, lw0, b0, kd0, lw1, b1, kd1, q0, q1, q2) = _inproj(
                x2, gmix, pk['ab_w_in'][j], cos_t, sin_t, prm, j, B, S, tiles['inproj'])
            nb = min(B, tiles['scan_b'])
            yf, yb = _scan(r, kk, v, lw0, b0, kd0, lw1, b1, kd1, B, S, tiles['scan_rows'] // nb, nb)
            attn = [_attn(q, B, S, dil) for q, (_, dil) in zip((q0, q1, q2), ATTN_PATTERNS)]
            x2 = _post(x2, yf, yb, bonus, g, attn, vec(prm['rwkv_ln_g'][j]), vec(prm['rwkv_ln_b'][j]),
                       pk['ab_w_out'][j], B, S, tiles['post'])
        else:
            x2 = _pool(x2, gmix, pk['c_w_in'][j], pk['c_w_group'][j], vec(prm['c_scale'][j]), pk['c_w_out'][j],
                       B, S, tiles['pool'])
        x2 = _ffn(x2, p_all, i, pk['ffn'][i], B, S, tiles['ffn_m'], tiles['ffn_rb'], final=(i == DEPTH - 1))
    return x2.reshape(B, S, D_MODEL)


_TILES = {'inproj': 256, 'scan_rows': 512, 'scan_b': 4, 'post': 512, 'pool': 1024, 'ffn_m': 1024, 'ffn_rb': 512}


def kernel(x_prompt, x_sample, p_prompt, p_sample, ab_w_in, ab_w_out, rwkv_mu, rwkv_w0, rwkv_w_up, rwkv_a0, rwkv_a_up, rwkv_g_up, rwkv_k_k, rwkv_k_a, rwkv_r_k, rwkv_ln_g, rwkv_ln_b, c_w_in, c_w_group, c_scale, c_w_out, norm_mix_g, norm_ffn_g, norm_ple_g, norm_final_g, ffn_w_up, ffn_conv_w, ffn_conv_b, ffn_w_down, ple_w_proj, ple_w_gate, ple_b_gate):
    prm = {
        'ab_w_in': ab_w_in, 'ab_w_out': ab_w_out, 'rwkv_mu': rwkv_mu, 'rwkv_w0': rwkv_w0,
        'rwkv_w_up': rwkv_w_up, 'rwkv_a0': rwkv_a0, 'rwkv_a_up': rwkv_a_up, 'rwkv_g_up': rwkv_g_up,
        'rwkv_k_k': rwkv_k_k, 'rwkv_k_a': rwkv_k_a, 'rwkv_r_k': rwkv_r_k, 'rwkv_ln_g': rwkv_ln_g,
        'rwkv_ln_b': rwkv_ln_b, 'c_w_in': c_w_in, 'c_w_group': c_w_group, 'c_scale': c_scale,
        'c_w_out': c_w_out, 'norm_mix_g': norm_mix_g, 'norm_ffn_g': norm_ffn_g, 'norm_ple_g': norm_ple_g,
        'norm_final_g': norm_final_g, 'ffn_w_up': ffn_w_up, 'ffn_conv_w': ffn_conv_w,
        'ffn_conv_b': ffn_conv_b, 'ffn_w_down': ffn_w_down, 'ple_w_proj': ple_w_proj,
        'ple_w_gate': ple_w_gate, 'ple_b_gate': ple_b_gate,
    }
    pk = _prepare(prm)
    y_prompt = _trunk(x_prompt, p_prompt, prm, pk, _TILES)
    y_sample = _trunk(x_sample, p_sample, prm, pk, _TILES)
    return (y_prompt, y_sample)
```

```python
import functools
import math

import jax
import jax.numpy as jnp
from jax import lax
from jax.experimental import pallas as pl
from jax.experimental.pallas import tpu as pltpu

D_MODEL = 1024
DEPTH = 4
PLE_DIM = 256
HEAD_DIM = 64
RMS_EPS = 1e-6
RWKV_HEADS = 8
RWKV_W = RWKV_HEADS * HEAD_DIM
DECAY_LORA = 64
ICLR_LORA = 64
GATE_LORA = 128
RWKV_IN = 3 * RWKV_W + 2 * DECAY_LORA + 2 * ICLR_LORA + GATE_LORA
GN_EPS = 64e-5
ATTN_PATTERNS = ((128, 1), (512, 4), (2048, 16))
ATTN_HEADS_PER_GROUP = 4
ATTN_HEADS = ATTN_HEADS_PER_GROUP * len(ATTN_PATTERNS)
ATTN_OUT = ATTN_HEADS_PER_GROUP * HEAD_DIM
ATTN_GROUP_IN = 3 * ATTN_OUT
ATTN_HALF = 64
ATTN_Q_ROWS = 256
ATTN_STEP_TOKENS = 2048
ROPE_THETA = 500000.0
ROT_DIM = HEAD_DIM // 4
NEG_INF = -1e30
AB_OUT = RWKV_W + ATTN_OUT
POOL_WINDOWS = (2, 4, 8, 16)
POOL_GROUP = 256
POOL_W = len(POOL_WINDOWS) * POOL_GROUP
D_FF = 2816

LANES = 128
SUBLANES = 8
HALO = SUBLANES
CHUNK = 64
MXU_DIM = 256
SCAN_GROUP_LANES = LANES
VMEM_BYTES = 64 * 1024 * 1024
VMEM_LIMIT = VMEM_BYTES // 8 * 7
assert all(win // (2 * dil) == ATTN_HALF for win, dil in ATTN_PATTERNS)
assert HALO >= max(POOL_WINDOWS) // 2

F32 = jnp.float32
BF16 = jnp.bfloat16


def _dot(a, b):
    return jnp.dot(a, b, preferred_element_type=F32)


def _dot_nt(a, b):
    return lax.dot_general(a, b, (((1,), (1,)), ((), ())), preferred_element_type=F32)


def _split(x):
    hi = x.astype(BF16)
    lo = (x - hi.astype(F32)).astype(BF16)
    return hi, lo


def _rms(x, g):
    return x * lax.rsqrt(jnp.mean(x * x, axis=-1, keepdims=True) + RMS_EPS) * g


def _sigmoid(x):
    return 1.0 / (1.0 + jnp.exp(-x))


def _head_ones():
    r = lax.broadcasted_iota(jnp.int32, (MXU_DIM, MXU_DIM), 0) // HEAD_DIM
    c = lax.broadcasted_iota(jnp.int32, (MXU_DIM, MXU_DIM), 1) // HEAD_DIM
    return jnp.where(r == c, 1.0, 0.0).astype(BF16)


def _head_sum(x, ones):
    outs = []
    for j in range(x.shape[1] // MXU_DIM):
        hi, lo = _split(x[:, j * MXU_DIM:(j + 1) * MXU_DIM])
        outs.append(_dot(hi, ones) + _dot(lo, ones))
    return jnp.concatenate(outs, axis=1)


def _cparams(sem):
    return pltpu.CompilerParams(dimension_semantics=sem, vmem_limit_bytes=VMEM_LIMIT)


def _halo_specs(tm, width, n_rows):
    nb = n_rows // HALO
    step = tm // HALO
    prev = pl.BlockSpec((HALO, width), lambda i: (jnp.maximum(i * step - 1, 0), 0))
    nxt = pl.BlockSpec((HALO, width), lambda i: (jnp.minimum((i + 1) * step, nb - 1), 0))
    return prev, nxt


def _resident(shape):
    nd = len(shape)
    return pl.BlockSpec(shape, lambda *_: (0,) * nd, pipeline_mode=pl.Buffered(1))


def _inproj_kernel(x_ref, xp_ref, xn_ref, g_ref, w_ref, cos_ref, sin_ref,
                   mu_ref, w0_ref, wu_ref, a0_ref, au_ref, gu_ref, kk_ref, ka_ref, rk_ref,
                   r_o, kk_o, v_o, g_o, bonus_o, lw0_o, b0_o, kd0_o, lw1_o, b1_o, kd1_o, q0_ref, q1_ref, q2_ref,
                   scr_ref, *, tm, nts):
    it = pl.program_id(0) % nts
    gm = g_ref[...]
    hm = _rms(x_ref[...], gm)
    hp = jnp.where(it == 0, 0.0, _rms(xp_ref[...], gm))
    hn = jnp.where(it == nts - 1, 0.0, _rms(xn_ref[...], gm))
    h = hm.astype(BF16)
    z_ext = _dot(jnp.concatenate([hp, hm, hn], axis=0).astype(BF16), w_ref[:, :RWKV_IN])
    t_attn = [_dot(h, w_ref[:, RWKV_IN + g * ATTN_GROUP_IN:RWKV_IN + (g + 1) * ATTN_GROUP_IN])
              for g in range(len(ATTN_PATTERNS))]
    cs = cos_ref[...]
    sn = sin_ref[...]
    lane = lax.broadcasted_iota(jnp.int32, (tm, LANES), 1)
    first = (lane & (HEAD_DIM - 1)) < (ROT_DIM // 2)
    outs = (q0_ref, q1_ref, q2_ref)
    for g, (_, dil) in enumerate(ATTN_PATTERNS):
        t = t_attn[g]
        for c in range(ATTN_GROUP_IN // LANES):
            tc = t[:, c * LANES:(c + 1) * LANES]
            if c < 2 * ATTN_OUT // LANES:
                rot = jnp.where(first, pltpu.roll(tc, LANES - ROT_DIM // 2, 1), pltpu.roll(tc, ROT_DIM // 2, 1))
                tc = tc * cs + rot * sn
            if dil == 1:
                outs[g][0, :, c * LANES:(c + 1) * LANES] = tc
            else:
                scr_ref[g, c] = tc
        if dil > 1:
            for c in range(ATTN_GROUP_IN // LANES):
                for r in range(dil):
                    outs[g][r, :, c * LANES:(c + 1) * LANES] = scr_ref[g, c, pl.ds(r, tm // dil, stride=dil), :]
    mid = slice(HALO, HALO + tm)
    z = z_ext[mid]
    zs = 0.5 * (pltpu.roll(z_ext, 1, 0)[mid] + pltpu.roll(z_ext, tm + 2 * HALO - 1, 0)[mid])
    zz = z + mu_ref[...] * (zs - z)
    W = RWKV_W
    r = zz[:, 0:W]
    k = zz[:, W:2 * W]
    v = zz[:, 2 * W:3 * W]
    o = 3 * W
    wd = jnp.tanh(zz[:, o:o + 2 * DECAY_LORA])
    o += 2 * DECAY_LORA
    ad = zz[:, o:o + 2 * ICLR_LORA]
    o += 2 * ICLR_LORA
    gd = _sigmoid(zz[:, o:])
    wlog = w0_ref[...] + _dot(wd.astype(BF16), wu_ref[...])
    aa = _sigmoid(a0_ref[...] + _dot(ad.astype(BF16), au_ref[...]))
    g_o[...] = _dot(gd.astype(BF16), gu_ref[...]).astype(g_o.dtype)
    ones = _head_ones()
    kkv = k * kk_ref[...]
    kkn = kkv / jnp.sqrt(_head_sum(kkv * kkv, ones) + 1e-12)
    r_o[...] = r
    kk_o[...] = kkn
    v_o[...] = v
    ka = ka_ref[...]
    kd_sum = None
    for d, (lw_o, b_o, kd_o) in enumerate(((lw0_o, b0_o, kd0_o), (lw1_o, b1_o, kd1_o))):
        a_d = aa[:, d * W:(d + 1) * W]
        lw_o[...] = -math.exp(-0.5) * _sigmoid(wlog[:, d * W:(d + 1) * W])
        kd = k * (1.0 + (a_d - 1.0) * ka)
        kd_o[...] = kd
        b_o[...] = kkn * a_d
        kd_sum = kd if kd_sum is None else kd_sum + kd
    bonus_o[...] = (_head_sum(r * kd_sum * rk_ref[...], ones) * v).astype(bonus_o.dtype)


def _inproj(x2, g, w, cos_t, sin_t, prm, j, B, S, tm):
    T = B * S
    nts = S // tm
    W = RWKV_W
    vec = lambda a: a.reshape(1, -1)

    def bdiag(m):
        L = m.shape[1]
        return jnp.zeros((2 * L, 2 * W), F32).at[:L, :W].set(m[0]).at[L:, W:].set(m[1]).astype(BF16)

    params = [vec(prm['rwkv_mu'][j]), vec(prm['rwkv_w0'][j]), bdiag(prm['rwkv_w_up'][j]),
              vec(prm['rwkv_a0'][j]), bdiag(prm['rwkv_a_up'][j]), prm['rwkv_g_up'][j].astype(BF16),
              vec(prm['rwkv_k_k'][j]), vec(prm['rwkv_k_a'][j]), vec(prm['rwkv_r_k'][j])]
    prev, nxt = _halo_specs(tm, D_MODEL, T)
    row = pl.BlockSpec((tm, W), lambda i: (i, 0))
    out_shape = [jax.ShapeDtypeStruct((T, W), BF16 if i in (3, 4) else F32) for i in range(11)]
    out_specs = [row] * 11
    for _, dil in ATTN_PATTERNS:
        out_shape.append(jax.ShapeDtypeStruct((B, dil, S // dil, ATTN_GROUP_IN), F32))
        out_specs.append(pl.BlockSpec((None, dil, tm // dil, ATTN_GROUP_IN),
                                      lambda i: (i // nts, 0, i % nts, 0)))
    return pl.pallas_call(
        functools.partial(_inproj_kernel, tm=tm, nts=nts),
        grid=(T // tm,),
        in_specs=[pl.BlockSpec((tm, D_MODEL), lambda i: (i, 0)), prev, nxt,
                  _resident((1, D_MODEL)),
                  _resident(w.shape),
                  pl.BlockSpec((tm, LANES), lambda i: (i % nts, 0)),
                  pl.BlockSpec((tm, LANES), lambda i: (i % nts, 0))] + [_resident(a.shape) for a in params],
        out_specs=out_specs,
        out_shape=out_shape,
        scratch_shapes=[pltpu.VMEM((len(ATTN_PATTERNS), ATTN_GROUP_IN // LANES, tm, LANES), F32)],
        compiler_params=_cparams(("parallel",)),
        name="inproj",
    )(x2, x2, x2, g, w, cos_t, sin_t, *params)


def _stack2(x):
    head = lax.broadcasted_iota(jnp.int32, x.shape, 1) // HEAD_DIM
    zero = jnp.zeros_like(x)
    return jnp.concatenate([jnp.where(head == h, x, zero) for h in range(x.shape[1] // HEAD_DIM)], axis=0)


def _pm(m, x):
    return _dot(m.astype(BF16), _stack2(x.astype(BF16)))


def _wkv_scaled(r, kk, v, lw, b, kd, fwd):
    C = CHUNK
    ti = lax.broadcasted_iota(jnp.int32, (C, C), 0)
    tj = lax.broadcasted_iota(jnp.int32, (C, C), 1)
    cum = jnp.where((tj <= ti) if fwd else (tj >= ti), 1.0, 0.0).astype(BF16)
    lh, ll = _split(lw)
    G = _dot(cum, lh) + _dot(cum, ll)
    g_tot = G[C - 1:C, :] if fwd else G[0:1, :]
    e_neg = jnp.exp(-G)
    rt = r * jnp.exp(G)
    at = -kk * jnp.exp(G - lw)
    kt = kd * e_neg
    bt = b * e_neg
    e_tot = jnp.exp(g_tot)
    kb = jnp.concatenate([kt * e_tot, bt * e_tot], axis=0)
    return dict(at=at, rt=rt, kt=kt, bt=bt, kb=kb, v=v, e_tot=e_tot, fwd=fwd)


def _wkv_pairs(chains):
    C = CHUNK
    hw = chains[0]['at'].shape[1]
    gw = hw // HEAD_DIM * C
    trow = lax.broadcasted_iota(jnp.int32, (C, gw), 0)
    tcol = lax.broadcasted_iota(jnp.int32, (C, gw), 1) & (C - 1)
    eye = jnp.where(tcol == trow, 1.0, 0.0)
    hr = lax.broadcasted_iota(jnp.int32, (hw, hw), 0) // HEAD_DIM
    hc = lax.broadcasted_iota(jnp.int32, (hw, hw), 1) // HEAD_DIM
    for ch in chains:
        ch['x'] = jnp.concatenate([ch['at'], ch['rt']], axis=0).astype(BF16)
    for ch in chains:
        kb2 = jnp.concatenate([_stack2(ch['kt'].astype(BF16)), _stack2(ch['bt'].astype(BF16))], axis=0)
        sc = _dot_nt(ch['x'], kb2)
        ch['sc_k'] = sc[:, :gw]
        ch['sc_b'] = sc[:, gw:]
        ch['p1'] = _dot_nt(ch['x'], ch['S'].astype(BF16))
    for ch in chains:
        strict = (tcol < trow) if ch['fwd'] else (tcol > trow)
        incl = (tcol <= trow) if ch['fwd'] else (tcol >= trow)
        ch['a_kk'] = jnp.concatenate([jnp.where(strict, ch['sc_k'][:C], 0.0),
                                      jnp.where(incl, ch['sc_k'][C:], 0.0)], axis=0)
        ch['n'] = jnp.where(strict, ch['sc_b'][:C], 0.0)
        ch['a_rb'] = jnp.where(incl, ch['sc_b'][C:], 0.0)
        ch['t'] = eye + ch['n']
    for j in range(6):
        for ch in chains:
            n = ch['n']
            if j == 0:
                ch['n'] = _pm(n, n)
                ch['p2'] = _pm(ch['a_kk'], ch['v'])
            elif j < 5:
                m = _pm(jnp.concatenate([ch['t'], n], axis=0), n)
                ch['t'] = ch['t'] + m[:C]
                ch['n'] = m[C:]
            else:
                ch['t'] = ch['t'] + _pm(ch['t'], n)
    for ch in chains:
        ch['u'] = _pm(ch['t'], ch['p1'][:C] + ch['p2'][:C])
    for ch in chains:
        ch['y'] = ch['p1'][C:] + ch['p2'][C:] + _pm(ch['a_rb'], ch['u'])
        vu_t = jnp.concatenate([ch['v'], ch['u']], axis=0).T
        upd = _dot(vu_t.astype(BF16), ch['kb'].astype(BF16))
        ch['s_new'] = ch['S'] * ch['e_tot'] + jnp.where(hr == hc, upd, 0.0)


def _scan_kernel(rf, kkf, vf, lwf, bf, kdf, rb, kkb, vb, lwb, bb, kdb, yf_o, yb_o, sf_ref, sb_ref, *, nc, nb):
    @pl.when(pl.program_id(1) == 0)
    def _():
        sf_ref[...] = jnp.zeros_like(sf_ref)
        sb_ref[...] = jnp.zeros_like(sb_ref)

    npairs = RWKV_W // SCAN_GROUP_LANES

    def body(c, carry):
        slf = pl.ds(pl.multiple_of(c * CHUNK, CHUNK), CHUNK)
        slb = pl.ds(pl.multiple_of((nc - 1 - c) * CHUNK, CHUNK), CHUNK)
        chains = []
        for bi in range(nb):
            dirs = (_wkv_scaled(*[ref[bi, slf, :] for ref in (rf, kkf, vf, lwf, bf, kdf)], True),
                    _wkv_scaled(*[ref[bi, slb, :] for ref in (rb, kkb, vb, lwb, bb, kdb)], False))
            for d, s_ref in zip(dirs, (sf_ref, sb_ref)):
                for p in range(npairs):
                    sl = slice(p * SCAN_GROUP_LANES, (p + 1) * SCAN_GROUP_LANES)
                    ch = {k: (val if k == 'fwd' else val[:, sl]) for k, val in d.items()}
                    ch['S'] = s_ref[bi * npairs + p]
                    chains.append(ch)
        _wkv_pairs(chains)
        chains = iter(chains)
        for bi in range(nb):
            for s_ref, y_o, sl in ((sf_ref, yf_o, slf), (sb_ref, yb_o, slb)):
                ys = []
                for p in range(npairs):
                    ch = next(chains)
                    s_ref[bi * npairs + p] = ch['s_new']
                    ys.append(ch['y'])
                y_o[bi, sl, :] = jnp.concatenate(ys, axis=1).astype(y_o.dtype)
        return carry

    lax.fori_loop(0, nc, body, 0)


def _scan(r, kk, v, lw0, b0, kd0, lw1, b1, kd1, B, S, tb, nb):
    assert B % nb == 0 and S % tb == 0 and tb % CHUNK == 0
    nblk = S // tb
    shp = (B, S, RWKV_W)
    args = [a.reshape(shp) for a in (r, kk, v, lw0, b0, kd0, r, kk, v, lw1, b1, kd1)]
    fwd = pl.BlockSpec((nb, tb, RWKV_W), lambda bi, j: (bi, j, 0))
    bwd = pl.BlockSpec((nb, tb, RWKV_W), lambda bi, j: (bi, nblk - 1 - j, 0))
    yf, yb = pl.pallas_call(
        functools.partial(_scan_kernel, nc=tb // CHUNK, nb=nb),
        grid=(B // nb, nblk),
        in_specs=[fwd] * 6 + [bwd] * 6,
        out_specs=[fwd, bwd],
        out_shape=[jax.ShapeDtypeStruct(shp, BF16)] * 2,
        scratch_shapes=[pltpu.VMEM((nb * RWKV_W // SCAN_GROUP_LANES, SCAN_GROUP_LANES, SCAN_GROUP_LANES), F32)] * 2,
        compiler_params=_cparams(("parallel", "arbitrary")),
        name="wkv_scan",
    )(*args)
    return yf.reshape(B * S, RWKV_W), yb.reshape(B * S, RWKV_W)


def _attn_kernel(band_ref, q_ref, km_ref, vm_ref, kp_ref, vp_ref, kn_ref, vn_ref, o_ref, lse_ref,
                 *, Q, Qs, L, rpb, ru):
    j = pl.program_id(2)
    Wn = Qs + 2 * ATTN_HALF
    heads = range(ATTN_HEADS_PER_GROUP)
    lane = lax.broadcasted_iota(jnp.int32, (Qs, ATTN_OUT), 1) // HEAD_DIM
    biases = []
    for sb in range(Q // Qs):
        kpos = j * Q + sb * Qs - ATTN_HALF + lax.broadcasted_iota(jnp.int32, (1, Wn), 1)
        biases.append(band_ref[...] + jnp.where(jnp.minimum(kpos, L - 1 - kpos) >= 0, 0.0, NEG_INF))

    def body(it, carry):
        units = []
        for rr in range(ru):
            r = it * ru + rr
            k_all = jnp.concatenate([kp_ref[r], km_ref[r], kn_ref[r]], axis=0).astype(BF16)
            v_all = jnp.concatenate([vp_ref[r], vm_ref[r], vn_ref[r]], axis=0).astype(BF16)
            for sb in range(Q // Qs):
                units.append(dict(r=r, rows=pl.ds(sb * Qs, Qs), bias=biases[sb],
                                  kw=k_all[sb * Qs:sb * Qs + Wn], vw=v_all[sb * Qs:sb * Qs + Wn]))
        for un in units:
            q = q_ref[un['r'], un['rows'], :] * (HEAD_DIM ** -0.5)
            un['s'] = [_dot_nt(jnp.where(lane == hh, q, 0.0).astype(BF16), un['kw']) + un['bias'] for hh in heads]
        for un in units:
            un['m'] = [jnp.max(s, axis=-1, keepdims=True) for s in un['s']]
            un['p'] = [jnp.exp(s - m) for s, m in zip(un['s'], un['m'])]
            un['den'] = [jnp.sum(p, axis=-1, keepdims=True) for p in un['p']]
        for un in units:
            un['o'] = [_dot(p.astype(BF16), un['vw']) for p in un['p']]
        for un in units:
            o_acc = jnp.zeros((Qs, ATTN_OUT), F32)
            l_acc = jnp.zeros((Qs, ATTN_OUT), F32)
            for hh in heads:
                o_acc = jnp.where(lane == hh, un['o'][hh] / un['den'][hh], o_acc)
                l_acc = jnp.where(lane == hh, un['m'][hh] + jnp.log(un['den'][hh]), l_acc)
            o_ref[un['r'], un['rows'], :] = o_acc
            lse_ref[un['r'], un['rows'], :] = l_acc
        return carry

    lax.fori_loop(0, rpb // ru, body, 0)


def _attn(qkv, B, S, dil):
    L = S // dil
    Qs = min(ATTN_Q_ROWS, L)
    nsub, ru = (2, 1) if dil == 1 else (1, 2)
    Q = Qs * nsub
    rpb = min(dil, ATTN_STEP_TOKENS // Q)
    assert L % Q == 0 and dil % rpb == 0 and rpb % ru == 0
    nq = L // Q
    qb = Q // ATTN_HALF
    nhb = L // ATTN_HALF
    Wn = Qs + 2 * ATTN_HALF
    rel = jnp.arange(Wn)[None, :] - ATTN_HALF - jnp.arange(Qs)[:, None]
    band = jnp.where(jnp.abs(rel) <= ATTN_HALF, 0.0, NEG_INF).astype(F32)

    def mid(col):
        return pl.BlockSpec((None, rpb, Q, ATTN_OUT), lambda bi, rb, j: (bi, rb, j, col))

    def prev(col):
        return pl.BlockSpec((None, rpb, ATTN_HALF, ATTN_OUT),
                            lambda bi, rb, j: (bi, rb, jnp.maximum(j * qb - 1, 0), col))

    def nxt(col):
        return pl.BlockSpec((None, rpb, ATTN_HALF, ATTN_OUT),
                            lambda bi, rb, j: (bi, rb, jnp.minimum((j + 1) * qb, nhb - 1), col))

    out = pl.BlockSpec((None, rpb, Q, ATTN_OUT), lambda bi, rb, j: (bi, rb, j, 0))
    return pl.pallas_call(
        functools.partial(_attn_kernel, Q=Q, Qs=Qs, L=L, rpb=rpb, ru=ru),
        grid=(B, dil // rpb, nq),
        in_specs=[_resident(band.shape), mid(0), mid(1), mid(2), prev(1), prev(2), nxt(1), nxt(2)],
        out_specs=[out, out],
        out_shape=[jax.ShapeDtypeStruct((B, dil, L, ATTN_OUT), F32)] * 2,
        compiler_params=_cparams(("parallel", "parallel", "parallel")),
        name="band_attn_d%d" % dil,
    )(band, qkv, qkv, qkv, qkv, qkv, qkv, qkv)


def _post_kernel(x_ref, yf_ref, yb_ref, bonus_ref, g_ref, o0, l0, o1, l1, o2, l2, lng_ref, lnb_ref, wo_ref,
                 out_ref, oscr, lscr, *, tm):
    ones = _head_ones()
    y = yf_ref[...].astype(F32) + yb_ref[...].astype(F32)
    yc = y - _head_sum(y, ones) * (1.0 / HEAD_DIM)
    var = _head_sum(yc * yc, ones) * (1.0 / HEAD_DIM)
    yn = yc * lax.rsqrt(var + GN_EPS) * lng_ref[...] + lnb_ref[...]
    ya = (yn + bonus_ref[...].astype(F32)) * g_ref[...].astype(F32)
    nlb = ATTN_OUT // LANES
    os_, ls = [], []
    for gi, (o_r, l_r, (_, dil)) in enumerate(zip((o0, o1, o2), (l0, l1, l2), ATTN_PATTERNS)):
        if dil == 1:
            os_.append(o_r[0])
            ls.append(l_r[0])
            continue
        for c in range(nlb):
            for r in range(dil):
                oscr[gi * nlb + c, pl.ds(r, tm // dil, stride=dil), :] = o_r[r, :, c * LANES:(c + 1) * LANES]
                lscr[gi * nlb + c, pl.ds(r, tm // dil, stride=dil), :] = l_r[r, :, c * LANES:(c + 1) * LANES]
        os_.append(jnp.concatenate([oscr[gi * nlb + c] for c in range(nlb)], axis=1))
        ls.append(jnp.concatenate([lscr[gi * nlb + c] for c in range(nlb)], axis=1))
    m = jnp.maximum(jnp.maximum(ls[0], ls[1]), ls[2])
    es = [jnp.exp(l - m) for l in ls]
    num = es[0] * os_[0] + es[1] * os_[1] + es[2] * os_[2]
    yb = num / (es[0] + es[1] + es[2])
    out_ref[...] = (x_ref[...] + _dot(ya.astype(BF16), wo_ref[:RWKV_W, :])
                    + _dot(yb.astype(BF16), wo_ref[RWKV_W:, :]))


def _post(x2, yf, yb, bonus, g, attn, lng, lnb, wo, B, S, tm):
    T = B * S
    nts = S // tm
    row = lambda w: pl.BlockSpec((tm, w), lambda i: (i, 0))
    in_specs = [row(D_MODEL)] + [row(RWKV_W)] * 4
    args = [x2, yf, yb, bonus, g]
    for (o, l), (_, dil) in zip(attn, ATTN_PATTERNS):
        spec = pl.BlockSpec((None, dil, tm // dil, ATTN_OUT), lambda i: (i // nts, 0, i % nts, 0))
        in_specs += [spec, spec]
        args += [o, l]
    in_specs += [_resident(lng.shape), _resident(lnb.shape), _resident(wo.shape)]
    args += [lng, lnb, wo]
    ng = len(ATTN_PATTERNS)
    return pl.pallas_call(
        functools.partial(_post_kernel, tm=tm),
        grid=(T // tm,),
        in_specs=in_specs,
        out_specs=row(D_MODEL),
        out_shape=jax.ShapeDtypeStruct((T, D_MODEL), F32),
        scratch_shapes=[pltpu.VMEM((ng * ATTN_OUT // LANES, tm, LANES), F32)] * 2,
        compiler_params=_cparams(("parallel",)),
        name="mixer_ab_out",
    )(*args)


def _window_sum(ue, rad):
    n = ue.shape[0]
    mid = slice(HALO, n - HALO)
    up = lambda a, k: pltpu.roll(a, n - k, 0)
    down = lambda a, k: pltpu.roll(a, k, 0)
    if rad < 4:
        ws = ue[mid]
        for d in range(1, rad + 1):
            ws = ws + (down(ue, d)[mid] + up(ue, d)[mid])
        return ws
    run, width = ue, 1
    while width < 2 * rad:
        run = run + up(run, width)
        width *= 2
    return down(run, rad)[mid] + up(ue, rad)[mid]


def _pool_kernel(x_ref, xp_ref, xn_ref, g_ref, win_ref, wg_ref, sc_ref, wo_ref, out_ref, *, tm, nts, S):
    it = pl.program_id(0) % nts
    x = x_ref[...]
    g = g_ref[...]
    hp = jnp.where(it == 0, 0.0, _rms(xp_ref[...], g))
    hn = jnp.where(it == nts - 1, 0.0, _rms(xn_ref[...], g))
    h = jnp.concatenate([hp, _rms(x, g), hn], axis=0).astype(BF16)
    u = _dot(h, win_ref[...])
    pos = it * tm + lax.broadcasted_iota(jnp.int32, (tm, 1), 0)
    acc = x
    for gi, win in enumerate(POOL_WINDOWS):
        rad = win // 2
        cols = slice(gi * POOL_GROUP, (gi + 1) * POOL_GROUP)
        ue = u[:, cols]
        cnt = (jnp.minimum(pos + rad + 1, S) - jnp.maximum(pos - rad, 0)).astype(F32)
        dlt = (_window_sum(ue, rad) / cnt - ue[HALO:HALO + tm]).astype(BF16)
        yg = _dot(dlt, wg_ref[gi]) * sc_ref[:, cols]
        acc = acc + _dot(yg.astype(BF16), wo_ref[cols, :])
    out_ref[...] = acc


def _pool(x2, g, w_in, w_group, scale, w_out, B, S, tm):
    T = B * S
    nts = S // tm
    prev, nxt = _halo_specs(tm, D_MODEL, T)
    row = pl.BlockSpec((tm, D_MODEL), lambda i: (i, 0))
    return pl.pallas_call(
        functools.partial(_pool_kernel, tm=tm, nts=nts, S=S),
        grid=(T // tm,),
        in_specs=[row, prev, nxt] + [_resident(a.shape) for a in (g, w_in, w_group, scale, w_out)],
        out_specs=row,
        out_shape=jax.ShapeDtypeStruct((T, D_MODEL), F32),
        compiler_params=_cparams(("parallel",)),
        name="pool_mixer",
    )(x2, x2, x2, g, w_in, w_group, scale, w_out)


def _gelu(x):
    c = math.sqrt(2.0 / math.pi)
    hx = 0.5 * x
    return hx + hx * jnp.tanh(x * (c + (c * 0.044715) * (x * x)))


def _ffn_kernel(x_ref, xp_ref, xn_ref, p_ref, gf_ref, wup_ref, cw_ref, cb_ref, wd_ref,
                gp_ref, wpg_ref, bpg_ref, wpp_ref, gfin_ref, out_ref, *, tm, rb, nts, final):
    it = pl.program_id(0) % nts
    g = gf_ref[...]
    n_ext = rb + 2 * HALO
    nsub = tm // rb
    for s in range(nsub):
        r0 = s * rb
        top = jnp.where(it == 0, 0.0, xp_ref[...]) if s == 0 else x_ref[pl.ds(r0 - HALO, HALO), :]
        bot = jnp.where(it == nts - 1, 0.0, xn_ref[...]) if s == nsub - 1 else x_ref[pl.ds(r0 + rb, HALO), :]
        h = _rms(jnp.concatenate([top, x_ref[pl.ds(r0, rb), :], bot], axis=0), g).astype(BF16)
        u = _dot(h, wup_ref[...])
        mid = slice(HALO, HALO + rb)
        c = (pltpu.roll(u, 1, 0)[mid] * cw_ref[0:1, :] + u[mid] * cw_ref[1:2, :]
             + pltpu.roll(u, n_ext - 1, 0)[mid] * cw_ref[2:3, :] + cb_ref[...])
        act = _gelu(c[:, :D_FF]) * c[:, D_FF:]
        rows = pl.ds(r0, rb)
        x2 = x_ref[rows, :] + _dot(act.astype(BF16), wd_ref[...])
        h3 = _rms(x2, gp_ref[...]).astype(BF16)
        gate = _sigmoid(_dot(h3, wpg_ref[...]) + bpg_ref[...])
        x3 = x2 + gate * _dot(p_ref[rows, :].astype(BF16), wpp_ref[...])
        out_ref[rows, :] = _rms(x3, gfin_ref[...]) if final else x3


def _ffn(x2, p_all, layer, w, B, S, tm, rb, final):
    T = B * S
    nt = T // tm
    nts = S // tm
    prev, nxt = _halo_specs(tm, D_MODEL, T)
    row = pl.BlockSpec((tm, D_MODEL), lambda i: (i, 0))
    names = ('gf', 'wup', 'cw', 'cb', 'wd', 'gp', 'wpg', 'bpg', 'wpp', 'gfin')
    in_specs = [row, prev, nxt, pl.BlockSpec((tm, PLE_DIM), lambda i: (layer * nt + i, 0))]
    in_specs += [_resident(w[k].shape) for k in names]
    return pl.pallas_call(
        functools.partial(_ffn_kernel, tm=tm, rb=rb, nts=nts, final=final),
        grid=(nt,),
        in_specs=in_specs,
        out_specs=row,
        out_shape=jax.ShapeDtypeStruct((T, D_MODEL), F32),
        compiler_params=_cparams(("parallel",)),
        name="convffn_ple",
    )(x2, x2, x2, p_all, *[w[k] for k in names])


def _rope_tables(S):
    half = ROT_DIM // 2
    inv = jnp.float32(ROPE_THETA) ** (-jnp.arange(half, dtype=F32) * 2.0 / ROT_DIM)
    ang = jnp.arange(S, dtype=F32)[:, None] * inv[None, :]
    cos = jnp.cos(ang)
    sin = jnp.sin(ang)
    pad1 = jnp.ones((S, HEAD_DIM - ROT_DIM), F32)
    pad0 = jnp.zeros((S, HEAD_DIM - ROT_DIM), F32)
    cos_h = jnp.concatenate([cos, cos, pad1], axis=1)
    sin_h = jnp.concatenate([-sin, sin, pad0], axis=1)
    reps = LANES // HEAD_DIM
    return jnp.tile(cos_h, (1, reps)), jnp.tile(sin_h, (1, reps))


def _pack_ab_w_in(w):
    parts = [w[:, :RWKV_IN]]
    qkv = [w[:, RWKV_IN + s * ATTN_HEADS * HEAD_DIM: RWKV_IN + (s + 1) * ATTN_HEADS * HEAD_DIM] for s in range(3)]
    for g in range(len(ATTN_PATTERNS)):
        parts += [t[:, g * ATTN_OUT:(g + 1) * ATTN_OUT] for t in qkv]
    return jnp.concatenate(parts, axis=1).astype(BF16)


def _prepare(prm):
    vec = lambda a: a.reshape(1, -1).astype(F32)
    pk = {'ab_w_in': [_pack_ab_w_in(prm['ab_w_in'][j]) for j in range(prm['ab_w_in'].shape[0])],
          'ab_w_out': prm['ab_w_out'].astype(BF16),
          'c_w_in': prm['c_w_in'].astype(BF16), 'c_w_group': prm['c_w_group'].astype(BF16),
          'c_w_out': prm['c_w_out'].astype(BF16), 'ffn': []}
    for i in range(DEPTH):
        pk['ffn'].append({
            'gf': vec(prm['norm_ffn_g'][i]), 'wup': prm['ffn_w_up'][i].astype(BF16),
            'cw': prm['ffn_conv_w'][i], 'cb': vec(prm['ffn_conv_b'][i]),
            'wd': prm['ffn_w_down'][i].astype(BF16), 'gp': vec(prm['norm_ple_g'][i]),
            'wpg': prm['ple_w_gate'][i].astype(BF16), 'bpg': vec(prm['ple_b_gate'][i]),
            'wpp': prm['ple_w_proj'][i].astype(BF16), 'gfin': vec(prm['norm_final_g'])})
    return pk


def _trunk(x, p, prm, pk, tiles):
    B, S, _ = x.shape
    T = B * S
    vec = lambda a: a.reshape(1, -1)
    x2 = x.reshape(T, D_MODEL)
    p_all = p.reshape(p.shape[0] * T, PLE_DIM)
    cos_t, sin_t = _rope_tables(S)
    for i in range(DEPTH):
        j = i // 2
        gmix = vec(prm['norm_mix_g'][i])
        if i % 2 == 0:
            (r, kk, v, g, bonus, lw0, b0, kd0, lw1, b1, kd1, q0, q1, q2) = _inproj(
                x2, gmix, pk['ab_w_in'][j], cos_t, sin_t, prm, j, B, S, tiles['inproj'])
            nb = min(B, tiles['scan_b'])
            yf, yb = _scan(r, kk, v, lw0, b0, kd0, lw1, b1, kd1, B, S, tiles['scan_rows'] // nb, nb)
            attn = [_attn(q, B, S, dil) for q, (_, dil) in zip((q0, q1, q2), ATTN_PATTERNS)]
            x2 = _post(x2, yf, yb, bonus, g, attn, vec(prm['rwkv_ln_g'][j]), vec(prm['rwkv_ln_b'][j]),
                       pk['ab_w_out'][j], B, S, tiles['post'])
        else:
            x2 = _pool(x2, gmix, pk['c_w_in'][j], pk['c_w_group'][j], vec(prm['c_scale'][j]), pk['c_w_out'][j],
                       B, S, tiles['pool'])
        x2 = _ffn(x2, p_all, i, pk['ffn'][i], B, S, tiles['ffn_m'], tiles['ffn_rb'], final=(i == DEPTH - 1))
    return x2.reshape(B, S, D_MODEL)


_TILES = {'inproj': 256, 'scan_rows': 512, 'scan_b': 4, 'post': 512, 'pool': 1024, 'ffn_m': 1024, 'ffn_rb': 512}


def kernel(x_prompt, x_sample, p_prompt, p_sample, ab_w_in, ab_w_out, rwkv_mu, rwkv_w0, rwkv_w_up, rwkv_a0, rwkv_a_up, rwkv_g_up, rwkv_k_k, rwkv_k_a, rwkv_r_k, rwkv_ln_g, rwkv_ln_b, c_w_in, c_w_group, c_scale, c_w_out, norm_mix_g, norm_ffn_g, norm_ple_g, norm_final_g, ffn_w_up, ffn_conv_w, ffn_conv_b, ffn_w_down, ple_w_proj, ple_w_gate, ple_b_gate):
    prm = {
        'ab_w_in': ab_w_in, 'ab_w_out': ab_w_out, 'rwkv_mu': rwkv_mu, 'rwkv_w0': rwkv_w0,
        'rwkv_w_up': rwkv_w_up, 'rwkv_a0': rwkv_a0, 'rwkv_a_up': rwkv_a_up, 'rwkv_g_up': rwkv_g_up,
        'rwkv_k_k': rwkv_k_k, 'rwkv_k_a': rwkv_k_a, 'rwkv_r_k': rwkv_r_k, 'rwkv_ln_g': rwkv_ln_g,
        'rwkv_ln_b': rwkv_ln_b, 'c_w_in': c_w_in, 'c_w_group': c_w_group, 'c_scale': c_scale,
        'c_w_out': c_w_out, 'norm_mix_g': norm_mix_g, 'norm_ffn_g': norm_ffn_g, 'norm_ple_g': norm_ple_g,
        'norm_final_g': norm_final_g, 'ffn_w_up': ffn_w_up, 'ffn_conv_w': ffn_conv_w,
        'ffn_conv_b': ffn_conv_b, 'ffn_w_down': ffn_w_down, 'ple_w_proj': ple_w_proj,
        'ple_w_gate': ple_w_gate, 'ple_b_gate': ple_b_gate,
    }
    pk = _prepare(prm)
    y_prompt = _trunk(x_prompt, p_prompt, prm, pk, _TILES)
    y_sample = _trunk(x_sample, p_sample, prm, pk, _TILES)
    return (y_prompt, y_sample)
```
